```python
import jax, jax.numpy as jnp
from jax import lax
import numpy as np

D_MODEL = 1024
BATCH = 4
SEQ = 4096
DEPTH = 1

D_MIX = D_MODEL
D_FOX = D_MIX // 2
D_RET = D_MIX - D_FOX
FOX_HEADS = 8
FOX_HEAD_DIM = D_FOX // FOX_HEADS
RET_HEADS = 4
RET_HEAD_DIM = D_RET // RET_HEADS
D_FF = 2816
BLOCK_Q = 128
RET_CHUNK = 128
ROPE_BASE = 10000.0
LN_EPS = 1e-5
N_MOD = 9
DEEPNORM_ALPHA = (2.0 * DEPTH) ** 0.25
DEEPNORM_BETA = (8.0 * DEPTH) ** -0.25
FFN_RES_WEIGHT = 0.5
SPLITS = [D_FOX, 2 * D_FOX, 3 * D_FOX, 3 * D_FOX + FOX_HEADS,
          3 * D_FOX + FOX_HEADS + D_RET, 3 * D_FOX + FOX_HEADS + 2 * D_RET,
          3 * D_FOX + FOX_HEADS + 3 * D_RET]
D_IN_PROJ = 3 * D_FOX + FOX_HEADS + 4 * D_RET

kernel_name = "fox_retnet_hymba_macaron_deepnorm_adaln"


def _layer_norm(x, g, b):
    xf = x.astype(jnp.float32)
    mu = xf.mean(-1, keepdims=True)
    var = jnp.square(xf - mu).mean(-1, keepdims=True)
    return ((xf - mu) * lax.rsqrt(var + LN_EPS)).astype(x.dtype) * g + b


def _modulate(x, shift, scale):
    return x * (1.0 + scale[:, None, :]) + shift[:, None, :]


def _swiglu(h, w_gate, w_up, w_down):
    return (jax.nn.silu(h @ w_gate) * (h @ w_up)) @ w_down


def _heads(t, n_heads):
    B, S, _ = t.shape
    return t.reshape(B, S, n_heads, -1).transpose(0, 2, 1, 3)


def _merge_heads(t):
    B, H, S, Dh = t.shape
    return t.transpose(0, 2, 1, 3).reshape(B, S, H * Dh)


def _rotary(t):
    S, Dk = t.shape[2], t.shape[3]
    half = Dk // 2
    inv_freq = ROPE_BASE ** (-jnp.arange(half, dtype=jnp.float32) / half)
    ang = jnp.arange(S, dtype=jnp.float32)[:, None] * inv_freq[None, :]
    cos = jnp.cos(ang).astype(t.dtype)
    sin = jnp.sin(ang).astype(t.dtype)
    t1, t2 = t[..., :half], t[..., half:]
    return jnp.concatenate([t1 * cos - t2 * sin, t1 * sin + t2 * cos], axis=-1)


def _forgetting_attention(q, k, v, log_f):
    B, H, S, Dh = q.shape
    cum = jnp.cumsum(log_f, axis=-1)
    scale = Dh ** -0.5
    kpos = jnp.arange(S)
    n_blocks = S // BLOCK_Q

    def one_block(i):
        start = i * BLOCK_Q
        qb = lax.dynamic_slice_in_dim(q, start, BLOCK_Q, axis=2)
        cb = lax.dynamic_slice_in_dim(cum, start, BLOCK_Q, axis=2)
        s = jnp.einsum('bhqd,bhkd->bhqk', qb, k).astype(jnp.float32) * scale
        s = s + cb[..., :, None] - cum[..., None, :]
        qpos = start + jnp.arange(BLOCK_Q)
        s = jnp.where(kpos[None, :] <= qpos[:, None], s, -jnp.inf)
        p = jax.nn.softmax(s, axis=-1).astype(v.dtype)
        return jnp.einsum('bhqk,bhkd->bhqd', p, v)

    out = lax.map(one_block, jnp.arange(n_blocks))
    return out.transpose(1, 2, 0, 3, 4).reshape(B, H, S, Dh)


def _retention_chunkwise(q, k, v):
    B, H, S, Dk = q.shape
    Dv = v.shape[-1]
    C = RET_CHUNK
    n = S // C
    log_gamma = jnp.log1p(-jnp.power(2.0, -5.0 - jnp.arange(H, dtype=jnp.float32)))
    idx = jnp.arange(C, dtype=jnp.float32)
    diff = idx[:, None] - idx[None, :]
    intra_decay = jnp.where(diff >= 0,
                            jnp.exp(log_gamma[:, None, None] * jnp.maximum(diff, 0.0)), 0.0)
    q_decay = jnp.exp(log_gamma[:, None] * (idx + 1.0))[..., None]
    k_decay = jnp.exp(log_gamma[:, None] * (C - 1.0 - idx))[..., None]
    chunk_decay = jnp.exp(log_gamma * C)[:, None, None]

    def to_chunks(t):
        return t.reshape(B, H, n, C, t.shape[-1]).transpose(2, 0, 1, 3, 4)

    def step(state, inp):
        qi, ki, vi = inp
        s = jnp.einsum('bhid,bhjd->bhij', qi, ki) * intra_decay
        o = jnp.einsum('bhij,bhjv->bhiv', s, vi) + jnp.einsum('bhid,bhdv->bhiv', qi * q_decay, state)
        new_state = state * chunk_decay + jnp.einsum('bhjd,bhjv->bhdv', ki * k_decay, vi)
        return new_state, o

    state0 = jnp.zeros((B, H, Dk, Dv), jnp.float32)
    _, out = lax.scan(step, state0, (to_chunks(q), to_chunks(k), to_chunks(v)))
    return out.transpose(1, 2, 0, 3, 4).reshape(B, H, S, Dv).astype(v.dtype)


def _group_norm_heads(y, g, b):
    yf = y.astype(jnp.float32)
    mu = yf.mean(-1, keepdims=True)
    var = jnp.square(yf - mu).mean(-1, keepdims=True)
    yn = ((yf - mu) * lax.rsqrt(var + LN_EPS)).astype(y.dtype)
    return _merge_heads(yn) * g + b


def _hybrid_mixer(h, w_in, fox_b_f, ret_gn_g, ret_gn_b, w_o):
    proj = h @ w_in
    fq, fk, fv, fl, rq, rk, rv, rg = jnp.split(proj, SPLITS, axis=-1)
    log_f = jax.nn.log_sigmoid(fl.astype(jnp.float32) + fox_b_f.astype(jnp.float32))
    log_f = log_f.transpose(0, 2, 1)
    fox = _forgetting_attention(_heads(fq, FOX_HEADS), _heads(fk, FOX_HEADS),
                                _heads(fv, FOX_HEADS), log_f)
    fox = _merge_heads(fox)
    q_r = _rotary(_heads(rq, RET_HEADS))
    k_r = _rotary(_heads(rk, RET_HEADS)) * (RET_HEAD_DIM ** -0.5)
    ret = _retention_chunkwise(q_r, k_r, _heads(rv, RET_HEADS))
    ret = jax.nn.silu(rg) * _group_norm_heads(ret, ret_gn_g, ret_gn_b)
    return jnp.concatenate([fox, ret], axis=-1) @ w_o


def setup_inputs(seed: int = 0) -> dict:
    key = jax.random.key(seed)
    ks = jax.random.split(key, 24)
    f32 = jnp.float32
    L, D = DEPTH, D_MODEL

    def nrm(k, shape, scale):
        return jax.random.normal(k, shape, f32) * scale

    x = jax.random.normal(ks[0], (BATCH, SEQ, D), f32)
    c = jax.random.normal(ks[1], (BATCH, D), f32)
    w_ada = nrm(ks[2], (L, D, N_MOD * D), 0.5 * D ** -0.5)
    b_ada = nrm(ks[3], (L, N_MOD * D), 0.02)

    def ffn(k):
        k1, k2, k3 = jax.random.split(k, 3)
        return (nrm(k1, (L, D, D_FF), D ** -0.5),
                nrm(k2, (L, D, D_FF), D ** -0.5),
                nrm(k3, (L, D_FF, D), DEEPNORM_BETA * D_FF ** -0.5))

    ffn1_w_gate, ffn1_w_up, ffn1_w_down = ffn(ks[4])
    ffn2_w_gate, ffn2_w_up, ffn2_w_down = ffn(ks[5])

    def ln(k):
        k1, k2 = jax.random.split(k)
        return 1.0 + nrm(k1, (L, D), 0.02), nrm(k2, (L, D), 0.02)

    ln1_g, ln1_b = ln(ks[6])
    ln2_g, ln2_b = ln(ks[7])
    ln3_g, ln3_b = ln(ks[8])

    s_in = D ** -0.5
    w_in = jnp.concatenate([
        nrm(ks[9], (L, D, D_FOX), s_in),
        nrm(ks[10], (L, D, D_FOX), s_in),
        nrm(ks[11], (L, D, D_FOX), DEEPNORM_BETA * s_in),
        nrm(ks[12], (L, D, FOX_HEADS), s_in),
        nrm(ks[13], (L, D, D_RET), s_in),
        nrm(ks[14], (L, D, D_RET), s_in),
        nrm(ks[15], (L, D, D_RET), DEEPNORM_BETA * s_in),
        nrm(ks[16], (L, D, D_RET), s_in),
    ], axis=-1)
    fox_b_f = 1.0 + 2.0 * jax.random.uniform(ks[17], (L, FOX_HEADS), f32)
    ret_gn_g = 1.0 + nrm(ks[18], (L, D_RET), 0.02)
    ret_gn_b = nrm(ks[19], (L, D_RET), 0.02)
    w_o = nrm(ks[20], (L, D_MIX, D), DEEPNORM_BETA * D_MIX ** -0.5)

    return {"x": x, "c": c, "w_ada": w_ada, "b_ada": b_ada,
            "ffn1_w_gate": ffn1_w_gate, "ffn1_w_up": ffn1_w_up, "ffn1_w_down": ffn1_w_down,
            "ln1_g": ln1_g, "ln1_b": ln1_b,
            "w_in": w_in, "fox_b_f": fox_b_f, "ret_gn_g": ret_gn_g, "ret_gn_b": ret_gn_b,
            "w_o": w_o, "ln2_g": ln2_g, "ln2_b": ln2_b,
            "ffn2_w_gate": ffn2_w_gate, "ffn2_w_up": ffn2_w_up, "ffn2_w_down": ffn2_w_down,
            "ln3_g": ln3_g, "ln3_b": ln3_b}


def reference(x, c, w_ada, b_ada,
              ffn1_w_gate, ffn1_w_up, ffn1_w_down, ln1_g, ln1_b,
              w_in, fox_b_f, ret_gn_g, ret_gn_b, w_o, ln2_g, ln2_b,
              ffn2_w_gate, ffn2_w_up, ffn2_w_down, ln3_g, ln3_b):
    c_act = jax.nn.silu(c)
    for l in range(DEPTH):
        mod = c_act @ w_ada[l] + b_ada[l]
        sh1, sc1, g1, sh2, sc2, g2, sh3, sc3, g3 = jnp.split(mod, N_MOD, axis=-1)
        h = _modulate(x, sh1, sc1)
        f = _swiglu(h, ffn1_w_gate[l], ffn1_w_up[l], ffn1_w_down[l])
        x = _layer_norm(DEEPNORM_ALPHA * x + FFN_RES_WEIGHT * g1[:, None, :] * f, ln1_g[l], ln1_b[l])
        h = _modulate(x, sh2, sc2)
        m = _hybrid_mixer(h, w_in[l], fox_b_f[l], ret_gn_g[l], ret_gn_b[l], w_o[l])
        x = _layer_norm(DEEPNORM_ALPHA * x + g2[:, None, :] * m, ln2_g[l], ln2_b[l])
        h = _modulate(x, sh3, sc3)
        f = _swiglu(h, ffn2_w_gate[l], ffn2_w_up[l], ffn2_w_down[l])
        x = _layer_norm(DEEPNORM_ALPHA * x + FFN_RES_WEIGHT * g3[:, None, :] * f, ln3_g[l], ln3_b[l])
    return x
```

```python
import functools

import numpy as np
import jax
import jax.numpy as jnp
from jax import lax
from jax.experimental import pallas as pl
from jax.experimental.pallas import tpu as pltpu

F32 = jnp.float32
BF16 = jnp.bfloat16

D_MODEL = 1024
D_FF = 2816
D_FOX = 512
D_RET = 512
FOX_HEADS = 8
FOX_HEAD_DIM = 64
RET_HEADS = 4
RET_HEAD_DIM = 128
N_MOD = 9
ROPE_BASE = 10000.0
LN_EPS = 1e-5
DEPTH = 1
DEEPNORM_ALPHA = (2.0 * DEPTH) ** 0.25
FFN_RES_WEIGHT = 0.5

LANES = 128
TOKEN_TILE = 512
FF_CHUNK = 256
RET_CHUNK = 256
ATTN_TILE = TOKEN_TILE
AUG_COLS = 6
VMEM_LIMIT = 56 * 2 ** 20

NT_DIMS = (((1,), (1,)), ((), ()))
TN_DIMS = (((0,), (0,)), ((), ()))


def _silu(v):
    return v / (1.0 + jnp.exp(-v))


def _layer_norm(y, g, b):
    mu = jnp.mean(y, axis=-1, keepdims=True)
    d = y - mu
    var = jnp.mean(d * d, axis=-1, keepdims=True)
    return d * lax.rsqrt(var + LN_EPS) * g + b


def _split3(v):
    hi = v.astype(BF16)
    r = v - hi.astype(F32)
    mid = r.astype(BF16)
    lo = (r - mid.astype(F32)).astype(BF16)
    return hi, mid, lo


def _const_spec(shape):
    zeros = (0,) * len(shape)
    return pl.BlockSpec(shape, lambda *_: zeros, pipeline_mode=pl.Buffered(1))


def _ada_kernel(c_ref, w_ref, b_ref, o_ref):
    ca = _silu(c_ref[...]).astype(BF16)
    o_ref[...] = jnp.dot(ca, w_ref[...].astype(BF16), preferred_element_type=F32) + b_ref[...]


def _ada(c_pad, w, b):
    rows, d = c_pad.shape
    n = w.shape[1]
    tn = 1024
    return pl.pallas_call(
        _ada_kernel,
        grid=(n // tn,),
        in_specs=[pl.BlockSpec((rows, d), lambda j: (0, 0)),
                  pl.BlockSpec((d, tn), lambda j: (0, j)),
                  pl.BlockSpec((1, tn), lambda j: (0, j))],
        out_specs=pl.BlockSpec((rows, tn), lambda j: (0, j)),
        out_shape=jax.ShapeDtypeStruct((rows, n), F32),
        compiler_params=pltpu.CompilerParams(dimension_semantics=("arbitrary",)),
        name="ada",
    )(c_pad, w, b)


def _swiglu_ln(x, mod_ref, mod_base, wgu_ref, wd_ref, lng_ref, lnb_ref, act_ref):
    sh = mod_ref[0, mod_base:mod_base + 1, :]
    sc = mod_ref[0, mod_base + 1:mod_base + 2, :]
    g = mod_ref[0, mod_base + 2:mod_base + 3, :]
    h = (x * (1.0 + sc) + sh).astype(BF16)
    for j in range(D_FF // FF_CHUNK):
        gu = jnp.dot(h, wgu_ref[j], preferred_element_type=F32)
        act = _silu(gu[:, :FF_CHUNK]) * gu[:, FF_CHUNK:]
        act_ref[:, j * FF_CHUNK:(j + 1) * FF_CHUNK] = act.astype(BF16)
    f = jnp.dot(act_ref[...], wd_ref[...], preferred_element_type=F32)
    y = DEEPNORM_ALPHA * x + (FFN_RES_WEIGHT * g) * f
    return _layer_norm(y, lng_ref[...], lnb_ref[...])


def _ffn_kernel(x_ref, mod_ref, wgu_ref, wd_ref, lng_ref, lnb_ref, o_ref, act_ref, *, mod_base):
    o_ref[0] = _swiglu_ln(x_ref[0], mod_ref, mod_base, wgu_ref, wd_ref, lng_ref, lnb_ref, act_ref)


def _mix_ffn_kernel(fox_ref, ret_ref, x_ref, mod_ref, wo_ref, ln2g_ref, ln2b_ref,
                    wgu_ref, wd_ref, lng_ref, lnb_ref, o_ref, act_ref):
    m = (jnp.dot(fox_ref[0], wo_ref[0:D_FOX, :], preferred_element_type=F32)
         + jnp.dot(ret_ref[0], wo_ref[D_FOX:D_FOX + D_RET, :], preferred_element_type=F32))
    g2 = mod_ref[0, 5:6, :]
    x2 = _layer_norm(DEEPNORM_ALPHA * x_ref[0] + g2 * m, ln2g_ref[...], ln2b_ref[...])
    o_ref[0] = _swiglu_ln(x2, mod_ref, 6, wgu_ref, wd_ref, lng_ref, lnb_ref, act_ref)


def _tile_spec(width):
    return pl.BlockSpec((1, TOKEN_TILE, width), lambda b, t: (b, t, 0))


def _ffn_weight_specs():
    n_chunks = D_FF // FF_CHUNK
    return [_const_spec((n_chunks, D_MODEL, 2 * FF_CHUNK)), _const_spec((D_FF, D_MODEL)),
            _const_spec((1, D_MODEL)), _const_spec((1, D_MODEL))]


def _ffn(x, mod3, wgu, wd, lng, lnb, mod_base):
    b, s, d = x.shape
    return pl.pallas_call(
        functools.partial(_ffn_kernel, mod_base=mod_base),
        grid=(b, s // TOKEN_TILE),
        in_specs=[_tile_spec(d), pl.BlockSpec((1, N_MOD, d), lambda b, t: (b, 0, 0))] + _ffn_weight_specs(),
        out_specs=_tile_spec(d),
        out_shape=jax.ShapeDtypeStruct((b, s, d), F32),
        scratch_shapes=[pltpu.VMEM((TOKEN_TILE, D_FF), BF16)],
        compiler_params=pltpu.CompilerParams(dimension_semantics=("arbitrary", "arbitrary"),
                                             vmem_limit_bytes=VMEM_LIMIT),
        name="ffn1",
    )(x, mod3, wgu, wd, lng, lnb)


def _mix_ffn(fox, ret, x1, mod3, wo, ln2g, ln2b, wgu, wd, lng, lnb):
    b, s, d = x1.shape
    return pl.pallas_call(
        _mix_ffn_kernel,
        grid=(b, s // TOKEN_TILE),
        in_specs=[_tile_spec(D_FOX), _tile_spec(D_RET), _tile_spec(d),
                  pl.BlockSpec((1, N_MOD, d), lambda b, t: (b, 0, 0)),
                  _const_spec((D_FOX + D_RET, d)), _const_spec((1, d)), _const_spec((1, d))] + _ffn_weight_specs(),
        out_specs=_tile_spec(d),
        out_shape=jax.ShapeDtypeStruct((b, s, d), F32),
        scratch_shapes=[pltpu.VMEM((TOKEN_TILE, D_FF), BF16)],
        compiler_params=pltpu.CompilerParams(dimension_semantics=("arbitrary", "arbitrary"),
                                             vmem_limit_bytes=VMEM_LIMIT),
        name="ffn2",
    )(fox, ret, x1, mod3, wo, ln2g, ln2b, wgu, wd, lng, lnb)


def _inproj_kernel(x_ref, mod_ref, wqk_ref, wvt_ref, wfl_ref, bfl_ref, wr_ref, tri_ref, paug_ref,
                   cos_ref, sin_ref, intra_ref, qdec_ref, kdec_ref, cdec_ref, gng_ref, gnb_ref,
                   fq_ref, fk_ref, fvt_ref, aug_ref, ret_ref, carry_ref, state_ref):
    tm = TOKEN_TILE

    @pl.when(pl.program_id(1) == 0)
    def _():
        carry_ref[...] = jnp.zeros_like(carry_ref)
        state_ref[...] = jnp.zeros_like(state_ref)

    sh = mod_ref[0, 3:4, :]
    sc = mod_ref[0, 4:5, :]
    h = (x_ref[0] * (1.0 + sc) + sh).astype(BF16)

    qk = jnp.dot(h, wqk_ref[...], preferred_element_type=F32)
    fq_ref[0] = qk[:, :D_FOX].astype(BF16)
    fk_ref[0] = qk[:, D_FOX:].astype(BF16)
    vt = lax.dot_general(wvt_ref[...], h, NT_DIMS, preferred_element_type=F32)
    fvt_ref[0, 0] = vt.astype(BF16)

    z = jnp.dot(h, wfl_ref[...], preferred_element_type=F32) + bfl_ref[...]
    logf = jnp.minimum(z, 0.0) - jnp.log1p(jnp.exp(-jnp.abs(z)))
    tri = tri_ref[...]
    cum = carry_ref[...]
    for part in _split3(logf):
        cum = cum + jnp.dot(tri, part, preferred_element_type=F32)
    carry_ref[...] = cum[tm - 1:tm, :]

    chi, cmid, clo = _split3(cum)
    lane = lax.broadcasted_iota(jnp.int32, (tm, LANES), 1)
    packed = jnp.where(lane < 8, chi.astype(F32),
             jnp.where(lane < 16, pltpu.roll(cmid.astype(F32), 8, 1),
             jnp.where(lane < 24, pltpu.roll(clo.astype(F32), 16, 1),
             jnp.where(lane == 24, 1.0, 0.0))))
    aug_ref[0] = jnp.dot(packed.astype(BF16), paug_ref[...], preferred_element_type=F32).astype(BF16)

    r_all = jnp.dot(h, wr_ref[...], preferred_element_type=F32)
    cos2 = cos_ref[...]
    sin2 = sin_ref[...]
    c = RET_CHUNK
    for hd in range(RET_HEADS):
        lo_, hi_ = hd * RET_HEAD_DIM, (hd + 1) * RET_HEAD_DIM
        tq = r_all[:, lo_:hi_]
        tk = r_all[:, D_RET + lo_:D_RET + hi_]
        q_r = tq * cos2 + pltpu.roll(tq, RET_HEAD_DIM // 2, 1) * sin2
        k_r = (tk * cos2 + pltpu.roll(tk, RET_HEAD_DIM // 2, 1) * sin2) * (RET_HEAD_DIM ** -0.5)
        v = r_all[:, 2 * D_RET + lo_:2 * D_RET + hi_]
        gate = r_all[:, 3 * D_RET + lo_:3 * D_RET + hi_]
        state = state_ref[hd]
        for ci in range(tm // c):
            rows = slice(ci * c, (ci + 1) * c)
            qc = q_r[rows].astype(BF16)
            kc = k_r[rows]
            vc = v[rows].astype(BF16)
            s = lax.dot_general(qc, kc.astype(BF16), NT_DIMS, preferred_element_type=F32) * intra_ref[hd]
            o = (jnp.dot(s.astype(BF16), vc, preferred_element_type=F32)
                 + qdec_ref[hd] * jnp.dot(qc, state.astype(BF16), preferred_element_type=F32))
            state = state * cdec_ref[hd] + lax.dot_general(
                (kc * kdec_ref[hd]).astype(BF16), vc, TN_DIMS, preferred_element_type=F32)
            mu = jnp.mean(o, axis=-1, keepdims=True)
            d = o - mu
            var = jnp.mean(d * d, axis=-1, keepdims=True)
            yn = d * lax.rsqrt(var + LN_EPS) * gng_ref[:, lo_:hi_] + gnb_ref[:, lo_:hi_]
            ret_ref[0, rows, lo_:hi_] = (_silu(gate[rows]) * yn).astype(BF16)
        state_ref[hd] = state


def _inproj(x1, mod3, wqk, wvt, wfl, bfl, wr, tri, paug, cos2, sin2, intra, qdec, kdec, cdec, gng, gnb):
    b, s, d = x1.shape
    tm = TOKEN_TILE
    nt = s // tm
    c = RET_CHUNK
    in_specs = [
        _tile_spec(d), pl.BlockSpec((1, N_MOD, d), lambda b, t: (b, 0, 0)),
        _const_spec((d, 2 * D_FOX)), _const_spec((D_FOX, d)), _const_spec((d, LANES)), _const_spec((1, LANES)),
        _const_spec((d, 4 * D_RET)), _const_spec((tm, tm)), _const_spec((LANES, 2 * LANES)),
        pl.BlockSpec((tm, LANES), lambda b, t: (t, 0)), pl.BlockSpec((tm, LANES), lambda b, t: (t, 0)),
        _const_spec((RET_HEADS, c, c)), _const_spec((RET_HEADS, c, LANES)), _const_spec((RET_HEADS, c, LANES)),
        _const_spec((RET_HEADS, 1, LANES)), _const_spec((1, D_RET)), _const_spec((1, D_RET)),
    ]
    out_specs = [
        _tile_spec(D_FOX), _tile_spec(D_FOX),
        pl.BlockSpec((1, 1, D_FOX, tm), lambda b, t: (b, t, 0, 0)),
        _tile_spec(2 * LANES), _tile_spec(D_RET),
    ]
    out_shape = [
        jax.ShapeDtypeStruct((b, s, D_FOX), BF16), jax.ShapeDtypeStruct((b, s, D_FOX), BF16),
        jax.ShapeDtypeStruct((b, nt, D_FOX, tm), BF16),
        jax.ShapeDtypeStruct((b, s, 2 * LANES), BF16), jax.ShapeDtypeStruct((b, s, D_RET), BF16),
    ]
    return pl.pallas_call(
        _inproj_kernel,
        grid=(b, nt),
        in_specs=in_specs, out_specs=out_specs, out_shape=out_shape,
        scratch_shapes=[pltpu.VMEM((1, LANES), F32),
                        pltpu.VMEM((RET_HEADS, RET_HEAD_DIM, RET_HEAD_DIM), F32)],
        compiler_params=pltpu.CompilerParams(dimension_semantics=("arbitrary", "arbitrary"),
                                             vmem_limit_bytes=VMEM_LIMIT),
        name="inproj",
    )(x1, mod3, wqk, wvt, wfl, bfl, wr, tri, paug, cos2, sin2, intra, qdec, kdec, cdec, gng, gnb)


def _fox_kernel(fq_ref, fk_ref, aug_ref, fvt_ref, o_ref, qp_ref, kp_ref, ot_ref):
    t = ATTN_TILE
    seq = fq_ref.shape[1]
    nq = seq // t
    pair = pl.program_id(1)

    for hh in range(2):
        def build(i, _, hh=hh):
            rows = pl.ds(pl.multiple_of(i * t, t), t)
            lane = lax.broadcasted_iota(jnp.int32, (t, LANES), 1)
            half = FOX_HEAD_DIM
            data = (lane < half) if hh == 0 else (lane >= half)
            base = (half if hh == 0 else 0) + AUG_COLS * pair
            is_aug = (lane >= base) & (lane < base + AUG_COLS)
            ka = aug_ref[0, rows, 0:LANES].astype(F32)
            qa = aug_ref[0, rows, LANES:2 * LANES].astype(F32)
            kp_ref[rows, :] = jnp.where(data, fk_ref[0, rows, :].astype(F32),
                                        jnp.where(is_aug, ka, 0.0)).astype(BF16)
            qp_ref[rows, :] = jnp.where(data, fq_ref[0, rows, :].astype(F32),
                                        jnp.where(is_aug, qa, 0.0)).astype(BF16)
            return 0
        lax.fori_loop(0, nq, build, 0)

        def q_block(qi, _, hh=hh):
            qblk = qp_ref[pl.ds(pl.multiple_of(qi * t, t), t), :]

            def k_step(kj, carry, masked):
                m, l, acc = carry
                kblk = kp_ref[pl.ds(pl.multiple_of(kj * t, t), t), :]
                s = lax.dot_general(kblk, qblk, NT_DIMS, preferred_element_type=F32)
                if masked:
                    k_pos = lax.broadcasted_iota(jnp.int32, (t, t), 0)
                    q_pos = lax.broadcasted_iota(jnp.int32, (t, t), 1)
                    s = jnp.where(k_pos <= q_pos, s, -jnp.inf)
                m_new = jnp.maximum(m, jnp.max(s, axis=0, keepdims=True))
                alpha = jnp.exp(m - m_new)
                p = jnp.exp(s - m_new)
                l = alpha * l + jnp.sum(p, axis=0, keepdims=True)
                vt = fvt_ref[0, kj, hh * FOX_HEAD_DIM:(hh + 1) * FOX_HEAD_DIM, :]
                acc = alpha * acc + jnp.dot(vt, p.astype(BF16), preferred_element_type=F32)
                return m_new, l, acc

            init = (jnp.full((1, t), -jnp.inf, F32), jnp.zeros((1, t), F32),
                    jnp.zeros((FOX_HEAD_DIM, t), F32))
            carry = lax.fori_loop(0, qi, lambda kj, cr: k_step(kj, cr, False), init)
            _, l, acc = k_step(qi, carry, True)
            ot_ref[qi, hh * FOX_HEAD_DIM:(hh + 1) * FOX_HEAD_DIM, :] = acc / l
            return 0
        lax.fori_loop(0, nq, q_block, 0)

    def emit(qi, _):
        o_ref[0, pl.ds(pl.multiple_of(qi * t, t), t), :] = ot_ref[qi].T.astype(BF16)
        return 0
    lax.fori_loop(0, nq, emit, 0)


def _fox(fq, fk, aug, fvt):
    b, s, _ = fq.shape
    t = ATTN_TILE
    nt = s // t
    pairs = FOX_HEADS // 2
    return pl.pallas_call(
        _fox_kernel,
        grid=(b, pairs),
        in_specs=[pl.BlockSpec((1, s, LANES), lambda b, p: (b, 0, p)),
                  pl.BlockSpec((1, s, LANES), lambda b, p: (b, 0, p)),
                  pl.BlockSpec((1, s, 2 * LANES), lambda b, p: (b, 0, 0)),
                  pl.BlockSpec((1, nt, LANES, t), lambda b, p: (b, 0, p, 0))],
        out_specs=pl.BlockSpec((1, s, LANES), lambda b, p: (b, 0, p)),
        out_shape=jax.ShapeDtypeStruct((b, s, D_FOX), BF16),
        scratch_shapes=[pltpu.VMEM((s, LANES), BF16), pltpu.VMEM((s, LANES), BF16),
                        pltpu.VMEM((nt, LANES, t), F32)],
        compiler_params=pltpu.CompilerParams(dimension_semantics=("arbitrary", "arbitrary"),
                                             vmem_limit_bytes=VMEM_LIMIT),
        name="fox",
    )(fq, fk, aug, fvt)


def _ffn_weights(w_gate, w_up, w_down):
    n_chunks = D_FF // FF_CHUNK
    wg = w_gate.reshape(D_MODEL, n_chunks, FF_CHUNK)
    wu = w_up.reshape(D_MODEL, n_chunks, FF_CHUNK)
    wgu = jnp.concatenate([wg, wu], axis=-1).transpose(1, 0, 2).astype(BF16)
    return wgu, w_down.astype(BF16)


def _aug_placement():
    p = np.zeros((LANES, 2 * LANES), np.float32)
    for head in range(FOX_HEADS):
        pair, hh = divmod(head, 2)
        base = (FOX_HEAD_DIM if hh == 0 else 0) + AUG_COLS * pair
        for j in range(3):
            p[24, base + j] = 1.0
            p[8 * j + head, base + 3 + j] = -1.0
            p[8 * j + head, LANES + base + j] = 1.0
            p[24, LANES + base + 3 + j] = 1.0
    return jnp.asarray(p, BF16)


def _rotary_tables(seq):
    half = RET_HEAD_DIM // 2
    inv_freq = ROPE_BASE ** (-jnp.arange(half, dtype=F32) / half)
    ang = jnp.arange(seq, dtype=F32)[:, None] * inv_freq[None, :]
    cos, sin = jnp.cos(ang), jnp.sin(ang)
    return jnp.concatenate([cos, cos], axis=-1), jnp.concatenate([-sin, sin], axis=-1)


def _retention_tables():
    c = RET_CHUNK
    log_gamma = jnp.log1p(-jnp.power(2.0, -5.0 - jnp.arange(RET_HEADS, dtype=F32)))
    idx = jnp.arange(c, dtype=F32)
    diff = idx[:, None] - idx[None, :]
    intra = jnp.where(diff >= 0, jnp.exp(log_gamma[:, None, None] * jnp.maximum(diff, 0.0)), 0.0)
    qdec = jnp.exp(log_gamma[:, None] * (idx + 1.0))[..., None]
    kdec = jnp.exp(log_gamma[:, None] * (c - 1.0 - idx))[..., None]
    cdec = jnp.exp(log_gamma * c)[:, None, None]
    bc = lambda a: jnp.broadcast_to(a, a.shape[:2] + (LANES,))
    return intra, bc(qdec), bc(kdec), bc(cdec)


def kernel(x, c, w_ada, b_ada, ffn1_w_gate, ffn1_w_up, ffn1_w_down, ln1_g, ln1_b, w_in, fox_b_f, ret_gn_g,
           ret_gn_b, w_o, ln2_g, ln2_b, ffn2_w_gate, ffn2_w_up, ffn2_w_down, ln3_g, ln3_b):
    batch, seq, d = x.shape
    assert d == D_MODEL and seq % TOKEN_TILE == 0 and w_ada.shape[0] == DEPTH
    layer = 0

    c_pad = jnp.pad(c, ((0, 8 - batch), (0, 0)))
    mod3 = _ada(c_pad, w_ada[layer], b_ada[layer][None, :])[:batch].reshape(batch, N_MOD, d)

    wgu1, wd1 = _ffn_weights(ffn1_w_gate[layer], ffn1_w_up[layer], ffn1_w_down[layer])
    x1 = _ffn(x, mod3, wgu1, wd1, ln1_g[layer][None, :], ln1_b[layer][None, :], 0)

    w = w_in[layer]
    o_fl = 3 * D_FOX
    o_r = o_fl + FOX_HEADS
    wqk = jnp.concatenate([w[:, :D_FOX] * (FOX_HEAD_DIM ** -0.5), w[:, D_FOX:2 * D_FOX]], axis=1).astype(BF16)
    wvt = w[:, 2 * D_FOX:3 * D_FOX].T.astype(BF16)
    wfl = jnp.pad(w[:, o_fl:o_r], ((0, 0), (0, LANES - FOX_HEADS))).astype(BF16)
    bfl = jnp.pad(fox_b_f[layer], (0, LANES - FOX_HEADS))[None, :]
    wr = w[:, o_r:o_r + 4 * D_RET].astype(BF16)
    tri = jnp.asarray(np.tril(np.ones((TOKEN_TILE, TOKEN_TILE), np.float32)), BF16)
    cos2, sin2 = _rotary_tables(seq)
    intra, qdec, kdec, cdec = _retention_tables()
    fq, fk, fvt, aug, ret = _inproj(x1, mod3, wqk, wvt, wfl, bfl, wr, tri, _aug_placement(), cos2, sin2,
                                    intra, qdec, kdec, cdec, ret_gn_g[layer][None, :], ret_gn_b[layer][None, :])

    fox = _fox(fq, fk, aug, fvt)

    wgu2, wd2 = _ffn_weights(ffn2_w_gate[layer], ffn2_w_up[layer], ffn2_w_down[layer])
    return _mix_ffn(fox, ret, x1, mod3, w_o[layer].astype(BF16), ln2_g[layer][None, :], ln2_b[layer][None, :],
                    wgu2, wd2, ln3_g[layer][None, :], ln3_b[layer][None, :])
```

```python
import functools
import math

import numpy as np
import jax
import jax.numpy as jnp
from jax import lax
from jax.experimental import pallas as pl
from jax.experimental.pallas import tpu as pltpu

F32 = jnp.float32
BF16 = jnp.bfloat16

D_MODEL = 1024
D_FF = 2816
D_FOX = 512
D_RET = 512
FOX_HEADS = 8
FOX_HEAD_DIM = 64
RET_HEADS = 4
RET_HEAD_DIM = 128
N_MOD = 9
ROPE_BASE = 10000.0
LN_EPS = 1e-5
DEPTH = 1
DEEPNORM_ALPHA = (2.0 * DEPTH) ** 0.25
FFN_RES_WEIGHT = 0.5
LOG2E = math.log2(math.e)

LANES = 128
BF16_ROWS = 16
TOKEN_TILE = 512
FF_CHUNK = 256
RET_CHUNK = 256
ATTN_TILE = TOKEN_TILE
AUG_COLS = 6
V_ROWS = FOX_HEAD_DIM + BF16_ROWS
VMEM_LIMIT = 56 * 2 ** 20

NT_DIMS = (((1,), (1,)), ((), ()))
TN_DIMS = (((0,), (0,)), ((), ()))


def _silu(v):
    return v / (1.0 + jnp.exp(-v))


def _layer_norm(y, g, b):
    mu = jnp.mean(y, axis=-1, keepdims=True)
    d = y - mu
    var = jnp.mean(d * d, axis=-1, keepdims=True)
    return d * lax.rsqrt(var + LN_EPS) * g + b


def _split3(v):
    hi = v.astype(BF16)
    r = v - hi.astype(F32)
    mid = r.astype(BF16)
    lo = (r - mid.astype(F32)).astype(BF16)
    return hi, mid, lo


def _const_spec(shape):
    zeros = (0,) * len(shape)
    return pl.BlockSpec(shape, lambda *_: zeros, pipeline_mode=pl.Buffered(1))


def _ada_kernel(c_ref, w_ref, b_ref, o_ref):
    ca = _silu(c_ref[...]).astype(BF16)
    o_ref[...] = jnp.dot(ca, w_ref[...].astype(BF16), preferred_element_type=F32) + b_ref[...]


def _ada(c_pad, w, b):
    rows, d = c_pad.shape
    n = w.shape[1]
    tn = 1024
    return pl.pallas_call(
        _ada_kernel,
        grid=(n // tn,),
        in_specs=[pl.BlockSpec((rows, d), lambda j: (0, 0)),
                  pl.BlockSpec((d, tn), lambda j: (0, j)),
                  pl.BlockSpec((1, tn), lambda j: (0, j))],
        out_specs=pl.BlockSpec((rows, tn), lambda j: (0, j)),
        out_shape=jax.ShapeDtypeStruct((rows, n), F32),
        compiler_params=pltpu.CompilerParams(dimension_semantics=("arbitrary",)),
        name="ada",
    )(c_pad, w, b)


def _swiglu_ln(x, mod_ref, mod_base, wg_ref, wu_ref, wd_ref, lng_ref, lnb_ref, act_ref):
    sh = mod_ref[0, mod_base:mod_base + 1, :]
    sc = mod_ref[0, mod_base + 1:mod_base + 2, :]
    g = mod_ref[0, mod_base + 2:mod_base + 3, :]
    h = (x * (1.0 + sc) + sh).astype(BF16)
    for j in range(D_FF // FF_CHUNK):
        cols = slice(j * FF_CHUNK, (j + 1) * FF_CHUNK)
        gate = jnp.dot(h, wg_ref[:, cols], preferred_element_type=F32)
        up = jnp.dot(h, wu_ref[:, cols], preferred_element_type=F32)
        act_ref[:, cols] = (_silu(gate) * up).astype(BF16)
    f = jnp.dot(act_ref[...], wd_ref[...], preferred_element_type=F32)
    y = DEEPNORM_ALPHA * x + (FFN_RES_WEIGHT * g) * f
    return _layer_norm(y, lng_ref[...], lnb_ref[...])


def _ffn_kernel(x_ref, mod_ref, wg_ref, wu_ref, wd_ref, lng_ref, lnb_ref, o_ref, act_ref, *, mod_base):
    o_ref[0] = _swiglu_ln(x_ref[0], mod_ref, mod_base, wg_ref, wu_ref, wd_ref, lng_ref, lnb_ref, act_ref)


def _mix_ffn_kernel(fox_ref, ret_ref, x_ref, mod_ref, wo_ref, ln2g_ref, ln2b_ref,
                    wg_ref, wu_ref, wd_ref, lng_ref, lnb_ref, o_ref, act_ref):
    m = (jnp.dot(fox_ref[0], wo_ref[0:D_FOX, :], preferred_element_type=F32)
         + jnp.dot(ret_ref[0], wo_ref[D_FOX:D_FOX + D_RET, :], preferred_element_type=F32))
    g2 = mod_ref[0, 5:6, :]
    x2 = _layer_norm(DEEPNORM_ALPHA * x_ref[0] + g2 * m, ln2g_ref[...], ln2b_ref[...])
    o_ref[0] = _swiglu_ln(x2, mod_ref, 6, wg_ref, wu_ref, wd_ref, lng_ref, lnb_ref, act_ref)


def _tile_spec(width):
    return pl.BlockSpec((1, TOKEN_TILE, width), lambda b, t: (b, t, 0))


def _ffn_weight_specs():
    return [_const_spec((D_MODEL, D_FF)), _const_spec((D_MODEL, D_FF)), _const_spec((D_FF, D_MODEL)),
            _const_spec((1, D_MODEL)), _const_spec((1, D_MODEL))]


def _ffn(x, mod3, wg, wu, wd, lng, lnb, mod_base):
    b, s, d = x.shape
    return pl.pallas_call(
        functools.partial(_ffn_kernel, mod_base=mod_base),
        grid=(b, s // TOKEN_TILE),
        in_specs=[_tile_spec(d), pl.BlockSpec((1, N_MOD, d), lambda b, t: (b, 0, 0))] + _ffn_weight_specs(),
        out_specs=_tile_spec(d),
        out_shape=jax.ShapeDtypeStruct((b, s, d), F32),
        scratch_shapes=[pltpu.VMEM((TOKEN_TILE, D_FF), BF16)],
        compiler_params=pltpu.CompilerParams(dimension_semantics=("arbitrary", "arbitrary"),
                                             vmem_limit_bytes=VMEM_LIMIT),
        name="ffn1",
    )(x, mod3, wg, wu, wd, lng, lnb)


def _mix_ffn(fox, ret, x1, mod3, wo, ln2g, ln2b, wg, wu, wd, lng, lnb):
    b, s, d = x1.shape
    return pl.pallas_call(
        _mix_ffn_kernel,
        grid=(b, s // TOKEN_TILE),
        in_specs=[_tile_spec(D_FOX), _tile_spec(D_RET), _tile_spec(d),
                  pl.BlockSpec((1, N_MOD, d), lambda b, t: (b, 0, 0)),
                  _const_spec((D_FOX + D_RET, d)), _const_spec((1, d)), _const_spec((1, d))] + _ffn_weight_specs(),
        out_specs=_tile_spec(d),
        out_shape=jax.ShapeDtypeStruct((b, s, d), F32),
        scratch_shapes=[pltpu.VMEM((TOKEN_TILE, D_FF), BF16)],
        compiler_params=pltpu.CompilerParams(dimension_semantics=("arbitrary", "arbitrary"),
                                             vmem_limit_bytes=VMEM_LIMIT),
        name="ffn2",
    )(fox, ret, x1, mod3, wo, ln2g, ln2b, wg, wu, wd, lng, lnb)


def _inproj_kernel(x_ref, mod_ref, wqk_ref, wvt_ref, wfl_ref, bfl_ref, wr_ref, tri_ref, paug_ref,
                   cos_ref, sin_ref, intra_ref, qdec_ref, kdec_ref, cdec_ref, gng_ref, gnb_ref,
                   fq_ref, fk_ref, fvt_ref, aug_ref, ret_ref, carry_ref, state_ref):
    tm = TOKEN_TILE

    @pl.when(pl.program_id(1) == 0)
    def _():
        carry_ref[...] = jnp.zeros_like(carry_ref)
        state_ref[...] = jnp.zeros_like(state_ref)

    sh = mod_ref[0, 3:4, :]
    sc = mod_ref[0, 4:5, :]
    h = (x_ref[0] * (1.0 + sc) + sh).astype(BF16)

    qk = jnp.dot(h, wqk_ref[...], preferred_element_type=F32)
    fq_ref[0] = qk[:, :D_FOX].astype(BF16)
    fk_ref[0] = qk[:, D_FOX:].astype(BF16)
    vt = lax.dot_general(wvt_ref[...], h, NT_DIMS, preferred_element_type=F32).astype(BF16)
    for hd in range(FOX_HEADS):
        fvt_ref[0, 0, hd * V_ROWS:hd * V_ROWS + FOX_HEAD_DIM, :] = vt[hd * FOX_HEAD_DIM:(hd + 1) * FOX_HEAD_DIM]
        fvt_ref[0, 0, hd * V_ROWS + FOX_HEAD_DIM:(hd + 1) * V_ROWS, :] = jnp.ones((BF16_ROWS, tm), BF16)

    z = jnp.dot(h, wfl_ref[...], preferred_element_type=F32) + bfl_ref[...]
    logf = jnp.minimum(z, 0.0) - jnp.log1p(jnp.exp(-jnp.abs(z)))
    tri = tri_ref[...]
    cum = carry_ref[...]
    for part in _split3(logf):
        cum = cum + jnp.dot(tri, part, preferred_element_type=F32)
    carry_ref[...] = cum[tm - 1:tm, :]

    chi, cmid, clo = _split3(cum * LOG2E)
    lane = lax.broadcasted_iota(jnp.int32, (tm, LANES), 1)
    packed = jnp.where(lane < 8, chi.astype(F32),
             jnp.where(lane < 16, pltpu.roll(cmid.astype(F32), 8, 1),
             jnp.where(lane < 24, pltpu.roll(clo.astype(F32), 16, 1),
             jnp.where(lane == 24, 1.0, 0.0))))
    aug_ref[0] = jnp.dot(packed.astype(BF16), paug_ref[...], preferred_element_type=F32).astype(BF16)

    r_all = jnp.dot(h, wr_ref[...], preferred_element_type=F32)
    cos2 = cos_ref[...]
    sin2 = sin_ref[...]
    c = RET_CHUNK
    for hd in range(RET_HEADS):
        lo_, hi_ = hd * RET_HEAD_DIM, (hd + 1) * RET_HEAD_DIM
        tq = r_all[:, lo_:hi_]
        tk = r_all[:, D_RET + lo_:D_RET + hi_]
        q_r = tq * cos2 + pltpu.roll(tq, RET_HEAD_DIM // 2, 1) * sin2
        k_r = (tk * cos2 + pltpu.roll(tk, RET_HEAD_DIM // 2, 1) * sin2) * (RET_HEAD_DIM ** -0.5)
        v = r_all[:, 2 * D_RET + lo_:2 * D_RET + hi_]
        gate = r_all[:, 3 * D_RET + lo_:3 * D_RET + hi_]
        state = state_ref[hd]
        for ci in range(tm // c):
            rows = slice(ci * c, (ci + 1) * c)
            qc = q_r[rows].astype(BF16)
            kc = k_r[rows]
            vc = v[rows].astype(BF16)
            s = lax.dot_general(qc, kc.astype(BF16), NT_DIMS, preferred_element_type=F32) * intra_ref[hd]
            o = (jnp.dot(s.astype(BF16), vc, preferred_element_type=F32)
                 + qdec_ref[hd] * jnp.dot(qc, state.astype(BF16), preferred_element_type=F32))
            state = state * cdec_ref[hd] + lax.dot_general(
                (kc * kdec_ref[hd]).astype(BF16), vc, TN_DIMS, preferred_element_type=F32)
            mu = jnp.mean(o, axis=-1, keepdims=True)
            d = o - mu
            var = jnp.mean(d * d, axis=-1, keepdims=True)
            yn = d * lax.rsqrt(var + LN_EPS) * gng_ref[:, lo_:hi_] + gnb_ref[:, lo_:hi_]
            ret_ref[0, rows, lo_:hi_] = (_silu(gate[rows]) * yn).astype(BF16)
        state_ref[hd] = state


def _inproj(x1, mod3, wqk, wvt, wfl, bfl, wr, tri, paug, cos2, sin2, intra, qdec, kdec, cdec, gng, gnb):
    b, s, d = x1.shape
    tm = TOKEN_TILE
    nt = s // tm
    c = RET_CHUNK
    in_specs = [
        _tile_spec(d), pl.BlockSpec((1, N_MOD, d), lambda b, t: (b, 0, 0)),
        _const_spec((d, 2 * D_FOX)), _const_spec((D_FOX, d)), _const_spec((d, LANES)), _const_spec((1, LANES)),
        _const_spec((d, 4 * D_RET)), _const_spec((tm, tm)), _const_spec((LANES, 2 * LANES)),
        pl.BlockSpec((tm, LANES), lambda b, t: (t, 0)), pl.BlockSpec((tm, LANES), lambda b, t: (t, 0)),
        _const_spec((RET_HEADS, c, c)), _const_spec((RET_HEADS, c, LANES)), _const_spec((RET_HEADS, c, LANES)),
        _const_spec((RET_HEADS, 1, LANES)), _const_spec((1, D_RET)), _const_spec((1, D_RET)),
    ]
    out_specs = [
        _tile_spec(D_FOX), _tile_spec(D_FOX),
        pl.BlockSpec((1, 1, FOX_HEADS * V_ROWS, tm), lambda b, t: (b, t, 0, 0)),
        _tile_spec(2 * LANES), _tile_spec(D_RET),
    ]
    out_shape = [
        jax.ShapeDtypeStruct((b, s, D_FOX), BF16), jax.ShapeDtypeStruct((b, s, D_FOX), BF16),
        jax.ShapeDtypeStruct((b, nt, FOX_HEADS * V_ROWS, tm), BF16),
        jax.ShapeDtypeStruct((b, s, 2 * LANES), BF16), jax.ShapeDtypeStruct((b, s, D_RET), BF16),
    ]
    return pl.pallas_call(
        _inproj_kernel,
        grid=(b, nt),
        in_specs=in_specs, out_specs=out_specs, out_shape=out_shape,
        scratch_shapes=[pltpu.VMEM((1, LANES), F32),
                        pltpu.VMEM((RET_HEADS, RET_HEAD_DIM, RET_HEAD_DIM), F32)],
        compiler_params=pltpu.CompilerParams(dimension_semantics=("arbitrary", "arbitrary"),
                                             vmem_limit_bytes=VMEM_LIMIT),
        name="inproj",
    )(x1, mod3, wqk, wvt, wfl, bfl, wr, tri, paug, cos2, sin2, intra, qdec, kdec, cdec, gng, gnb)


def _fox_kernel(fq_ref, fk_ref, aug_ref, fvt_ref, o_ref,
                qp_ref, kp_ref, ot_ref, s_ref, p_ref, acc_ref, mask_ref, stat_ref):
    t = ATTN_TILE
    seq = fq_ref.shape[1]
    nq = seq // t
    pair = pl.program_id(1)
    half = FOX_HEAD_DIM

    def build(i, _):
        rows = pl.ds(pl.multiple_of(i * t, t), t)
        lane = lax.broadcasted_iota(jnp.int32, (t, LANES), 1)
        ka = aug_ref[0, rows, 0:LANES].astype(F32)
        qa = aug_ref[0, rows, LANES:2 * LANES].astype(F32)
        kd = fk_ref[0, rows, :].astype(F32)
        qd = fq_ref[0, rows, :].astype(F32)
        for hh in range(2):
            data = (lane < half) if hh == 0 else (lane >= half)
            base = (half if hh == 0 else 0) + AUG_COLS * pair
            is_aug = (lane >= base) & (lane < base + AUG_COLS)
            kp_ref[hh, rows, :] = jnp.where(data, kd, jnp.where(is_aug, ka, 0.0)).astype(BF16)
            qp_ref[hh, rows, :] = jnp.where(data, qd, jnp.where(is_aug, qa, 0.0)).astype(BF16)
        return 0
    lax.fori_loop(0, nq, build, 0)

    k_pos = lax.broadcasted_iota(jnp.int32, (t, t), 0)
    q_pos = lax.broadcasted_iota(jnp.int32, (t, t), 1)
    mask_ref[0] = jnp.zeros((t, t), F32)
    mask_ref[1] = jnp.where(k_pos <= q_pos, 0.0, -jnp.inf)
    s_ref[...] = jnp.zeros_like(s_ref)
    p_ref[...] = jnp.zeros_like(p_ref)
    acc_ref[...] = jnp.zeros_like(acc_ref)
    stat_ref[...] = jnp.zeros_like(stat_ref)

    def next_step(q, k):
        wrap = k >= q
        q2 = jnp.where(wrap, q + 1, q)
        k2 = jnp.where(wrap, 0, k + 1)
        done = q2 >= nq
        return jnp.where(done, nq - 1, q2), jnp.where(done, nq - 1, k2)

    def body(i, carry):
        qa, ka, qb, kb, qc, kc = carry
        krows = pl.ds(pl.multiple_of(ka * t, t), t)
        qrows = pl.ds(pl.multiple_of(qa * t, t), t)
        diag = (ka == qa).astype(jnp.int32)
        for hh in range(2):
            m = stat_ref[hh, 0:1, :]
            alpha1 = stat_ref[hh, 1:2, :]
            alpha2 = stat_ref[hh, 2:3, :]
            vt = fvt_ref[0, kc, hh * V_ROWS:(hh + 1) * V_ROWS, :]
            acc = alpha2 * acc_ref[hh] + jnp.dot(vt, p_ref[hh], preferred_element_type=F32)
            acc_ref[hh] = acc
            ot_ref[qc, hh * half:(hh + 1) * half, :] = acc[:half] / acc[half:half + 1]

            p_ref[hh] = jnp.exp2(s_ref[hh] - m).astype(BF16)

            s = lax.dot_general(kp_ref[hh, krows, :], qp_ref[hh, qrows, :], NT_DIMS,
                                preferred_element_type=F32) + mask_ref[diag]
            s_ref[hh] = s
            m_prev = jnp.where(ka == 0, -jnp.inf, m)
            m_new = jnp.maximum(m_prev, jnp.max(s, axis=0, keepdims=True))
            stat_ref[hh, 0:1, :] = m_new
            stat_ref[hh, 1:2, :] = jnp.exp2(m_prev - m_new)
            stat_ref[hh, 2:3, :] = alpha1
        qn, kn = next_step(qa, ka)
        return qn, kn, qa, ka, qb, kb

    n_steps = nq * (nq + 1) // 2
    lax.fori_loop(0, n_steps + 2, body, (jnp.int32(0),) * 6)

    def emit(qi, _):
        o_ref[0, pl.ds(pl.multiple_of(qi * t, t), t), :] = ot_ref[qi].T.astype(BF16)
        return 0
    lax.fori_loop(0, nq, emit, 0)


def _fox(fq, fk, aug, fvt):
    b, s, _ = fq.shape
    t = ATTN_TILE
    nt = s // t
    pairs = FOX_HEADS // 2
    return pl.pallas_call(
        _fox_kernel,
        grid=(b, pairs),
        in_specs=[pl.BlockSpec((1, s, LANES), lambda b, p: (b, 0, p)),
                  pl.BlockSpec((1, s, LANES), lambda b, p: (b, 0, p)),
                  pl.BlockSpec((1, s, 2 * LANES), lambda b, p: (b, 0, 0)),
                  pl.BlockSpec((1, nt, 2 * V_ROWS, t), lambda b, p: (b, 0, p, 0))],
        out_specs=pl.BlockSpec((1, s, LANES), lambda b, p: (b, 0, p)),
        out_shape=jax.ShapeDtypeStruct((b, s, D_FOX), BF16),
        scratch_shapes=[pltpu.VMEM((2, s, LANES), BF16), pltpu.VMEM((2, s, LANES), BF16),
                        pltpu.VMEM((nt, LANES, t), F32),
                        pltpu.VMEM((2, t, t), F32), pltpu.VMEM((2, t, t), BF16),
                        pltpu.VMEM((2, V_ROWS, t), F32), pltpu.VMEM((2, t, t), F32),
                        pltpu.VMEM((2, 8, t), F32)],
        compiler_params=pltpu.CompilerParams(dimension_semantics=("arbitrary", "arbitrary"),
                                             vmem_limit_bytes=VMEM_LIMIT),
        name="fox",
    )(fq, fk, aug, fvt)


def _aug_placement():
    p = np.zeros((LANES, 2 * LANES), np.float32)
    for head in range(FOX_HEADS):
        pair, hh = divmod(head, 2)
        base = (FOX_HEAD_DIM if hh == 0 else 0) + AUG_COLS * pair
        for j in range(3):
            p[24, base + j] = 1.0
            p[8 * j + head, base + 3 + j] = -1.0
            p[8 * j + head, LANES + base + j] = 1.0
            p[24, LANES + base + 3 + j] = 1.0
    return jnp.asarray(p, BF16)


def _rotary_tables(seq):
    half = RET_HEAD_DIM // 2
    inv_freq = ROPE_BASE ** (-jnp.arange(half, dtype=F32) / half)
    ang = jnp.arange(seq, dtype=F32)[:, None] * inv_freq[None, :]
    cos, sin = jnp.cos(ang), jnp.sin(ang)
    return jnp.concatenate([cos, cos], axis=-1), jnp.concatenate([-sin, sin], axis=-1)


def _retention_tables():
    c = RET_CHUNK
    log_gamma = jnp.log1p(-jnp.power(2.0, -5.0 - jnp.arange(RET_HEADS, dtype=F32)))
    idx = jnp.arange(c, dtype=F32)
    diff = idx[:, None] - idx[None, :]
    intra = jnp.where(diff >= 0, jnp.exp(log_gamma[:, None, None] * jnp.maximum(diff, 0.0)), 0.0)
    qdec = jnp.exp(log_gamma[:, None] * (idx + 1.0))[..., None]
    kdec = jnp.exp(log_gamma[:, None] * (c - 1.0 - idx))[..., None]
    cdec = jnp.exp(log_gamma * c)[:, None, None]
    bc = lambda a: jnp.broadcast_to(a, a.shape[:2] + (LANES,))
    return intra, bc(qdec), bc(kdec), bc(cdec)


def kernel(x, c, w_ada, b_ada, ffn1_w_gate, ffn1_w_up, ffn1_w_down, ln1_g, ln1_b, w_in, fox_b_f, ret_gn_g,
           ret_gn_b, w_o, ln2_g, ln2_b, ffn2_w_gate, ffn2_w_up, ffn2_w_down, ln3_g, ln3_b):
    batch, seq, d = x.shape
    assert d == D_MODEL and seq % TOKEN_TILE == 0 and w_ada.shape[0] == DEPTH
    layer = 0

    c_pad = jnp.pad(c, ((0, 8 - batch), (0, 0)))
    mod3 = _ada(c_pad, w_ada[layer], b_ada[layer][None, :])[:batch].reshape(batch, N_MOD, d)

    x1 = _ffn(x, mod3, ffn1_w_gate[layer].astype(BF16), ffn1_w_up[layer].astype(BF16),
              ffn1_w_down[layer].astype(BF16), ln1_g[layer][None, :], ln1_b[layer][None, :], 0)

    w = w_in[layer]
    o_fl = 3 * D_FOX
    o_r = o_fl + FOX_HEADS
    q_scale = LOG2E * FOX_HEAD_DIM ** -0.5
    wqk = jnp.concatenate([w[:, :D_FOX] * q_scale, w[:, D_FOX:2 * D_FOX]], axis=1).astype(BF16)
    wvt = w[:, 2 * D_FOX:3 * D_FOX].T.astype(BF16)
    wfl = jnp.pad(w[:, o_fl:o_r], ((0, 0), (0, LANES - FOX_HEADS))).astype(BF16)
    bfl = jnp.pad(fox_b_f[layer], (0, LANES - FOX_HEADS))[None, :]
    wr = w[:, o_r:o_r + 4 * D_RET].astype(BF16)
    tri = jnp.asarray(np.tril(np.ones((TOKEN_TILE, TOKEN_TILE), np.float32)), BF16)
    cos2, sin2 = _rotary_tables(seq)
    intra, qdec, kdec, cdec = _retention_tables()
    fq, fk, fvt, aug, ret = _inproj(x1, mod3, wqk, wvt, wfl, bfl, wr, tri, _aug_placement(), cos2, sin2,
                                    intra, qdec, kdec, cdec, ret_gn_g[layer][None, :], ret_gn_b[layer][None, :])

    fox = _fox(fq, fk, aug, fvt)

    return _mix_ffn(fox, ret, x1, mod3, w_o[layer].astype(BF16), ln2_g[layer][None, :], ln2_b[layer][None, :],
                    ffn2_w_gate[layer].astype(BF16), ffn2_w_up[layer].astype(BF16),
                    ffn2_w_down[layer].astype(BF16), ln3_g[layer][None, :], ln3_b[layer][None, :])
```

```python
import functools
import math

import numpy as np
import jax
import jax.numpy as jnp
from jax import lax
from jax.experimental import pallas as pl
from jax.experimental.pallas import tpu as pltpu

F32 = jnp.float32
BF16 = jnp.bfloat16

D_MODEL = 1024
D_FF = 2816
D_FOX = 512
D_RET = 512
FOX_HEADS = 8
FOX_HEAD_DIM = 64
RET_HEADS = 4
RET_HEAD_DIM = 128
N_MOD = 9
ROPE_BASE = 10000.0
LN_EPS = 1e-5
DEPTH = 1
DEEPNORM_ALPHA = (2.0 * DEPTH) ** 0.25
FFN_RES_WEIGHT = 0.5
LOG2E = math.log2(math.e)

LANES = 128
BF16_ROWS = 16
TOKEN_TILE = 512
FFN_TILE = 1024
FF_CHUNK = 256
RET_CHUNK = 256
ATTN_TILE = TOKEN_TILE
ATTN_HEADS = 2
AUG_COLS = 6
V_ROWS = FOX_HEAD_DIM + BF16_ROWS
VMEM_LIMIT = 56 * 2 ** 20

NT_DIMS = (((1,), (1,)), ((), ()))
TN_DIMS = (((0,), (0,)), ((), ()))


def _silu(v):
    return v / (1.0 + jnp.exp(-v))


def _layer_norm(y, g, b):
    mu = jnp.mean(y, axis=-1, keepdims=True)
    d = y - mu
    var = jnp.mean(d * d, axis=-1, keepdims=True)
    return d * lax.rsqrt(var + LN_EPS) * g + b


def _split3(v):
    hi = v.astype(BF16)
    r = v - hi.astype(F32)
    mid = r.astype(BF16)
    lo = (r - mid.astype(F32)).astype(BF16)
    return hi, mid, lo


def _const_spec(shape):
    zeros = (0,) * len(shape)
    return pl.BlockSpec(shape, lambda *_: zeros, pipeline_mode=pl.Buffered(1))


def _aug_lanes(head):
    half = FOX_HEAD_DIM
    pair, odd = divmod(head, 2)
    data_lo = half if odd else 0
    aug_lo = (0 if odd else half) + AUG_COLS * pair
    return data_lo, aug_lo


def _ada_kernel(c_ref, w_ref, b_ref, o_ref):
    ca = _silu(c_ref[...]).astype(BF16)
    o_ref[...] = jnp.dot(ca, w_ref[...].astype(BF16), preferred_element_type=F32) + b_ref[...]


def _ada(c_pad, w, b):
    rows, d = c_pad.shape
    n = w.shape[1]
    tn = 1024
    return pl.pallas_call(
        _ada_kernel,
        grid=(n // tn,),
        in_specs=[pl.BlockSpec((rows, d), lambda j: (0, 0)),
                  pl.BlockSpec((d, tn), lambda j: (0, j)),
                  pl.BlockSpec((1, tn), lambda j: (0, j))],
        out_specs=pl.BlockSpec((rows, tn), lambda j: (0, j)),
        out_shape=jax.ShapeDtypeStruct((rows, n), F32),
        compiler_params=pltpu.CompilerParams(dimension_semantics=("arbitrary",)),
        name="ada",
    )(c_pad, w, b)


def _swiglu_ln(x, mod_ref, mod_base, wg_ref, wu_ref, wd_ref, lng_ref, lnb_ref, act_ref):
    sh = mod_ref[0, mod_base:mod_base + 1, :]
    sc = mod_ref[0, mod_base + 1:mod_base + 2, :]
    g = mod_ref[0, mod_base + 2:mod_base + 3, :]
    h = (x * (1.0 + sc) + sh).astype(BF16)
    for j in range(D_FF // FF_CHUNK):
        cols = slice(j * FF_CHUNK, (j + 1) * FF_CHUNK)
        gate = jnp.dot(h, wg_ref[:, cols], preferred_element_type=F32)
        up = jnp.dot(h, wu_ref[:, cols], preferred_element_type=F32)
        act_ref[:, cols] = (_silu(gate) * up).astype(BF16)
    f = jnp.dot(act_ref[...], wd_ref[...], preferred_element_type=F32)
    y = DEEPNORM_ALPHA * x + (FFN_RES_WEIGHT * g) * f
    return _layer_norm(y, lng_ref[...], lnb_ref[...])


def _ffn_kernel(x_ref, mod_ref, wg_ref, wu_ref, wd_ref, lng_ref, lnb_ref, o_ref, act_ref, *, mod_base):
    o_ref[0] = _swiglu_ln(x_ref[0], mod_ref, mod_base, wg_ref, wu_ref, wd_ref, lng_ref, lnb_ref, act_ref)


def _mix_ffn_kernel(foxt_ref, ret_ref, x_ref, mod_ref, wo_ref, ln2g_ref, ln2b_ref,
                    wg_ref, wu_ref, wd_ref, lng_ref, lnb_ref, o_ref, act_ref):
    wo_fox = wo_ref[0:D_FOX, :]
    m_fox = [lax.dot_general(foxt_ref[0, i], wo_fox, TN_DIMS, preferred_element_type=F32)
             for i in range(FFN_TILE // ATTN_TILE)]
    m = (jnp.concatenate(m_fox, axis=0)
         + jnp.dot(ret_ref[0], wo_ref[D_FOX:D_FOX + D_RET, :], preferred_element_type=F32))
    g2 = mod_ref[0, 5:6, :]
    x2 = _layer_norm(DEEPNORM_ALPHA * x_ref[0] + g2 * m, ln2g_ref[...], ln2b_ref[...])
    o_ref[0] = _swiglu_ln(x2, mod_ref, 6, wg_ref, wu_ref, wd_ref, lng_ref, lnb_ref, act_ref)


def _tile_spec(tile, width):
    return pl.BlockSpec((1, tile, width), lambda b, t: (b, t, 0))


def _ffn_weight_specs():
    return [_const_spec((D_MODEL, D_FF)), _const_spec((D_MODEL, D_FF)), _const_spec((D_FF, D_MODEL)),
            _const_spec((1, D_MODEL)), _const_spec((1, D_MODEL))]


def _ffn(x, mod3, wg, wu, wd, lng, lnb, mod_base):
    b, s, d = x.shape
    return pl.pallas_call(
        functools.partial(_ffn_kernel, mod_base=mod_base),
        grid=(b, s // FFN_TILE),
        in_specs=[_tile_spec(FFN_TILE, d), pl.BlockSpec((1, N_MOD, d), lambda b, t: (b, 0, 0))]
        + _ffn_weight_specs(),
        out_specs=_tile_spec(FFN_TILE, d),
        out_shape=jax.ShapeDtypeStruct((b, s, d), F32),
        scratch_shapes=[pltpu.VMEM((FFN_TILE, D_FF), BF16)],
        compiler_params=pltpu.CompilerParams(dimension_semantics=("arbitrary", "arbitrary"),
                                             vmem_limit_bytes=VMEM_LIMIT),
        name="ffn1",
    )(x, mod3, wg, wu, wd, lng, lnb)


def _mix_ffn(foxt, ret, x1, mod3, wo, ln2g, ln2b, wg, wu, wd, lng, lnb):
    b, s, d = x1.shape
    per = FFN_TILE // ATTN_TILE
    return pl.pallas_call(
        _mix_ffn_kernel,
        grid=(b, s // FFN_TILE),
        in_specs=[pl.BlockSpec((1, per, D_FOX, ATTN_TILE), lambda b, t: (b, t, 0, 0)),
                  _tile_spec(FFN_TILE, D_RET), _tile_spec(FFN_TILE, d),
                  pl.BlockSpec((1, N_MOD, d), lambda b, t: (b, 0, 0)),
                  _const_spec((D_FOX + D_RET, d)), _const_spec((1, d)), _const_spec((1, d))] + _ffn_weight_specs(),
        out_specs=_tile_spec(FFN_TILE, d),
        out_shape=jax.ShapeDtypeStruct((b, s, d), F32),
        scratch_shapes=[pltpu.VMEM((FFN_TILE, D_FF), BF16)],
        compiler_params=pltpu.CompilerParams(dimension_semantics=("arbitrary", "arbitrary"),
                                             vmem_limit_bytes=VMEM_LIMIT),
        name="ffn2",
    )(foxt, ret, x1, mod3, wo, ln2g, ln2b, wg, wu, wd, lng, lnb)


def _inproj_kernel(x_ref, mod_ref, wqk_ref, wv_ref, wfl_ref, bfl_ref, wr_ref, tri_ref, paug_ref,
                   cos_ref, sin_ref, intra_ref, qdec_ref, kdec_ref, cdec_ref, gng_ref, gnb_ref,
                   qp_ref, kp_ref, fvt_ref, ret_ref, carry_ref, state_ref):
    tm = TOKEN_TILE
    half = FOX_HEAD_DIM

    @pl.when(pl.program_id(1) == 0)
    def _():
        carry_ref[...] = jnp.zeros_like(carry_ref)
        state_ref[...] = jnp.zeros_like(state_ref)

    sh = mod_ref[0, 3:4, :]
    sc = mod_ref[0, 4:5, :]
    h = (x_ref[0] * (1.0 + sc) + sh).astype(BF16)

    vt = jnp.dot(h, wv_ref[...], preferred_element_type=F32).T.astype(BF16)
    for hd in range(FOX_HEADS):
        fvt_ref[0, 0, hd * V_ROWS:hd * V_ROWS + half, :] = vt[hd * half:(hd + 1) * half]
        fvt_ref[0, 0, hd * V_ROWS + half:(hd + 1) * V_ROWS, :] = jnp.ones((BF16_ROWS, tm), BF16)

    z = jnp.dot(h, wfl_ref[...], preferred_element_type=F32) + bfl_ref[...]
    logf = jnp.minimum(z, 0.0) - jnp.log1p(jnp.exp(-jnp.abs(z)))
    tri = tri_ref[...]
    cum = carry_ref[...]
    for part in _split3(logf):
        cum = cum + jnp.dot(tri, part, preferred_element_type=F32)
    carry_ref[...] = cum[tm - 1:tm, :]

    chi, cmid, clo = _split3(cum * LOG2E)
    lane = lax.broadcasted_iota(jnp.int32, (tm, LANES), 1)
    packed = jnp.where(lane < 8, chi.astype(F32),
             jnp.where(lane < 16, pltpu.roll(cmid.astype(F32), 8, 1),
             jnp.where(lane < 24, pltpu.roll(clo.astype(F32), 16, 1),
             jnp.where(lane == 24, 1.0, 0.0))))
    aug = jnp.dot(packed.astype(BF16), paug_ref[...], preferred_element_type=F32)

    qk = jnp.dot(h, wqk_ref[...], preferred_element_type=F32)
    for hd in range(FOX_HEADS):
        data_lo, aug_lo = _aug_lanes(hd)
        is_data = (lane >= data_lo) & (lane < data_lo + half)
        is_aug = (lane >= aug_lo) & (lane < aug_lo + AUG_COLS)
        src = (hd // 2) * LANES
        out = slice(hd * LANES, (hd + 1) * LANES)
        kp_ref[0, :, out] = jnp.where(is_data, qk[:, D_FOX + src:D_FOX + src + LANES],
                                      jnp.where(is_aug, aug[:, :LANES], 0.0)).astype(BF16)
        qp_ref[0, :, out] = jnp.where(is_data, qk[:, src:src + LANES],
                                      jnp.where(is_aug, aug[:, LANES:], 0.0)).astype(BF16)

    r_all = jnp.dot(h, wr_ref[...], preferred_element_type=F32)
    cos2 = cos_ref[...]
    sin2 = sin_ref[...]
    c = RET_CHUNK
    for hd in range(RET_HEADS):
        lo_, hi_ = hd * RET_HEAD_DIM, (hd + 1) * RET_HEAD_DIM
        tq = r_all[:, lo_:hi_]
        tk = r_all[:, D_RET + lo_:D_RET + hi_]
        q_r = tq * cos2 + pltpu.roll(tq, RET_HEAD_DIM // 2, 1) * sin2
        k_r = (tk * cos2 + pltpu.roll(tk, RET_HEAD_DIM // 2, 1) * sin2) * (RET_HEAD_DIM ** -0.5)
        v = r_all[:, 2 * D_RET + lo_:2 * D_RET + hi_]
        gate = r_all[:, 3 * D_RET + lo_:3 * D_RET + hi_]
        state = state_ref[hd]
        for ci in range(tm // c):
            rows = slice(ci * c, (ci + 1) * c)
            qc = q_r[rows].astype(BF16)
            kc = k_r[rows]
            vc = v[rows].astype(BF16)
            s = lax.dot_general(qc, kc.astype(BF16), NT_DIMS, preferred_element_type=F32) * intra_ref[hd]
            o = (jnp.dot(s.astype(BF16), vc, preferred_element_type=F32)
                 + qdec_ref[hd] * jnp.dot(qc, state.astype(BF16), preferred_element_type=F32))
            state = state * cdec_ref[hd] + lax.dot_general(
                (kc * kdec_ref[hd]).astype(BF16), vc, TN_DIMS, preferred_element_type=F32)
            mu = jnp.mean(o, axis=-1, keepdims=True)
            d = o - mu
            var = jnp.mean(d * d, axis=-1, keepdims=True)
            yn = d * lax.rsqrt(var + LN_EPS) * gng_ref[:, lo_:hi_] + gnb_ref[:, lo_:hi_]
            ret_ref[0, rows, lo_:hi_] = (_silu(gate[rows]) * yn).astype(BF16)
        state_ref[hd] = state


def _inproj(x1, mod3, wqk, wv, wfl, bfl, wr, tri, paug, cos2, sin2, intra, qdec, kdec, cdec, gng, gnb):
    b, s, d = x1.shape
    tm = TOKEN_TILE
    nt = s // tm
    c = RET_CHUNK
    in_specs = [
        _tile_spec(tm, d), pl.BlockSpec((1, N_MOD, d), lambda b, t: (b, 0, 0)),
        _const_spec((d, 2 * D_FOX)), _const_spec((d, D_FOX)), _const_spec((d, LANES)), _const_spec((1, LANES)),
        _const_spec((d, 4 * D_RET)), _const_spec((tm, tm)), _const_spec((LANES, 2 * LANES)),
        pl.BlockSpec((tm, LANES), lambda b, t: (t, 0)), pl.BlockSpec((tm, LANES), lambda b, t: (t, 0)),
        _const_spec((RET_HEADS, c, c)), _const_spec((RET_HEADS, c, LANES)), _const_spec((RET_HEADS, c, LANES)),
        _const_spec((RET_HEADS, 1, LANES)), _const_spec((1, D_RET)), _const_spec((1, D_RET)),
    ]
    out_specs = [
        _tile_spec(tm, FOX_HEADS * LANES), _tile_spec(tm, FOX_HEADS * LANES),
        pl.BlockSpec((1, 1, FOX_HEADS * V_ROWS, tm), lambda b, t: (b, t, 0, 0)),
        _tile_spec(tm, D_RET),
    ]
    out_shape = [
        jax.ShapeDtypeStruct((b, s, FOX_HEADS * LANES), BF16), jax.ShapeDtypeStruct((b, s, FOX_HEADS * LANES), BF16),
        jax.ShapeDtypeStruct((b, nt, FOX_HEADS * V_ROWS, tm), BF16),
        jax.ShapeDtypeStruct((b, s, D_RET), BF16),
    ]
    return pl.pallas_call(
        _inproj_kernel,
        grid=(b, nt),
        in_specs=in_specs, out_specs=out_specs, out_shape=out_shape,
        scratch_shapes=[pltpu.VMEM((1, LANES), F32),
                        pltpu.VMEM((RET_HEADS, RET_HEAD_DIM, RET_HEAD_DIM), F32)],
        compiler_params=pltpu.CompilerParams(dimension_semantics=("arbitrary", "arbitrary"),
                                             vmem_limit_bytes=VMEM_LIMIT),
        name="inproj",
    )(x1, mod3, wqk, wv, wfl, bfl, wr, tri, paug, cos2, sin2, intra, qdec, kdec, cdec, gng, gnb)


def _fox_kernel(qp_ref, kp_ref, fvt_ref, o_ref, s_ref, p_ref, acc_ref, mask_ref, stat_ref):
    t = ATTN_TILE
    nq = qp_ref.shape[1] // t
    half = FOX_HEAD_DIM

    k_pos = lax.broadcasted_iota(jnp.int32, (t, t), 0)
    q_pos = lax.broadcasted_iota(jnp.int32, (t, t), 1)
    mask_ref[0] = jnp.zeros((t, t), F32)
    mask_ref[1] = jnp.where(k_pos <= q_pos, 0.0, -jnp.inf)
    s_ref[...] = jnp.zeros_like(s_ref)
    p_ref[...] = jnp.zeros_like(p_ref)
    acc_ref[...] = jnp.zeros_like(acc_ref)
    stat_ref[...] = jnp.zeros_like(stat_ref)

    def next_step(q, k):
        wrap = k >= q
        q2 = jnp.where(wrap, q + 1, q)
        k2 = jnp.where(wrap, 0, k + 1)
        done = q2 >= nq
        return jnp.where(done, nq - 1, q2), jnp.where(done, nq - 1, k2)

    def body(i, carry):
        qa, ka, qb, kb, qc, kc = carry
        krows = pl.ds(pl.multiple_of(ka * t, t), t)
        qrows = pl.ds(pl.multiple_of(qa * t, t), t)
        diag = (ka == qa).astype(jnp.int32)
        for hh in range(ATTN_HEADS):
            lanes = slice(hh * LANES, (hh + 1) * LANES)
            m = stat_ref[hh, 0:1, :]
            alpha1 = stat_ref[hh, 1:2, :]
            alpha2 = stat_ref[hh, 2:3, :]
            vt = fvt_ref[0, kc, hh * V_ROWS:(hh + 1) * V_ROWS, :]
            acc = alpha2 * acc_ref[hh] + jnp.dot(vt, p_ref[hh], preferred_element_type=F32)
            acc_ref[hh] = acc
            o_ref[0, qc, hh * half:(hh + 1) * half, :] = (acc[:half] / acc[half:half + 1]).astype(BF16)

            p_ref[hh] = jnp.exp2(s_ref[hh] - m).astype(BF16)

            s = lax.dot_general(kp_ref[0, krows, lanes], qp_ref[0, qrows, lanes], NT_DIMS,
                                preferred_element_type=F32) + mask_ref[diag]
            s_ref[hh] = s
            m_prev = jnp.where(ka == 0, -jnp.inf, m)
            m_new = jnp.maximum(m_prev, jnp.max(s, axis=0, keepdims=True))
            stat_ref[hh, 0:1, :] = m_new
            stat_ref[hh, 1:2, :] = jnp.exp2(m_prev - m_new)
            stat_ref[hh, 2:3, :] = alpha1
        qn, kn = next_step(qa, ka)
        return qn, kn, qa, ka, qb, kb

    n_steps = nq * (nq + 1) // 2
    lax.fori_loop(0, n_steps + 2, body, (jnp.int32(0),) * 6)


def _fox(qp, kp, fvt):
    b, s, _ = qp.shape
    t = ATTN_TILE
    nt = s // t
    g = ATTN_HEADS
    return pl.pallas_call(
        _fox_kernel,
        grid=(b, FOX_HEADS // g),
        in_specs=[pl.BlockSpec((1, s, g * LANES), lambda b, p: (b, 0, p)),
                  pl.BlockSpec((1, s, g * LANES), lambda b, p: (b, 0, p)),
                  pl.BlockSpec((1, nt, g * V_ROWS, t), lambda b, p: (b, 0, p, 0))],
        out_specs=pl.BlockSpec((1, nt, g * FOX_HEAD_DIM, t), lambda b, p: (b, 0, p, 0)),
        out_shape=jax.ShapeDtypeStruct((b, nt, D_FOX, t), BF16),
        scratch_shapes=[pltpu.VMEM((g, t, t), F32), pltpu.VMEM((g, t, t), BF16),
                        pltpu.VMEM((g, V_ROWS, t), F32), pltpu.VMEM((2, t, t), F32),
                        pltpu.VMEM((g, 8, t), F32)],
        compiler_params=pltpu.CompilerParams(dimension_semantics=("arbitrary", "arbitrary"),
                                             vmem_limit_bytes=VMEM_LIMIT),
        name="fox",
    )(qp, kp, fvt)


def _aug_placement():
    p = np.zeros((LANES, 2 * LANES), np.float32)
    for head in range(FOX_HEADS):
        _, base = _aug_lanes(head)
        for j in range(3):
            p[24, base + j] = 1.0
            p[8 * j + head, base + 3 + j] = -1.0
            p[8 * j + head, LANES + base + j] = 1.0
            p[24, LANES + base + 3 + j] = 1.0
    return jnp.asarray(p, BF16)


def _rotary_tables(seq):
    half = RET_HEAD_DIM // 2
    inv_freq = ROPE_BASE ** (-jnp.arange(half, dtype=F32) / half)
    ang = jnp.arange(seq, dtype=F32)[:, None] * inv_freq[None, :]
    cos, sin = jnp.cos(ang), jnp.sin(ang)
    return jnp.concatenate([cos, cos], axis=-1), jnp.concatenate([-sin, sin], axis=-1)


def _retention_tables():
    c = RET_CHUNK
    log_gamma = jnp.log1p(-jnp.power(2.0, -5.0 - jnp.arange(RET_HEADS, dtype=F32)))
    idx = jnp.arange(c, dtype=F32)
    diff = idx[:, None] - idx[None, :]
    intra = jnp.where(diff >= 0, jnp.exp(log_gamma[:, None, None] * jnp.maximum(diff, 0.0)), 0.0)
    qdec = jnp.exp(log_gamma[:, None] * (idx + 1.0))[..., None]
    kdec = jnp.exp(log_gamma[:, None] * (c - 1.0 - idx))[..., None]
    cdec = jnp.exp(log_gamma * c)[:, None, None]
    bc = lambda a: jnp.broadcast_to(a, a.shape[:2] + (LANES,))
    return intra, bc(qdec), bc(kdec), bc(cdec)


def kernel(x, c, w_ada, b_ada, ffn1_w_gate, ffn1_w_up, ffn1_w_down, ln1_g, ln1_b, w_in, fox_b_f, ret_gn_g,
           ret_gn_b, w_o, ln2_g, ln2_b, ffn2_w_gate, ffn2_w_up, ffn2_w_down, ln3_g, ln3_b):
    batch, seq, d = x.shape
    assert d == D_MODEL and seq % FFN_TILE == 0 and FFN_TILE % TOKEN_TILE == 0 and w_ada.shape[0] == DEPTH
    layer = 0

    c_pad = jnp.pad(c, ((0, 8 - batch), (0, 0)))
    mod3 = _ada(c_pad, w_ada[layer], b_ada[layer][None, :])[:batch].reshape(batch, N_MOD, d)

    x1 = _ffn(x, mod3, ffn1_w_gate[layer].astype(BF16), ffn1_w_up[layer].astype(BF16),
              ffn1_w_down[layer].astype(BF16), ln1_g[layer][None, :], ln1_b[layer][None, :], 0)

    w = w_in[layer]
    o_fl = 3 * D_FOX
    o_r = o_fl + FOX_HEADS
    q_scale = LOG2E * FOX_HEAD_DIM ** -0.5
    wqk = jnp.concatenate([w[:, :D_FOX] * q_scale, w[:, D_FOX:2 * D_FOX]], axis=1).astype(BF16)
    wv = w[:, 2 * D_FOX:3 * D_FOX].astype(BF16)
    wfl = jnp.pad(w[:, o_fl:o_r], ((0, 0), (0, LANES - FOX_HEADS))).astype(BF16)
    bfl = jnp.pad(fox_b_f[layer], (0, LANES - FOX_HEADS))[None, :]
    wr = w[:, o_r:o_r + 4 * D_RET].astype(BF16)
    tri = jnp.asarray(np.tril(np.ones((TOKEN_TILE, TOKEN_TILE), np.float32)), BF16)
    cos2, sin2 = _rotary_tables(seq)
    intra, qdec, kdec, cdec = _retention_tables()
    qp, kp, fvt, ret = _inproj(x1, mod3, wqk, wv, wfl, bfl, wr, tri, _aug_placement(), cos2, sin2,
                               intra, qdec, kdec, cdec, ret_gn_g[layer][None, :], ret_gn_b[layer][None, :])

    foxt = _fox(qp, kp, fvt)

    return _mix_ffn(foxt, ret, x1, mod3, w_o[layer].astype(BF16), ln2_g[layer][None, :], ln2_b[layer][None, :],
                    ffn2_w_gate[layer].astype(BF16), ffn2_w_up[layer].astype(BF16),
                    ffn2_w_down[layer].astype(BF16), ln3_g[layer][None, :], ln3_b[layer][None, :])
```

```python
import functools
import math

import numpy as np
import jax
import jax.numpy as jnp
from jax import lax
from jax.experimental import pallas as pl
from jax.experimental.pallas import tpu as pltpu

F32 = jnp.float32
BF16 = jnp.bfloat16

D_MODEL = 1024
D_FF = 2816
D_FOX = 512
D_RET = 512
FOX_HEADS = 8
FOX_HEAD_DIM = 64
RET_HEADS = 4
RET_HEAD_DIM = 128
N_MOD = 9
ROPE_BASE = 10000.0
LN_EPS = 1e-5
DEPTH = 1
DEEPNORM_ALPHA = (2.0 * DEPTH) ** 0.25
FFN_RES_WEIGHT = 0.5
LOG2E = math.log2(math.e)

LANES = 128
BF16_ROWS = 16
TOKEN_TILE = 512
FFN_TILE = 1024
FF_CHUNK = 256
RET_CHUNK = 256
ATTN_TILE = TOKEN_TILE
ATTN_HEADS = 2
ATTN_UNROLL = 2
AUG_COLS = 6
V_ROWS = FOX_HEAD_DIM + BF16_ROWS
VMEM_LIMIT = 56 * 2 ** 20

NT_DIMS = (((1,), (1,)), ((), ()))
TN_DIMS = (((0,), (0,)), ((), ()))


def _silu(v):
    return v / (1.0 + jnp.exp(-v))


def _layer_norm(y, g, b):
    mu = jnp.mean(y, axis=-1, keepdims=True)
    d = y - mu
    var = jnp.mean(d * d, axis=-1, keepdims=True)
    return d * lax.rsqrt(var + LN_EPS) * g + b


def _split3(v):
    hi = v.astype(BF16)
    r = v - hi.astype(F32)
    mid = r.astype(BF16)
    lo = (r - mid.astype(F32)).astype(BF16)
    return hi, mid, lo


def _const_spec(shape):
    zeros = (0,) * len(shape)
    return pl.BlockSpec(shape, lambda *_: zeros, pipeline_mode=pl.Buffered(1))


def _aug_lanes(head):
    half = FOX_HEAD_DIM
    pair, odd = divmod(head, 2)
    data_lo = half if odd else 0
    aug_lo = (0 if odd else half) + AUG_COLS * pair
    return data_lo, aug_lo


def _ada_kernel(c_ref, w_ref, b_ref, o_ref):
    ca = _silu(c_ref[...]).astype(BF16)
    o_ref[...] = jnp.dot(ca, w_ref[...].astype(BF16), preferred_element_type=F32) + b_ref[...]


def _ada(c_pad, w, b):
    rows, d = c_pad.shape
    n = w.shape[1]
    tn = 1024
    return pl.pallas_call(
        _ada_kernel,
        grid=(n // tn,),
        in_specs=[pl.BlockSpec((rows, d), lambda j: (0, 0)),
                  pl.BlockSpec((d, tn), lambda j: (0, j)),
                  pl.BlockSpec((1, tn), lambda j: (0, j))],
        out_specs=pl.BlockSpec((rows, tn), lambda j: (0, j)),
        out_shape=jax.ShapeDtypeStruct((rows, n), F32),
        compiler_params=pltpu.CompilerParams(dimension_semantics=("arbitrary",)),
        name="ada",
    )(c_pad, w, b)


def _swiglu_ln(x, mod_ref, mod_base, wg_ref, wu_ref, wd_ref, lng_ref, lnb_ref, act_ref):
    sh = mod_ref[0, mod_base:mod_base + 1, :]
    sc = mod_ref[0, mod_base + 1:mod_base + 2, :]
    g = mod_ref[0, mod_base + 2:mod_base + 3, :]
    h = (x * (1.0 + sc) + sh).astype(BF16)
    for j in range(D_FF // FF_CHUNK):
        cols = slice(j * FF_CHUNK, (j + 1) * FF_CHUNK)
        gate = jnp.dot(h, wg_ref[:, cols], preferred_element_type=F32)
        up = jnp.dot(h, wu_ref[:, cols], preferred_element_type=F32)
        act_ref[:, cols] = (_silu(gate) * up).astype(BF16)
    f = jnp.dot(act_ref[...], wd_ref[...], preferred_element_type=F32)
    y = DEEPNORM_ALPHA * x + (FFN_RES_WEIGHT * g) * f
    return _layer_norm(y, lng_ref[...], lnb_ref[...])


def _ffn_kernel(x_ref, mod_ref, wg_ref, wu_ref, wd_ref, lng_ref, lnb_ref, o_ref, act_ref, *, mod_base):
    o_ref[0] = _swiglu_ln(x_ref[0], mod_ref, mod_base, wg_ref, wu_ref, wd_ref, lng_ref, lnb_ref, act_ref)


def _mix_ffn_kernel(foxt_ref, ret_ref, x_ref, mod_ref, wo_ref, ln2g_ref, ln2b_ref,
                    wg_ref, wu_ref, wd_ref, lng_ref, lnb_ref, o_ref, act_ref):
    wo_fox = wo_ref[0:D_FOX, :]
    m_fox = [lax.dot_general(foxt_ref[0, i], wo_fox, TN_DIMS, preferred_element_type=F32)
             for i in range(FFN_TILE // ATTN_TILE)]
    m = (jnp.concatenate(m_fox, axis=0)
         + jnp.dot(ret_ref[0], wo_ref[D_FOX:D_FOX + D_RET, :], preferred_element_type=F32))
    g2 = mod_ref[0, 5:6, :]
    x2 = _layer_norm(DEEPNORM_ALPHA * x_ref[0] + g2 * m, ln2g_ref[...], ln2b_ref[...])
    o_ref[0] = _swiglu_ln(x2, mod_ref, 6, wg_ref, wu_ref, wd_ref, lng_ref, lnb_ref, act_ref)


def _tile_spec(tile, width):
    return pl.BlockSpec((1, tile, width), lambda b, t: (b, t, 0))


def _ffn_weight_specs():
    return [_const_spec((D_MODEL, D_FF)), _const_spec((D_MODEL, D_FF)), _const_spec((D_FF, D_MODEL)),
            _const_spec((1, D_MODEL)), _const_spec((1, D_MODEL))]


def _ffn(x, mod3, wg, wu, wd, lng, lnb, mod_base):
    b, s, d = x.shape
    return pl.pallas_call(
        functools.partial(_ffn_kernel, mod_base=mod_base),
        grid=(b, s // FFN_TILE),
        in_specs=[_tile_spec(FFN_TILE, d), pl.BlockSpec((1, N_MOD, d), lambda b, t: (b, 0, 0))]
        + _ffn_weight_specs(),
        out_specs=_tile_spec(FFN_TILE, d),
        out_shape=jax.ShapeDtypeStruct((b, s, d), F32),
        scratch_shapes=[pltpu.VMEM((FFN_TILE, D_FF), BF16)],
        compiler_params=pltpu.CompilerParams(dimension_semantics=("arbitrary", "arbitrary"),
                                             vmem_limit_bytes=VMEM_LIMIT),
        name="ffn1",
    )(x, mod3, wg, wu, wd, lng, lnb)


def _mix_ffn(foxt, ret, x1, mod3, wo, ln2g, ln2b, wg, wu, wd, lng, lnb):
    b, s, d = x1.shape
    per = FFN_TILE // ATTN_TILE
    return pl.pallas_call(
        _mix_ffn_kernel,
        grid=(b, s // FFN_TILE),
        in_specs=[pl.BlockSpec((1, per, D_FOX, ATTN_TILE), lambda b, t: (b, t, 0, 0)),
                  _tile_spec(FFN_TILE, D_RET), _tile_spec(FFN_TILE, d),
                  pl.BlockSpec((1, N_MOD, d), lambda b, t: (b, 0, 0)),
                  _const_spec((D_FOX + D_RET, d)), _const_spec((1, d)), _const_spec((1, d))] + _ffn_weight_specs(),
        out_specs=_tile_spec(FFN_TILE, d),
        out_shape=jax.ShapeDtypeStruct((b, s, d), F32),
        scratch_shapes=[pltpu.VMEM((FFN_TILE, D_FF), BF16)],
        compiler_params=pltpu.CompilerParams(dimension_semantics=("arbitrary", "arbitrary"),
                                             vmem_limit_bytes=VMEM_LIMIT),
        name="ffn2",
    )(foxt, ret, x1, mod3, wo, ln2g, ln2b, wg, wu, wd, lng, lnb)


def _inproj_kernel(x_ref, mod_ref, wqk_ref, wv_ref, wfl_ref, bfl_ref, wr_ref, tri_ref, paug_ref,
                   cos_ref, sin_ref, intra_ref, qdec_ref, kdec_ref, cdec_ref, gng_ref, gnb_ref,
                   qpt_ref, kp_ref, fvt_ref, ret_ref, carry_ref, state_ref):
    tm = TOKEN_TILE
    half = FOX_HEAD_DIM

    @pl.when(pl.program_id(1) == 0)
    def _():
        carry_ref[...] = jnp.zeros_like(carry_ref)
        state_ref[...] = jnp.zeros_like(state_ref)

    sh = mod_ref[0, 3:4, :]
    sc = mod_ref[0, 4:5, :]
    h = (x_ref[0] * (1.0 + sc) + sh).astype(BF16)

    vt = jnp.dot(h, wv_ref[...], preferred_element_type=F32).T.astype(BF16)
    for hd in range(FOX_HEADS):
        fvt_ref[0, 0, hd * V_ROWS:hd * V_ROWS + half, :] = vt[hd * half:(hd + 1) * half]
        fvt_ref[0, 0, hd * V_ROWS + half:(hd + 1) * V_ROWS, :] = jnp.ones((BF16_ROWS, tm), BF16)

    z = jnp.dot(h, wfl_ref[...], preferred_element_type=F32) + bfl_ref[...]
    logf = jnp.minimum(z, 0.0) - jnp.log1p(jnp.exp(-jnp.abs(z)))
    tri = tri_ref[...]
    cum = carry_ref[...]
    for part in _split3(logf):
        cum = cum + jnp.dot(tri, part, preferred_element_type=F32)
    carry_ref[...] = cum[tm - 1:tm, :]

    chi, cmid, clo = _split3(cum * LOG2E)
    lane = lax.broadcasted_iota(jnp.int32, (tm, LANES), 1)
    packed = jnp.where(lane < 8, chi.astype(F32),
             jnp.where(lane < 16, pltpu.roll(cmid.astype(F32), 8, 1),
             jnp.where(lane < 24, pltpu.roll(clo.astype(F32), 16, 1),
             jnp.where(lane == 24, 1.0, 0.0))))
    aug = jnp.dot(packed.astype(BF16), paug_ref[...], preferred_element_type=F32)

    qk = jnp.dot(h, wqk_ref[...], preferred_element_type=F32)
    for hd in range(FOX_HEADS):
        data_lo, aug_lo = _aug_lanes(hd)
        is_data = (lane >= data_lo) & (lane < data_lo + half)
        is_aug = (lane >= aug_lo) & (lane < aug_lo + AUG_COLS)
        src = (hd // 2) * LANES
        out = slice(hd * LANES, (hd + 1) * LANES)
        kp_ref[0, :, out] = jnp.where(is_data, qk[:, D_FOX + src:D_FOX + src + LANES],
                                      jnp.where(is_aug, aug[:, :LANES], 0.0)).astype(BF16)
        q_head = jnp.where(is_data, qk[:, src:src + LANES], jnp.where(is_aug, aug[:, LANES:], 0.0))
        qpt_ref[0, 0, out, :] = q_head.T.astype(BF16)

    r_all = jnp.dot(h, wr_ref[...], preferred_element_type=F32)
    cos2 = cos_ref[...]
    sin2 = sin_ref[...]
    c = RET_CHUNK
    for hd in range(RET_HEADS):
        lo_, hi_ = hd * RET_HEAD_DIM, (hd + 1) * RET_HEAD_DIM
        tq = r_all[:, lo_:hi_]
        tk = r_all[:, D_RET + lo_:D_RET + hi_]
        q_r = tq * cos2 + pltpu.roll(tq, RET_HEAD_DIM // 2, 1) * sin2
        k_r = (tk * cos2 + pltpu.roll(tk, RET_HEAD_DIM // 2, 1) * sin2) * (RET_HEAD_DIM ** -0.5)
        v = r_all[:, 2 * D_RET + lo_:2 * D_RET + hi_]
        gate = r_all[:, 3 * D_RET + lo_:3 * D_RET + hi_]
        state = state_ref[hd]
        for ci in range(tm // c):
            rows = slice(ci * c, (ci + 1) * c)
            qc = q_r[rows].astype(BF16)
            kc = k_r[rows]
            vc = v[rows].astype(BF16)
            s = lax.dot_general(qc, kc.astype(BF16), NT_DIMS, preferred_element_type=F32) * intra_ref[hd]
            o = (jnp.dot(s.astype(BF16), vc, preferred_element_type=F32)
                 + qdec_ref[hd] * jnp.dot(qc, state.astype(BF16), preferred_element_type=F32))
            state = state * cdec_ref[hd] + lax.dot_general(
                (kc * kdec_ref[hd]).astype(BF16), vc, TN_DIMS, preferred_element_type=F32)
            mu = jnp.mean(o, axis=-1, keepdims=True)
            d = o - mu
            var = jnp.mean(d * d, axis=-1, keepdims=True)
            yn = d * lax.rsqrt(var + LN_EPS) * gng_ref[:, lo_:hi_] + gnb_ref[:, lo_:hi_]
            ret_ref[0, rows, lo_:hi_] = (_silu(gate[rows]) * yn).astype(BF16)
        state_ref[hd] = state


def _inproj(x1, mod3, wqk, wv, wfl, bfl, wr, tri, paug, cos2, sin2, intra, qdec, kdec, cdec, gng, gnb):
    b, s, d = x1.shape
    tm = TOKEN_TILE
    nt = s // tm
    c = RET_CHUNK
    in_specs = [
        _tile_spec(tm, d), pl.BlockSpec((1, N_MOD, d), lambda b, t: (b, 0, 0)),
        _const_spec((d, 2 * D_FOX)), _const_spec((d, D_FOX)), _const_spec((d, LANES)), _const_spec((1, LANES)),
        _const_spec((d, 4 * D_RET)), _const_spec((tm, tm)), _const_spec((LANES, 2 * LANES)),
        pl.BlockSpec((tm, LANES), lambda b, t: (t, 0)), pl.BlockSpec((tm, LANES), lambda b, t: (t, 0)),
        _const_spec((RET_HEADS, c, c)), _const_spec((RET_HEADS, c, LANES)), _const_spec((RET_HEADS, c, LANES)),
        _const_spec((RET_HEADS, 1, LANES)), _const_spec((1, D_RET)), _const_spec((1, D_RET)),
    ]
    out_specs = [
        pl.BlockSpec((1, 1, FOX_HEADS * LANES, tm), lambda b, t: (b, t, 0, 0)), _tile_spec(tm, FOX_HEADS * LANES),
        pl.BlockSpec((1, 1, FOX_HEADS * V_ROWS, tm), lambda b, t: (b, t, 0, 0)),
        _tile_spec(tm, D_RET),
    ]
    out_shape = [
        jax.ShapeDtypeStruct((b, nt, FOX_HEADS * LANES, tm), BF16), jax.ShapeDtypeStruct((b, s, FOX_HEADS * LANES), BF16),
        jax.ShapeDtypeStruct((b, nt, FOX_HEADS * V_ROWS, tm), BF16),
        jax.ShapeDtypeStruct((b, s, D_RET), BF16),
    ]
    return pl.pallas_call(
        _inproj_kernel,
        grid=(b, nt),
        in_specs=in_specs, out_specs=out_specs, out_shape=out_shape,
        scratch_shapes=[pltpu.VMEM((1, LANES), F32),
                        pltpu.VMEM((RET_HEADS, RET_HEAD_DIM, RET_HEAD_DIM), F32)],
        compiler_params=pltpu.CompilerParams(dimension_semantics=("arbitrary", "arbitrary"),
                                             vmem_limit_bytes=VMEM_LIMIT),
        name="inproj",
    )(x1, mod3, wqk, wv, wfl, bfl, wr, tri, paug, cos2, sin2, intra, qdec, kdec, cdec, gng, gnb)


def _fox_kernel(qpt_ref, kp_ref, fvt_ref, o_ref, s_ref, p_ref, acc_ref, mask_ref, stat_ref):
    t = ATTN_TILE
    nq = qpt_ref.shape[1]
    half = FOX_HEAD_DIM

    k_pos = lax.broadcasted_iota(jnp.int32, (t, t), 0)
    q_pos = lax.broadcasted_iota(jnp.int32, (t, t), 1)
    mask_ref[0] = jnp.zeros((t, t), F32)
    mask_ref[1] = jnp.where(k_pos <= q_pos, 0.0, -jnp.inf)
    s_ref[...] = jnp.zeros_like(s_ref)
    p_ref[...] = jnp.zeros_like(p_ref)
    acc_ref[...] = jnp.zeros_like(acc_ref)
    stat_ref[...] = jnp.zeros_like(stat_ref)

    def next_step(q, k):
        wrap = k >= q
        q2 = jnp.where(wrap, q + 1, q)
        k2 = jnp.where(wrap, 0, k + 1)
        done = q2 >= nq
        return jnp.where(done, nq - 1, q2), jnp.where(done, nq - 1, k2)

    unroll = ATTN_UNROLL
    row_m, row_alpha_b, row_alpha_c = 1, 1 + unroll, 1 + 2 * unroll

    def stat(hh, row):
        return stat_ref[hh, row:row + 1, :]

    def body(j, carry):
        a_first, b_steps, c_steps = carry
        a_steps = []
        nxt = a_first
        for _ in range(unroll):
            a_steps.append(nxt)
            nxt = next_step(*nxt)
        for hh in range(ATTN_HEADS):
            lanes = slice(hh * LANES, (hh + 1) * LANES)
            alpha_b = [stat(hh, row_alpha_b + u) for u in range(unroll)]

            acc = acc_ref[hh]
            for u, (qc, kc) in enumerate(c_steps):
                vt = fvt_ref[0, kc, hh * V_ROWS:(hh + 1) * V_ROWS, :]
                acc = stat(hh, row_alpha_c + u) * acc + jnp.dot(vt, p_ref[u * ATTN_HEADS + hh],
                                                                preferred_element_type=F32)
                o_ref[0, qc, hh * half:(hh + 1) * half, :] = (acc[:half] / acc[half:half + 1]).astype(BF16)
            acc_ref[hh] = acc

            for u in range(unroll):
                buf = u * ATTN_HEADS + hh
                p_ref[buf] = jnp.exp2(s_ref[buf] - stat(hh, row_m + u)).astype(BF16)

            m = stat(hh, 0)
            for u, (qa, ka) in enumerate(a_steps):
                krows = pl.ds(pl.multiple_of(ka * t, t), t)
                diag = (ka == qa).astype(jnp.int32)
                s = jnp.dot(kp_ref[0, krows, lanes], qpt_ref[0, qa, hh * LANES:(hh + 1) * LANES, :],
                            preferred_element_type=F32) + mask_ref[diag]
                s_ref[u * ATTN_HEADS + hh] = s
                m_prev = jnp.where(ka == 0, -jnp.inf, m)
                m = jnp.maximum(m_prev, jnp.max(s, axis=0, keepdims=True))
                stat_ref[hh, row_m + u:row_m + u + 1, :] = m
                stat_ref[hh, row_alpha_b + u:row_alpha_b + u + 1, :] = jnp.exp2(m_prev - m)
                stat_ref[hh, row_alpha_c + u:row_alpha_c + u + 1, :] = alpha_b[u]
            stat_ref[hh, 0:1, :] = m
        return nxt, tuple(a_steps), b_steps

    n_steps = nq * (nq + 1) // 2
    assert n_steps % unroll == 0 and 1 + 3 * unroll <= stat_ref.shape[1]
    origin = (jnp.int32(0), jnp.int32(0))
    lax.fori_loop(0, n_steps // unroll + 2, body, (origin, (origin,) * unroll, (origin,) * unroll))


def _fox(qpt, kp, fvt):
    b, s, _ = kp.shape
    t = ATTN_TILE
    nt = s // t
    g = ATTN_HEADS
    return pl.pallas_call(
        _fox_kernel,
        grid=(b, FOX_HEADS // g),
        in_specs=[pl.BlockSpec((1, nt, g * LANES, t), lambda b, p: (b, 0, p, 0)),
                  pl.BlockSpec((1, s, g * LANES), lambda b, p: (b, 0, p)),
                  pl.BlockSpec((1, nt, g * V_ROWS, t), lambda b, p: (b, 0, p, 0))],
        out_specs=pl.BlockSpec((1, nt, g * FOX_HEAD_DIM, t), lambda b, p: (b, 0, p, 0)),
        out_shape=jax.ShapeDtypeStruct((b, nt, D_FOX, t), BF16),
        scratch_shapes=[pltpu.VMEM((ATTN_UNROLL * g, t, t), F32), pltpu.VMEM((ATTN_UNROLL * g, t, t), BF16),
                        pltpu.VMEM((g, V_ROWS, t), F32), pltpu.VMEM((2, t, t), F32),
                        pltpu.VMEM((g, 8, t), F32)],
        compiler_params=pltpu.CompilerParams(dimension_semantics=("arbitrary", "arbitrary"),
                                             vmem_limit_bytes=VMEM_LIMIT),
        name="fox",
    )(qpt, kp, fvt)


def _aug_placement():
    p = np.zeros((LANES, 2 * LANES), np.float32)
    for head in range(FOX_HEADS):
        _, base = _aug_lanes(head)
        for j in range(3):
            p[24, base + j] = 1.0
            p[8 * j + head, base + 3 + j] = -1.0
            p[8 * j + head, LANES + base + j] = 1.0
            p[24, LANES + base + 3 + j] = 1.0
    return jnp.asarray(p, BF16)


def _rotary_tables(seq):
    half = RET_HEAD_DIM // 2
    inv_freq = ROPE_BASE ** (-jnp.arange(half, dtype=F32) / half)
    ang = jnp.arange(seq, dtype=F32)[:, None] * inv_freq[None, :]
    cos, sin = jnp.cos(ang), jnp.sin(ang)
    return jnp.concatenate([cos, cos], axis=-1), jnp.concatenate([-sin, sin], axis=-1)


def _retention_tables():
    c = RET_CHUNK
    log_gamma = jnp.log1p(-jnp.power(2.0, -5.0 - jnp.arange(RET_HEADS, dtype=F32)))
    idx = jnp.arange(c, dtype=F32)
    diff = idx[:, None] - idx[None, :]
    intra = jnp.where(diff >= 0, jnp.exp(log_gamma[:, None, None] * jnp.maximum(diff, 0.0)), 0.0)
    qdec = jnp.exp(log_gamma[:, None] * (idx + 1.0))[..., None]
    kdec = jnp.exp(log_gamma[:, None] * (c - 1.0 - idx))[..., None]
    cdec = jnp.exp(log_gamma * c)[:, None, None]
    bc = lambda a: jnp.broadcast_to(a, a.shape[:2] + (LANES,))
    return intra, bc(qdec), bc(kdec), bc(cdec)


def kernel(x, c, w_ada, b_ada, ffn1_w_gate, ffn1_w_up, ffn1_w_down, ln1_g, ln1_b, w_in, fox_b_f, ret_gn_g,
           ret_gn_b, w_o, ln2_g, ln2_b, ffn2_w_gate, ffn2_w_up, ffn2_w_down, ln3_g, ln3_b):
    batch, seq, d = x.shape
    assert d == D_MODEL and seq % FFN_TILE == 0 and FFN_TILE % TOKEN_TILE == 0 and w_ada.shape[0] == DEPTH
    layer = 0

    c_pad = jnp.pad(c, ((0, 8 - batch), (0, 0)))
    mod3 = _ada(c_pad, w_ada[layer], b_ada[layer][None, :])[:batch].reshape(batch, N_MOD, d)

    x1 = _ffn(x, mod3, ffn1_w_gate[layer].astype(BF16), ffn1_w_up[layer].astype(BF16),
              ffn1_w_down[layer].astype(BF16), ln1_g[layer][None, :], ln1_b[layer][None, :], 0)

    w = w_in[layer]
    o_fl = 3 * D_FOX
    o_r = o_fl + FOX_HEADS
    q_scale = LOG2E * FOX_HEAD_DIM ** -0.5
    wqk = jnp.concatenate([w[:, :D_FOX] * q_scale, w[:, D_FOX:2 * D_FOX]], axis=1).astype(BF16)
    wv = w[:, 2 * D_FOX:3 * D_FOX].astype(BF16)
    wfl = jnp.pad(w[:, o_fl:o_r], ((0, 0), (0, LANES - FOX_HEADS))).astype(BF16)
    bfl = jnp.pad(fox_b_f[layer], (0, LANES - FOX_HEADS))[None, :]
    wr = w[:, o_r:o_r + 4 * D_RET].astype(BF16)
    tri = jnp.asarray(np.tril(np.ones((TOKEN_TILE, TOKEN_TILE), np.float32)), BF16)
    cos2, sin2 = _rotary_tables(seq)
    intra, qdec, kdec, cdec = _retention_tables()
    qpt, kp, fvt, ret = _inproj(x1, mod3, wqk, wv, wfl, bfl, wr, tri, _aug_placement(), cos2, sin2,
                               intra, qdec, kdec, cdec, ret_gn_g[layer][None, :], ret_gn_b[layer][None, :])

    foxt = _fox(qpt, kp, fvt)

    return _mix_ffn(foxt, ret, x1, mod3, w_o[layer].astype(BF16), ln2_g[layer][None, :], ln2_b[layer][None, :],
                    ffn2_w_gate[layer].astype(BF16), ffn2_w_up[layer].astype(BF16),
                    ffn2_w_down[layer].astype(BF16), ln3_g[layer][None, :], ln3_b[layer][None, :])
```

```python
import functools
import math

import numpy as np
import jax
import jax.numpy as jnp
from jax import lax
from jax.experimental import pallas as pl
from jax.experimental.pallas import tpu as pltpu

F32 = jnp.float32
BF16 = jnp.bfloat16

D_MODEL = 1024
D_FF = 2816
D_FOX = 512
D_RET = 512
FOX_HEADS = 8
FOX_HEAD_DIM = 64
RET_HEADS = 4
RET_HEAD_DIM = 128
N_MOD = 9
ROPE_BASE = 10000.0
LN_EPS = 1e-5
DEPTH = 1
DEEPNORM_ALPHA = (2.0 * DEPTH) ** 0.25
FFN_RES_WEIGHT = 0.5
LOG2E = math.log2(math.e)

LANES = 128
BF16_ROWS = 16
TOKEN_TILE = 512
FFN_TILE = 1024
FF_CHUNK = 256
RET_CHUNK = 256
ATTN_TILE = TOKEN_TILE
ATTN_HEADS = 2
ATTN_UNROLL = 2
AUG_COLS = 6
V_ROWS = FOX_HEAD_DIM + BF16_ROWS
VMEM_LIMIT = 56 * 2 ** 20

NT_DIMS = (((1,), (1,)), ((), ()))
TN_DIMS = (((0,), (0,)), ((), ()))


def _silu(v):
    return v / (1.0 + jnp.exp(-v))


def _layer_norm(y, g, b):
    mu = jnp.mean(y, axis=-1, keepdims=True)
    d = y - mu
    var = jnp.mean(d * d, axis=-1, keepdims=True)
    return d * lax.rsqrt(var + LN_EPS) * g + b


def _split3(v):
    hi = v.astype(BF16)
    r = v - hi.astype(F32)
    mid = r.astype(BF16)
    lo = (r - mid.astype(F32)).astype(BF16)
    return hi, mid, lo


def _stack_terms(v, tail, rows):
    hi, mid, lo = _split3(v)
    parts = [hi.astype(F32), mid.astype(F32), lo.astype(F32), tail]
    if rows > 32:
        parts.append(jnp.zeros((rows - 32, v.shape[1]), F32))
    return jnp.concatenate(parts, axis=0).astype(BF16)


def _const_spec(shape):
    zeros = (0,) * len(shape)
    return pl.BlockSpec(shape, lambda *_: zeros, pipeline_mode=pl.Buffered(1))


def _aug_lanes(head):
    half = FOX_HEAD_DIM
    pair, odd = divmod(head, 2)
    data_lo = half if odd else 0
    aug_lo = (0 if odd else half) + AUG_COLS * pair
    return data_lo, aug_lo


def _ada_kernel(c_ref, w_ref, b_ref, o_ref):
    ca = _silu(c_ref[...]).astype(BF16)
    o_ref[...] = jnp.dot(ca, w_ref[...].astype(BF16), preferred_element_type=F32) + b_ref[...]


def _ada(c_pad, w, b):
    rows, d = c_pad.shape
    n = w.shape[1]
    tn = 1024
    return pl.pallas_call(
        _ada_kernel,
        grid=(n // tn,),
        in_specs=[pl.BlockSpec((rows, d), lambda j: (0, 0)),
                  pl.BlockSpec((d, tn), lambda j: (0, j)),
                  pl.BlockSpec((1, tn), lambda j: (0, j))],
        out_specs=pl.BlockSpec((rows, tn), lambda j: (0, j)),
        out_shape=jax.ShapeDtypeStruct((rows, n), F32),
        compiler_params=pltpu.CompilerParams(dimension_semantics=("arbitrary",)),
        name="ada",
    )(c_pad, w, b)


def _swiglu_ln(x, mod_ref, mod_base, wg_ref, wu_ref, wd_ref, lng_ref, lnb_ref, act_ref):
    sh = mod_ref[0, mod_base:mod_base + 1, :]
    sc = mod_ref[0, mod_base + 1:mod_base + 2, :]
    g = mod_ref[0, mod_base + 2:mod_base + 3, :]
    h = (x * (1.0 + sc) + sh).astype(BF16)
    for j in range(D_FF // FF_CHUNK):
        cols = slice(j * FF_CHUNK, (j + 1) * FF_CHUNK)
        gate = jnp.dot(h, wg_ref[:, cols], preferred_element_type=F32)
        up = jnp.dot(h, wu_ref[:, cols], preferred_element_type=F32)
        act_ref[:, cols] = (_silu(gate) * up).astype(BF16)
    f = jnp.dot(act_ref[...], wd_ref[...], preferred_element_type=F32)
    y = DEEPNORM_ALPHA * x + (FFN_RES_WEIGHT * g) * f
    return _layer_norm(y, lng_ref[...], lnb_ref[...])


def _ffn_kernel(x_ref, mod_ref, wg_ref, wu_ref, wd_ref, lng_ref, lnb_ref, o_ref, act_ref, *, mod_base):
    o_ref[0] = _swiglu_ln(x_ref[0], mod_ref, mod_base, wg_ref, wu_ref, wd_ref, lng_ref, lnb_ref, act_ref)


def _mix_ffn_kernel(foxt_ref, ret_ref, x_ref, mod_ref, wo_ref, ln2g_ref, ln2b_ref,
                    wg_ref, wu_ref, wd_ref, lng_ref, lnb_ref, o_ref, act_ref):
    wo_fox = wo_ref[0:D_FOX, :]
    m_fox = [lax.dot_general(foxt_ref[0, i], wo_fox, TN_DIMS, preferred_element_type=F32)
             for i in range(FFN_TILE // ATTN_TILE)]
    m = (jnp.concatenate(m_fox, axis=0)
         + jnp.dot(ret_ref[0], wo_ref[D_FOX:D_FOX + D_RET, :], preferred_element_type=F32))
    g2 = mod_ref[0, 5:6, :]
    x2 = _layer_norm(DEEPNORM_ALPHA * x_ref[0] + g2 * m, ln2g_ref[...], ln2b_ref[...])
    o_ref[0] = _swiglu_ln(x2, mod_ref, 6, wg_ref, wu_ref, wd_ref, lng_ref, lnb_ref, act_ref)


def _tile_spec(tile, width):
    return pl.BlockSpec((1, tile, width), lambda b, t: (b, t, 0))


def _ffn_weight_specs():
    return [_const_spec((D_MODEL, D_FF)), _const_spec((D_MODEL, D_FF)), _const_spec((D_FF, D_MODEL)),
            _const_spec((1, D_MODEL)), _const_spec((1, D_MODEL))]


def _ffn(x, mod3, wg, wu, wd, lng, lnb, mod_base):
    b, s, d = x.shape
    return pl.pallas_call(
        functools.partial(_ffn_kernel, mod_base=mod_base),
        grid=(b, s // FFN_TILE),
        in_specs=[_tile_spec(FFN_TILE, d), pl.BlockSpec((1, N_MOD, d), lambda b, t: (b, 0, 0))]
        + _ffn_weight_specs(),
        out_specs=_tile_spec(FFN_TILE, d),
        out_shape=jax.ShapeDtypeStruct((b, s, d), F32),
        scratch_shapes=[pltpu.VMEM((FFN_TILE, D_FF), BF16)],
        compiler_params=pltpu.CompilerParams(dimension_semantics=("arbitrary", "arbitrary"),
                                             vmem_limit_bytes=VMEM_LIMIT),
        name="ffn1",
    )(x, mod3, wg, wu, wd, lng, lnb)


def _mix_ffn(foxt, ret, x1, mod3, wo, ln2g, ln2b, wg, wu, wd, lng, lnb):
    b, s, d = x1.shape
    per = FFN_TILE // ATTN_TILE
    return pl.pallas_call(
        _mix_ffn_kernel,
        grid=(b, s // FFN_TILE),
        in_specs=[pl.BlockSpec((1, per, D_FOX, ATTN_TILE), lambda b, t: (b, t, 0, 0)),
                  _tile_spec(FFN_TILE, D_RET), _tile_spec(FFN_TILE, d),
                  pl.BlockSpec((1, N_MOD, d), lambda b, t: (b, 0, 0)),
                  _const_spec((D_FOX + D_RET, d)), _const_spec((1, d)), _const_spec((1, d))] + _ffn_weight_specs(),
        out_specs=_tile_spec(FFN_TILE, d),
        out_shape=jax.ShapeDtypeStruct((b, s, d), F32),
        scratch_shapes=[pltpu.VMEM((FFN_TILE, D_FF), BF16)],
        compiler_params=pltpu.CompilerParams(dimension_semantics=("arbitrary", "arbitrary"),
                                             vmem_limit_bytes=VMEM_LIMIT),
        name="ffn2",
    )(foxt, ret, x1, mod3, wo, ln2g, ln2b, wg, wu, wd, lng, lnb)


def _inproj_kernel(x_ref, mod_ref, wqk_ref, wv_ref, wfl_ref, bfl_ref, wr_ref, triu_ref, paugk_ref, paugqt_ref,
                   cos_ref, sin_ref, intra_ref, qdec_ref, kdec_ref, cdec_ref, gng_ref, gnb_ref,
                   qpt_ref, kp_ref, fvt_ref, ret_ref, carry_ref, state_ref):
    tm = TOKEN_TILE
    half = FOX_HEAD_DIM

    @pl.when(pl.program_id(1) == 0)
    def _():
        carry_ref[...] = jnp.zeros_like(carry_ref)
        state_ref[...] = jnp.zeros_like(state_ref)

    sh = mod_ref[0, 3:4, :]
    sc = mod_ref[0, 4:5, :]
    h = (x_ref[0] * (1.0 + sc) + sh).astype(BF16)

    vt = jnp.dot(h, wv_ref[...], preferred_element_type=F32).T.astype(BF16)
    for hd in range(FOX_HEADS):
        fvt_ref[0, 0, hd * V_ROWS:hd * V_ROWS + half, :] = vt[hd * half:(hd + 1) * half]
        fvt_ref[0, 0, hd * V_ROWS + half:(hd + 1) * V_ROWS, :] = jnp.ones((BF16_ROWS, tm), BF16)

    z = jnp.dot(h, wfl_ref[...], preferred_element_type=F32) + bfl_ref[...]
    logf = jnp.minimum(z, 0.0) - jnp.log1p(jnp.exp(-jnp.abs(z)))
    logf_t = logf.T[0:FOX_HEADS, :]
    terms = _stack_terms(logf_t, jnp.zeros((FOX_HEADS, tm), F32), 4 * FOX_HEADS)
    sums = jnp.dot(terms, triu_ref[...], preferred_element_type=F32)
    cum_t = (sums[0:8] + sums[8:16] + sums[16:24]) + carry_ref[:, 0:1]
    carry_ref[...] = jnp.broadcast_to(cum_t[:, tm - 1:tm], carry_ref.shape)

    row8 = lax.broadcasted_iota(jnp.int32, (FOX_HEADS, tm), 0)
    packed_t = _stack_terms(cum_t * LOG2E, jnp.where(row8 == 0, 1.0, 0.0), LANES)
    aug_k = lax.dot_general(packed_t, paugk_ref[...], TN_DIMS, preferred_element_type=F32)
    aug_qt = jnp.dot(paugqt_ref[...], packed_t, preferred_element_type=F32)

    qk = jnp.dot(h, wqk_ref[...], preferred_element_type=F32)
    lane = lax.broadcasted_iota(jnp.int32, (tm, LANES), 1)
    row = lax.broadcasted_iota(jnp.int32, (LANES, tm), 0)
    q_t = [qk[:, pr * LANES:(pr + 1) * LANES].T for pr in range(FOX_HEADS // 2)]
    for hd in range(FOX_HEADS):
        data_lo, aug_lo = _aug_lanes(hd)
        src = (hd // 2) * LANES
        out = slice(hd * LANES, (hd + 1) * LANES)
        is_data = (lane >= data_lo) & (lane < data_lo + half)
        is_aug = (lane >= aug_lo) & (lane < aug_lo + AUG_COLS)
        kp_ref[0, :, out] = jnp.where(is_data, qk[:, D_FOX + src:D_FOX + src + LANES],
                                      jnp.where(is_aug, aug_k, 0.0)).astype(BF16)
        is_data_t = (row >= data_lo) & (row < data_lo + half)
        is_aug_t = (row >= aug_lo) & (row < aug_lo + AUG_COLS)
        qpt_ref[0, 0, out, :] = jnp.where(is_data_t, q_t[hd // 2],
                                          jnp.where(is_aug_t, aug_qt, 0.0)).astype(BF16)

    r_all = jnp.dot(h, wr_ref[...], preferred_element_type=F32)
    cos2 = cos_ref[...]
    sin2 = sin_ref[...]
    c = RET_CHUNK
    for hd in range(RET_HEADS):
        lo_, hi_ = hd * RET_HEAD_DIM, (hd + 1) * RET_HEAD_DIM
        tq = r_all[:, lo_:hi_]
        tk = r_all[:, D_RET + lo_:D_RET + hi_]
        q_r = tq * cos2 + pltpu.roll(tq, RET_HEAD_DIM // 2, 1) * sin2
        k_r = (tk * cos2 + pltpu.roll(tk, RET_HEAD_DIM // 2, 1) * sin2) * (RET_HEAD_DIM ** -0.5)
        v = r_all[:, 2 * D_RET + lo_:2 * D_RET + hi_]
        gate = r_all[:, 3 * D_RET + lo_:3 * D_RET + hi_]
        state = state_ref[hd]
        for ci in range(tm // c):
            rows = slice(ci * c, (ci + 1) * c)
            qc = q_r[rows].astype(BF16)
            kc = k_r[rows]
            vc = v[rows].astype(BF16)
            s = lax.dot_general(qc, kc.astype(BF16), NT_DIMS, preferred_element_type=F32) * intra_ref[hd]
            o = (jnp.dot(s.astype(BF16), vc, preferred_element_type=F32)
                 + qdec_ref[hd] * jnp.dot(qc, state.astype(BF16), preferred_element_type=F32))
            state = state * cdec_ref[hd] + lax.dot_general(
                (kc * kdec_ref[hd]).astype(BF16), vc, TN_DIMS, preferred_element_type=F32)
            mu = jnp.mean(o, axis=-1, keepdims=True)
            d = o - mu
            var = jnp.mean(d * d, axis=-1, keepdims=True)
            yn = d * lax.rsqrt(var + LN_EPS) * gng_ref[:, lo_:hi_] + gnb_ref[:, lo_:hi_]
            ret_ref[0, rows, lo_:hi_] = (_silu(gate[rows]) * yn).astype(BF16)
        state_ref[hd] = state


def _inproj(x1, mod3, wqk, wv, wfl, bfl, wr, triu, paugk, paugqt, cos2, sin2, intra, qdec, kdec, cdec, gng, gnb):
    b, s, d = x1.shape
    tm = TOKEN_TILE
    nt = s // tm
    c = RET_CHUNK
    in_specs = [
        _tile_spec(tm, d), pl.BlockSpec((1, N_MOD, d), lambda b, t: (b, 0, 0)),
        _const_spec((d, 2 * D_FOX)), _const_spec((d, D_FOX)), _const_spec((d, LANES)), _const_spec((1, LANES)),
        _const_spec((d, 4 * D_RET)), _const_spec((tm, tm)), _const_spec((LANES, LANES)), _const_spec((LANES, LANES)),
        pl.BlockSpec((tm, LANES), lambda b, t: (t, 0)), pl.BlockSpec((tm, LANES), lambda b, t: (t, 0)),
        _const_spec((RET_HEADS, c, c)), _const_spec((RET_HEADS, c, LANES)), _const_spec((RET_HEADS, c, LANES)),
        _const_spec((RET_HEADS, 1, LANES)), _const_spec((1, D_RET)), _const_spec((1, D_RET)),
    ]
    out_specs = [
        pl.BlockSpec((1, 1, FOX_HEADS * LANES, tm), lambda b, t: (b, t, 0, 0)), _tile_spec(tm, FOX_HEADS * LANES),
        pl.BlockSpec((1, 1, FOX_HEADS * V_ROWS, tm), lambda b, t: (b, t, 0, 0)),
        _tile_spec(tm, D_RET),
    ]
    out_shape = [
        jax.ShapeDtypeStruct((b, nt, FOX_HEADS * LANES, tm), BF16), jax.ShapeDtypeStruct((b, s, FOX_HEADS * LANES), BF16),
        jax.ShapeDtypeStruct((b, nt, FOX_HEADS * V_ROWS, tm), BF16),
        jax.ShapeDtypeStruct((b, s, D_RET), BF16),
    ]
    return pl.pallas_call(
        _inproj_kernel,
        grid=(b, nt),
        in_specs=in_specs, out_specs=out_specs, out_shape=out_shape,
        scratch_shapes=[pltpu.VMEM((FOX_HEADS, LANES), F32),
                        pltpu.VMEM((RET_HEADS, RET_HEAD_DIM, RET_HEAD_DIM), F32)],
        compiler_params=pltpu.CompilerParams(dimension_semantics=("arbitrary", "arbitrary"),
                                             vmem_limit_bytes=VMEM_LIMIT),
        name="inproj",
    )(x1, mod3, wqk, wv, wfl, bfl, wr, triu, paugk, paugqt, cos2, sin2, intra, qdec, kdec, cdec, gng, gnb)


def _fox_kernel(qpt_ref, kp_ref, fvt_ref, o_ref, s_ref, p_ref, acc_ref, mask_ref, stat_ref):
    t = ATTN_TILE
    nq = qpt_ref.shape[1]
    half = FOX_HEAD_DIM

    k_pos = lax.broadcasted_iota(jnp.int32, (t, t), 0)
    q_pos = lax.broadcasted_iota(jnp.int32, (t, t), 1)
    mask_ref[0] = jnp.zeros((t, t), F32)
    mask_ref[1] = jnp.where(k_pos <= q_pos, 0.0, -jnp.inf)
    s_ref[...] = jnp.zeros_like(s_ref)
    p_ref[...] = jnp.zeros_like(p_ref)
    acc_ref[...] = jnp.zeros_like(acc_ref)
    stat_ref[...] = jnp.zeros_like(stat_ref)

    def next_step(q, k):
        wrap = k >= q
        q2 = jnp.where(wrap, q + 1, q)
        k2 = jnp.where(wrap, 0, k + 1)
        done = q2 >= nq
        return jnp.where(done, nq - 1, q2), jnp.where(done, nq - 1, k2)

    unroll = ATTN_UNROLL
    row_m, row_alpha_b, row_alpha_c = 1, 1 + unroll, 1 + 2 * unroll

    def stat(hh, row):
        return stat_ref[hh, row:row + 1, :]

    def body(j, carry):
        a_first, b_steps, c_steps = carry
        a_steps = []
        nxt = a_first
        for _ in range(unroll):
            a_steps.append(nxt)
            nxt = next_step(*nxt)
        for hh in range(ATTN_HEADS):
            lanes = slice(hh * LANES, (hh + 1) * LANES)
            alpha_b = [stat(hh, row_alpha_b + u) for u in range(unroll)]

            acc = acc_ref[hh]
            for u, (qc, kc) in enumerate(c_steps):
                vt = fvt_ref[0, kc, hh * V_ROWS:(hh + 1) * V_ROWS, :]
                acc = stat(hh, row_alpha_c + u) * acc + jnp.dot(vt, p_ref[u * ATTN_HEADS + hh],
                                                                preferred_element_type=F32)
                o_ref[0, qc, hh * half:(hh + 1) * half, :] = (acc[:half] / acc[half:half + 1]).astype(BF16)
            acc_ref[hh] = acc

            for u in range(unroll):
                buf = u * ATTN_HEADS + hh
                p_ref[buf] = jnp.exp2(s_ref[buf] - stat(hh, row_m + u)).astype(BF16)

            m = stat(hh, 0)
            for u, (qa, ka) in enumerate(a_steps):
                krows = pl.ds(pl.multiple_of(ka * t, t), t)
                diag = (ka == qa).astype(jnp.int32)
                s = jnp.dot(kp_ref[0, krows, lanes], qpt_ref[0, qa, hh * LANES:(hh + 1) * LANES, :],
                            preferred_element_type=F32) + mask_ref[diag]
                s_ref[u * ATTN_HEADS + hh] = s
                m_prev = jnp.where(ka == 0, -jnp.inf, m)
                m = jnp.maximum(m_prev, jnp.max(s, axis=0, keepdims=True))
                stat_ref[hh, row_m + u:row_m + u + 1, :] = m
                stat_ref[hh, row_alpha_b + u:row_alpha_b + u + 1, :] = jnp.exp2(m_prev - m)
                stat_ref[hh, row_alpha_c + u:row_alpha_c + u + 1, :] = alpha_b[u]
            stat_ref[hh, 0:1, :] = m
        return nxt, tuple(a_steps), b_steps

    n_steps = nq * (nq + 1) // 2
    assert n_steps % unroll == 0 and 1 + 3 * unroll <= stat_ref.shape[1]
    origin = (jnp.int32(0), jnp.int32(0))
    lax.fori_loop(0, n_steps // unroll + 2, body, (origin, (origin,) * unroll, (origin,) * unroll))


def _fox(qpt, kp, fvt):
    b, s, _ = kp.shape
    t = ATTN_TILE
    nt = s // t
    g = ATTN_HEADS
    return pl.pallas_call(
        _fox_kernel,
        grid=(b, FOX_HEADS // g),
        in_specs=[pl.BlockSpec((1, nt, g * LANES, t), lambda b, p: (b, 0, p, 0)),
                  pl.BlockSpec((1, s, g * LANES), lambda b, p: (b, 0, p)),
                  pl.BlockSpec((1, nt, g * V_ROWS, t), lambda b, p: (b, 0, p, 0))],
        out_specs=pl.BlockSpec((1, nt, g * FOX_HEAD_DIM, t), lambda b, p: (b, 0, p, 0)),
        out_shape=jax.ShapeDtypeStruct((b, nt, D_FOX, t), BF16),
        scratch_shapes=[pltpu.VMEM((ATTN_UNROLL * g, t, t), F32), pltpu.VMEM((ATTN_UNROLL * g, t, t), BF16),
                        pltpu.VMEM((g, V_ROWS, t), F32), pltpu.VMEM((2, t, t), F32),
                        pltpu.VMEM((g, 8, t), F32)],
        compiler_params=pltpu.CompilerParams(dimension_semantics=("arbitrary", "arbitrary"),
                                             vmem_limit_bytes=VMEM_LIMIT),
        name="fox",
    )(qpt, kp, fvt)


def _aug_placement():
    pk = np.zeros((LANES, LANES), np.float32)
    pq = np.zeros((LANES, LANES), np.float32)
    for head in range(FOX_HEADS):
        _, base = _aug_lanes(head)
        for j in range(3):
            pk[24, base + j] = 1.0
            pk[8 * j + head, base + 3 + j] = -1.0
            pq[base + j, 8 * j + head] = 1.0
            pq[base + 3 + j, 24] = 1.0
    return jnp.asarray(pk, BF16), jnp.asarray(pq, BF16)


def _rotary_tables(seq):
    half = RET_HEAD_DIM // 2
    inv_freq = ROPE_BASE ** (-np.arange(half, dtype=np.float64) / half)
    ang = np.arange(seq, dtype=np.float64)[:, None] * inv_freq[None, :]
    cos, sin = np.cos(ang), np.sin(ang)
    return (jnp.asarray(np.concatenate([cos, cos], axis=-1), F32),
            jnp.asarray(np.concatenate([-sin, sin], axis=-1), F32))


def _retention_tables():
    c = RET_CHUNK
    log_gamma = np.log1p(-np.power(2.0, -5.0 - np.arange(RET_HEADS, dtype=np.float64)))
    idx = np.arange(c, dtype=np.float64)
    diff = idx[:, None] - idx[None, :]
    intra = np.where(diff >= 0, np.exp(log_gamma[:, None, None] * np.maximum(diff, 0.0)), 0.0)
    qdec = np.exp(log_gamma[:, None] * (idx + 1.0))[..., None]
    kdec = np.exp(log_gamma[:, None] * (c - 1.0 - idx))[..., None]
    cdec = np.exp(log_gamma * c)[:, None, None]
    bc = lambda a: jnp.asarray(np.broadcast_to(a, a.shape[:2] + (LANES,)), F32)
    return jnp.asarray(intra, F32), bc(qdec), bc(kdec), bc(cdec)


def kernel(x, c, w_ada, b_ada, ffn1_w_gate, ffn1_w_up, ffn1_w_down, ln1_g, ln1_b, w_in, fox_b_f, ret_gn_g,
           ret_gn_b, w_o, ln2_g, ln2_b, ffn2_w_gate, ffn2_w_up, ffn2_w_down, ln3_g, ln3_b):
    batch, seq, d = x.shape
    assert d == D_MODEL and seq % FFN_TILE == 0 and FFN_TILE % TOKEN_TILE == 0 and w_ada.shape[0] == DEPTH
    layer = 0

    c_pad = jnp.pad(c, ((0, 8 - batch), (0, 0)))
    mod3 = _ada(c_pad, w_ada[layer], b_ada[layer][None, :])[:batch].reshape(batch, N_MOD, d)

    x1 = _ffn(x, mod3, ffn1_w_gate[layer].astype(BF16), ffn1_w_up[layer].astype(BF16),
              ffn1_w_down[layer].astype(BF16), ln1_g[layer][None, :], ln1_b[layer][None, :], 0)

    w = w_in[layer]
    o_fl = 3 * D_FOX
    o_r = o_fl + FOX_HEADS
    q_scale = LOG2E * FOX_HEAD_DIM ** -0.5
    wqk = jnp.concatenate([w[:, :D_FOX] * q_scale, w[:, D_FOX:2 * D_FOX]], axis=1).astype(BF16)
    wv = w[:, 2 * D_FOX:3 * D_FOX].astype(BF16)
    wfl = jnp.pad(w[:, o_fl:o_r], ((0, 0), (0, LANES - FOX_HEADS))).astype(BF16)
    bfl = jnp.pad(fox_b_f[layer], (0, LANES - FOX_HEADS))[None, :]
    wr = w[:, o_r:o_r + 4 * D_RET].astype(BF16)
    triu = jnp.asarray(np.triu(np.ones((TOKEN_TILE, TOKEN_TILE), np.float32)), BF16)
    paugk, paugqt = _aug_placement()
    cos2, sin2 = _rotary_tables(seq)
    intra, qdec, kdec, cdec = _retention_tables()
    qpt, kp, fvt, ret = _inproj(x1, mod3, wqk, wv, wfl, bfl, wr, triu, paugk, paugqt, cos2, sin2,
                               intra, qdec, kdec, cdec, ret_gn_g[layer][None, :], ret_gn_b[layer][None, :])

    foxt = _fox(qpt, kp, fvt)

    return _mix_ffn(foxt, ret, x1, mod3, w_o[layer].astype(BF16), ln2_g[layer][None, :], ln2_b[layer][None, :],
                    ffn2_w_gate[layer].astype(BF16), ffn2_w_up[layer].astype(BF16),
                    ffn2_w_down[layer].astype(BF16), ln3_g[layer][None, :], ln3_b[layer][None, :])
```

```python
import functools
import math

import numpy as np
import jax
import jax.numpy as jnp
from jax import lax
from jax.experimental import pallas as pl
from jax.experimental.pallas import tpu as pltpu

F32 = jnp.float32
BF16 = jnp.bfloat16

D_MODEL = 1024
D_FF = 2816
D_FOX = 512
D_RET = 512
FOX_HEADS = 8
FOX_HEAD_DIM = 64
RET_HEADS = 4
RET_HEAD_DIM = 128
N_MOD = 9
ROPE_BASE = 10000.0
LN_EPS = 1e-5
DEPTH = 1
DEEPNORM_ALPHA = (2.0 * DEPTH) ** 0.25
FFN_RES_WEIGHT = 0.5
LOG2E = math.log2(math.e)

LANES = 128
BF16_ROWS = 16
TOKEN_TILE = 512
FFN_TILE = 1024
FFN_SUB_TILE = 256
FF_CHUNK = 256
RET_CHUNK = 256
ATTN_TILE = TOKEN_TILE
ATTN_HEADS = 2
ATTN_UNROLL = 2
AUG_COLS = 6
V_ROWS = FOX_HEAD_DIM + BF16_ROWS
VMEM_LIMIT = 56 * 2 ** 20

NT_DIMS = (((1,), (1,)), ((), ()))
TN_DIMS = (((0,), (0,)), ((), ()))


def _silu(v):
    return v / (1.0 + jnp.exp(-v))


def _layer_norm(y, g, b):
    mu = jnp.mean(y, axis=-1, keepdims=True)
    d = y - mu
    var = jnp.mean(d * d, axis=-1, keepdims=True)
    return d * lax.rsqrt(var + LN_EPS) * g + b


def _split3(v):
    hi = v.astype(BF16)
    r = v - hi.astype(F32)
    mid = r.astype(BF16)
    lo = (r - mid.astype(F32)).astype(BF16)
    return hi, mid, lo


def _stack_terms(v, tail, rows):
    hi, mid, lo = _split3(v)
    parts = [hi.astype(F32), mid.astype(F32), lo.astype(F32), tail]
    if rows > 32:
        parts.append(jnp.zeros((rows - 32, v.shape[1]), F32))
    return jnp.concatenate(parts, axis=0).astype(BF16)


def _const_spec(shape):
    zeros = (0,) * len(shape)
    return pl.BlockSpec(shape, lambda *_: zeros, pipeline_mode=pl.Buffered(1))


def _aug_lanes(head):
    half = FOX_HEAD_DIM
    pair, odd = divmod(head, 2)
    data_lo = half if odd else 0
    aug_lo = (0 if odd else half) + AUG_COLS * pair
    return data_lo, aug_lo


def _ada_kernel(c_ref, w_ref, b_ref, o_ref):
    ca = _silu(c_ref[...]).astype(BF16)
    o_ref[...] = jnp.dot(ca, w_ref[...].astype(BF16), preferred_element_type=F32) + b_ref[...]


def _ada(c_pad, w, b):
    rows, d = c_pad.shape
    n = w.shape[1]
    tn = 1024
    return pl.pallas_call(
        _ada_kernel,
        grid=(n // tn,),
        in_specs=[pl.BlockSpec((rows, d), lambda j: (0, 0)),
                  pl.BlockSpec((d, tn), lambda j: (0, j)),
                  pl.BlockSpec((1, tn), lambda j: (0, j))],
        out_specs=pl.BlockSpec((rows, tn), lambda j: (0, j)),
        out_shape=jax.ShapeDtypeStruct((rows, n), F32),
        compiler_params=pltpu.CompilerParams(dimension_semantics=("arbitrary",)),
        name="ada",
    )(c_pad, w, b)


def _swiglu_ln_tile(load_x, store_out, mod_ref, mod_base, wg_ref, wu_ref, wd_ref, lng_ref, lnb_ref, act_ref):
    n_sub = FFN_TILE // FFN_SUB_TILE
    n_chunks = D_FF // FF_CHUNK
    lead_finish, lead_next = 2, 6
    sh = mod_ref[0, mod_base:mod_base + 1, :]
    sc = mod_ref[0, mod_base + 1:mod_base + 2, :]
    g = mod_ref[0, mod_base + 2:mod_base + 3, :]
    xs, hs = {}, {}

    def rows_of(i):
        return slice(i * FFN_SUB_TILE, (i + 1) * FFN_SUB_TILE)

    def begin(i):
        xs[i] = load_x(i)
        hs[i] = (xs[i] * (1.0 + sc) + sh).astype(BF16)

    def chunks(i, first, last):
        for j in range(first, last):
            cols = slice(j * FF_CHUNK, (j + 1) * FF_CHUNK)
            gate = jnp.dot(hs[i], wg_ref[:, cols], preferred_element_type=F32)
            up = jnp.dot(hs[i], wu_ref[:, cols], preferred_element_type=F32)
            act_ref[rows_of(i), cols] = (_silu(gate) * up).astype(BF16)

    def finish(i):
        f = jnp.dot(act_ref[rows_of(i), :], wd_ref[...], preferred_element_type=F32)
        y = DEEPNORM_ALPHA * xs.pop(i) + (FFN_RES_WEIGHT * g) * f
        store_out(i, _layer_norm(y, lng_ref[...], lnb_ref[...]))

    begin(0)
    for i in range(n_sub):
        chunks(i, 0, lead_finish)
        if i > 0:
            finish(i - 1)
        chunks(i, lead_finish, lead_next)
        if i + 1 < n_sub:
            begin(i + 1)
        chunks(i, lead_next, n_chunks)
    finish(n_sub - 1)


def _ffn_kernel(x_ref, mod_ref, wg_ref, wu_ref, wd_ref, lng_ref, lnb_ref, o_ref, act_ref, *, mod_base):
    def rows_of(i):
        return slice(i * FFN_SUB_TILE, (i + 1) * FFN_SUB_TILE)

    def load_x(i):
        return x_ref[0, rows_of(i)]

    def store_out(i, value):
        o_ref[0, rows_of(i)] = value

    _swiglu_ln_tile(load_x, store_out, mod_ref, mod_base, wg_ref, wu_ref, wd_ref, lng_ref, lnb_ref, act_ref)


def _mix_ffn_kernel(foxt_ref, ret_ref, x_ref, mod_ref, wo_ref, ln2g_ref, ln2b_ref,
                    wg_ref, wu_ref, wd_ref, lng_ref, lnb_ref, o_ref, act_ref):
    g2 = mod_ref[0, 5:6, :]

    def rows_of(i):
        return slice(i * FFN_SUB_TILE, (i + 1) * FFN_SUB_TILE)

    def load_x(i):
        tile, part = divmod(i * FFN_SUB_TILE, ATTN_TILE)
        fox_t = foxt_ref[0, tile, :, part:part + FFN_SUB_TILE]
        m = (lax.dot_general(fox_t, wo_ref[0:D_FOX, :], TN_DIMS, preferred_element_type=F32)
             + jnp.dot(ret_ref[0, rows_of(i)], wo_ref[D_FOX:D_FOX + D_RET, :], preferred_element_type=F32))
        return _layer_norm(DEEPNORM_ALPHA * x_ref[0, rows_of(i)] + g2 * m, ln2g_ref[...], ln2b_ref[...])

    def store_out(i, value):
        o_ref[0, rows_of(i)] = value

    _swiglu_ln_tile(load_x, store_out, mod_ref, 6, wg_ref, wu_ref, wd_ref, lng_ref, lnb_ref, act_ref)


def _tile_spec(tile, width):
    return pl.BlockSpec((1, tile, width), lambda b, t: (b, t, 0))


def _ffn_weight_specs():
    return [_const_spec((D_MODEL, D_FF)), _const_spec((D_MODEL, D_FF)), _const_spec((D_FF, D_MODEL)),
            _const_spec((1, D_MODEL)), _const_spec((1, D_MODEL))]


def _ffn(x, mod3, wg, wu, wd, lng, lnb, mod_base):
    b, s, d = x.shape
    return pl.pallas_call(
        functools.partial(_ffn_kernel, mod_base=mod_base),
        grid=(b, s // FFN_TILE),
        in_specs=[_tile_spec(FFN_TILE, d), pl.BlockSpec((1, N_MOD, d), lambda b, t: (b, 0, 0))]
        + _ffn_weight_specs(),
        out_specs=_tile_spec(FFN_TILE, d),
        out_shape=jax.ShapeDtypeStruct((b, s, d), F32),
        scratch_shapes=[pltpu.VMEM((FFN_TILE, D_FF), BF16)],
        compiler_params=pltpu.CompilerParams(dimension_semantics=("arbitrary", "arbitrary"),
                                             vmem_limit_bytes=VMEM_LIMIT),
        name="ffn1",
    )(x, mod3, wg, wu, wd, lng, lnb)


def _mix_ffn(foxt, ret, x1, mod3, wo, ln2g, ln2b, wg, wu, wd, lng, lnb):
    b, s, d = x1.shape
    per = FFN_TILE // ATTN_TILE
    return pl.pallas_call(
        _mix_ffn_kernel,
        grid=(b, s // FFN_TILE),
        in_specs=[pl.BlockSpec((1, per, D_FOX, ATTN_TILE), lambda b, t: (b, t, 0, 0)),
                  _tile_spec(FFN_TILE, D_RET), _tile_spec(FFN_TILE, d),
                  pl.BlockSpec((1, N_MOD, d), lambda b, t: (b, 0, 0)),
                  _const_spec((D_FOX + D_RET, d)), _const_spec((1, d)), _const_spec((1, d))] + _ffn_weight_specs(),
        out_specs=_tile_spec(FFN_TILE, d),
        out_shape=jax.ShapeDtypeStruct((b, s, d), F32),
        scratch_shapes=[pltpu.VMEM((FFN_TILE, D_FF), BF16)],
        compiler_params=pltpu.CompilerParams(dimension_semantics=("arbitrary", "arbitrary"),
                                             vmem_limit_bytes=VMEM_LIMIT),
        name="ffn2",
    )(foxt, ret, x1, mod3, wo, ln2g, ln2b, wg, wu, wd, lng, lnb)


def _inproj_kernel(x_ref, mod_ref, wqk_ref, wv_ref, wfl_ref, bfl_ref, wr_ref, triu_ref, paugk_ref, paugqt_ref,
                   cos_ref, sin_ref, intra_ref, qdec_ref, kdec_ref, cdec_ref, gng_ref, gnb_ref,
                   qpt_ref, kp_ref, fvt_ref, ret_ref, carry_ref, state_ref):
    tm = TOKEN_TILE
    half = FOX_HEAD_DIM

    @pl.when(pl.program_id(1) == 0)
    def _():
        carry_ref[...] = jnp.zeros_like(carry_ref)
        state_ref[...] = jnp.zeros_like(state_ref)

    sh = mod_ref[0, 3:4, :]
    sc = mod_ref[0, 4:5, :]
    h = (x_ref[0] * (1.0 + sc) + sh).astype(BF16)

    r_all = jnp.dot(h, wr_ref[...], preferred_element_type=F32)
    cos2 = cos_ref[...]
    sin2 = sin_ref[...]
    c = RET_CHUNK
    for hd in range(RET_HEADS):
        lo_, hi_ = hd * RET_HEAD_DIM, (hd + 1) * RET_HEAD_DIM
        tq = r_all[:, lo_:hi_]
        tk = r_all[:, D_RET + lo_:D_RET + hi_]
        q_r = tq * cos2 + pltpu.roll(tq, RET_HEAD_DIM // 2, 1) * sin2
        k_r = (tk * cos2 + pltpu.roll(tk, RET_HEAD_DIM // 2, 1) * sin2) * (RET_HEAD_DIM ** -0.5)
        v = r_all[:, 2 * D_RET + lo_:2 * D_RET + hi_]
        gate = r_all[:, 3 * D_RET + lo_:3 * D_RET + hi_]
        state = state_ref[hd]
        for ci in range(tm // c):
            rows = slice(ci * c, (ci + 1) * c)
            qc = q_r[rows].astype(BF16)
            kc = k_r[rows]
            vc = v[rows].astype(BF16)
            s = lax.dot_general(qc, kc.astype(BF16), NT_DIMS, preferred_element_type=F32) * intra_ref[hd]
            o = (jnp.dot(s.astype(BF16), vc, preferred_element_type=F32)
                 + qdec_ref[hd] * jnp.dot(qc, state.astype(BF16), preferred_element_type=F32))
            state = state * cdec_ref[hd] + lax.dot_general(
                (kc * kdec_ref[hd]).astype(BF16), vc, TN_DIMS, preferred_element_type=F32)
            mu = jnp.mean(o, axis=-1, keepdims=True)
            d = o - mu
            var = jnp.mean(d * d, axis=-1, keepdims=True)
            yn = d * lax.rsqrt(var + LN_EPS) * gng_ref[:, lo_:hi_] + gnb_ref[:, lo_:hi_]
            ret_ref[0, rows, lo_:hi_] = (_silu(gate[rows]) * yn).astype(BF16)
        state_ref[hd] = state

    z = jnp.dot(h, wfl_ref[...], preferred_element_type=F32) + bfl_ref[...]
    logf = jnp.minimum(z, 0.0) - jnp.log1p(jnp.exp(-jnp.abs(z)))
    logf_t = logf.T[0:FOX_HEADS, :]
    terms = _stack_terms(logf_t, jnp.zeros((FOX_HEADS, tm), F32), 4 * FOX_HEADS)
    sums = jnp.dot(terms, triu_ref[...], preferred_element_type=F32)
    cum_t = (sums[0:8] + sums[8:16] + sums[16:24]) + carry_ref[:, 0:1]
    carry_ref[...] = jnp.broadcast_to(cum_t[:, tm - 1:tm], carry_ref.shape)

    row8 = lax.broadcasted_iota(jnp.int32, (FOX_HEADS, tm), 0)
    packed_t = _stack_terms(cum_t * LOG2E, jnp.where(row8 == 0, 1.0, 0.0), LANES)
    aug_k = lax.dot_general(packed_t, paugk_ref[...], TN_DIMS, preferred_element_type=F32)
    aug_qt = jnp.dot(paugqt_ref[...], packed_t, preferred_element_type=F32)

    qk = jnp.dot(h, wqk_ref[...], preferred_element_type=F32)
    lane = lax.broadcasted_iota(jnp.int32, (tm, LANES), 1)
    row = lax.broadcasted_iota(jnp.int32, (LANES, tm), 0)
    q_t = [qk[:, pr * LANES:(pr + 1) * LANES].T for pr in range(FOX_HEADS // 2)]
    for hd in range(FOX_HEADS):
        data_lo, aug_lo = _aug_lanes(hd)
        src = (hd // 2) * LANES
        out = slice(hd * LANES, (hd + 1) * LANES)
        is_data = (lane >= data_lo) & (lane < data_lo + half)
        is_aug = (lane >= aug_lo) & (lane < aug_lo + AUG_COLS)
        kp_ref[0, :, out] = jnp.where(is_data, qk[:, D_FOX + src:D_FOX + src + LANES],
                                      jnp.where(is_aug, aug_k, 0.0)).astype(BF16)
        is_data_t = (row >= data_lo) & (row < data_lo + half)
        is_aug_t = (row >= aug_lo) & (row < aug_lo + AUG_COLS)
        qpt_ref[0, 0, out, :] = jnp.where(is_data_t, q_t[hd // 2],
                                          jnp.where(is_aug_t, aug_qt, 0.0)).astype(BF16)

    vt = jnp.dot(h, wv_ref[...], preferred_element_type=F32).T.astype(BF16)
    for hd in range(FOX_HEADS):
        fvt_ref[0, 0, hd * V_ROWS:hd * V_ROWS + half, :] = vt[hd * half:(hd + 1) * half]
        fvt_ref[0, 0, hd * V_ROWS + half:(hd + 1) * V_ROWS, :] = jnp.ones((BF16_ROWS, tm), BF16)


def _inproj(x1, mod3, wqk, wv, wfl, bfl, wr, triu, paugk, paugqt, cos2, sin2, intra, qdec, kdec, cdec, gng, gnb):
    b, s, d = x1.shape
    tm = TOKEN_TILE
    nt = s // tm
    c = RET_CHUNK
    in_specs = [
        _tile_spec(tm, d), pl.BlockSpec((1, N_MOD, d), lambda b, t: (b, 0, 0)),
        _const_spec((d, 2 * D_FOX)), _const_spec((d, D_FOX)), _const_spec((d, LANES)), _const_spec((1, LANES)),
        _const_spec((d, 4 * D_RET)), _const_spec((tm, tm)), _const_spec((LANES, LANES)), _const_spec((LANES, LANES)),
        pl.BlockSpec((tm, LANES), lambda b, t: (t, 0)), pl.BlockSpec((tm, LANES), lambda b, t: (t, 0)),
        _const_spec((RET_HEADS, c, c)), _const_spec((RET_HEADS, c, LANES)), _const_spec((RET_HEADS, c, LANES)),
        _const_spec((RET_HEADS, 1, LANES)), _const_spec((1, D_RET)), _const_spec((1, D_RET)),
    ]
    out_specs = [
        pl.BlockSpec((1, 1, FOX_HEADS * LANES, tm), lambda b, t: (b, t, 0, 0)), _tile_spec(tm, FOX_HEADS * LANES),
        pl.BlockSpec((1, 1, FOX_HEADS * V_ROWS, tm), lambda b, t: (b, t, 0, 0)),
        _tile_spec(tm, D_RET),
    ]
    out_shape = [
        jax.ShapeDtypeStruct((b, nt, FOX_HEADS * LANES, tm), BF16), jax.ShapeDtypeStruct((b, s, FOX_HEADS * LANES), BF16),
        jax.ShapeDtypeStruct((b, nt, FOX_HEADS * V_ROWS, tm), BF16),
        jax.ShapeDtypeStruct((b, s, D_RET), BF16),
    ]
    return pl.pallas_call(
        _inproj_kernel,
        grid=(b, nt),
        in_specs=in_specs, out_specs=out_specs, out_shape=out_shape,
        scratch_shapes=[pltpu.VMEM((FOX_HEADS, LANES), F32),
                        pltpu.VMEM((RET_HEADS, RET_HEAD_DIM, RET_HEAD_DIM), F32)],
        compiler_params=pltpu.CompilerParams(dimension_semantics=("arbitrary", "arbitrary"),
                                             vmem_limit_bytes=VMEM_LIMIT),
        name="inproj",
    )(x1, mod3, wqk, wv, wfl, bfl, wr, triu, paugk, paugqt, cos2, sin2, intra, qdec, kdec, cdec, gng, gnb)


def _fox_kernel(qpt_ref, kp_ref, fvt_ref, o_ref, s_ref, p_ref, acc_ref, mask_ref, stat_ref):
    t = ATTN_TILE
    nq = qpt_ref.shape[1]
    half = FOX_HEAD_DIM

    k_pos = lax.broadcasted_iota(jnp.int32, (t, t), 0)
    q_pos = lax.broadcasted_iota(jnp.int32, (t, t), 1)
    mask_ref[0] = jnp.zeros((t, t), F32)
    mask_ref[1] = jnp.where(k_pos <= q_pos, 0.0, -jnp.inf)
    acc_ref[...] = jnp.zeros_like(acc_ref)
    stat_ref[...] = jnp.zeros_like(stat_ref)

    unroll = ATTN_UNROLL
    row_m, row_alpha_new, row_alpha_old = 1, 1 + unroll, 1 + 2 * unroll
    steps = [(q, k) for q in range(nq) for k in range(q + 1)]
    n_groups = len(steps) // unroll
    assert len(steps) % unroll == 0 and n_groups >= 4 and 1 + 3 * unroll <= stat_ref.shape[1]

    def stat(hh, row):
        return stat_ref[hh, row:row + 1, :]

    def stage_a(hh, group):
        alpha_new = [stat(hh, row_alpha_new + u) for u in range(unroll)]
        m = stat(hh, 0)
        for u, (qa, ka) in enumerate(group):
            static = isinstance(ka, int)
            krows = pl.ds(ka * t, t) if static else pl.ds(pl.multiple_of(ka * t, t), t)
            diag = int(ka == qa) if static else (ka == qa).astype(jnp.int32)
            s = jnp.dot(kp_ref[0, krows, hh * LANES:(hh + 1) * LANES],
                        qpt_ref[0, qa, hh * LANES:(hh + 1) * LANES, :],
                        preferred_element_type=F32) + mask_ref[diag]
            s_ref[u * ATTN_HEADS + hh] = s
            m_prev = jnp.where(ka == 0, -jnp.inf, m)
            m = jnp.maximum(m_prev, jnp.max(s, axis=0, keepdims=True))
            stat_ref[hh, row_m + u:row_m + u + 1, :] = m
            stat_ref[hh, row_alpha_new + u:row_alpha_new + u + 1, :] = jnp.exp2(m_prev - m)
            stat_ref[hh, row_alpha_old + u:row_alpha_old + u + 1, :] = alpha_new[u]
        stat_ref[hh, 0:1, :] = m

    def stage_b(hh):
        for u in range(unroll):
            buf = u * ATTN_HEADS + hh
            p_ref[buf] = jnp.exp2(s_ref[buf] - stat(hh, row_m + u)).astype(BF16)

    def stage_c(hh, group, alpha_row):
        acc = acc_ref[hh]
        for u, (qc, kc) in enumerate(group):
            vt = fvt_ref[0, kc, hh * V_ROWS:(hh + 1) * V_ROWS, :]
            acc = stat(hh, alpha_row + u) * acc + jnp.dot(vt, p_ref[u * ATTN_HEADS + hh],
                                                          preferred_element_type=F32)
            if isinstance(kc, int):
                if kc != qc:
                    continue
                tile = qc
            else:
                tile = jnp.where(kc == qc, qc, nq)
            o_ref[0, tile, hh * half:(hh + 1) * half, :] = (acc[:half] / acc[half:half + 1]).astype(BF16)
        acc_ref[hh] = acc

    def group(g):
        return steps[g * unroll:(g + 1) * unroll]

    def next_step(q, k):
        wrap = k >= q
        return jnp.where(wrap, q + 1, q), jnp.where(wrap, 0, k + 1)

    def body(j, carry):
        a_first, b_group, c_group = carry
        a_group = []
        nxt = a_first
        for _ in range(unroll):
            a_group.append(nxt)
            nxt = next_step(*nxt)
        for hh in range(ATTN_HEADS):
            stage_c(hh, c_group, row_alpha_old)
            stage_b(hh)
            stage_a(hh, a_group)
        return nxt, tuple(a_group), b_group

    as_i32 = lambda grp: tuple((jnp.int32(q), jnp.int32(k)) for q, k in grp)
    for hh in range(ATTN_HEADS):
        stage_a(hh, group(0))
    for hh in range(ATTN_HEADS):
        stage_b(hh)
        stage_a(hh, group(1))
    lax.fori_loop(2, n_groups, body, (as_i32(group(2))[0], as_i32(group(1)), as_i32(group(0))))
    for hh in range(ATTN_HEADS):
        stage_c(hh, group(n_groups - 2), row_alpha_old)
        stage_b(hh)
    for hh in range(ATTN_HEADS):
        stage_c(hh, group(n_groups - 1), row_alpha_new)


def _fox(qpt, kp, fvt):
    b, s, _ = kp.shape
    t = ATTN_TILE
    nt = s // t
    g = ATTN_HEADS
    return pl.pallas_call(
        _fox_kernel,
        grid=(b, FOX_HEADS // g),
        in_specs=[pl.BlockSpec((1, nt, g * LANES, t), lambda b, p: (b, 0, p, 0)),
                  pl.BlockSpec((1, s, g * LANES), lambda b, p: (b, 0, p)),
                  pl.BlockSpec((1, nt, g * V_ROWS, t), lambda b, p: (b, 0, p, 0))],
        out_specs=pl.BlockSpec((1, nt + 1, g * FOX_HEAD_DIM, t), lambda b, p: (b, 0, p, 0)),
        out_shape=jax.ShapeDtypeStruct((b, nt + 1, D_FOX, t), BF16),
        scratch_shapes=[pltpu.VMEM((ATTN_UNROLL * g, t, t), F32), pltpu.VMEM((ATTN_UNROLL * g, t, t), BF16),
                        pltpu.VMEM((g, V_ROWS, t), F32), pltpu.VMEM((2, t, t), F32),
                        pltpu.VMEM((g, 8 * pl.cdiv(1 + 3 * ATTN_UNROLL, 8), t), F32)],
        compiler_params=pltpu.CompilerParams(dimension_semantics=("arbitrary", "arbitrary"),
                                             vmem_limit_bytes=VMEM_LIMIT),
        name="fox",
    )(qpt, kp, fvt)


def _aug_placement():
    pk = np.zeros((LANES, LANES), np.float32)
    pq = np.zeros((LANES, LANES), np.float32)
    for head in range(FOX_HEADS):
        _, base = _aug_lanes(head)
        for j in range(3):
            pk[24, base + j] = 1.0
            pk[8 * j + head, base + 3 + j] = -1.0
            pq[base + j, 8 * j + head] = 1.0
            pq[base + 3 + j, 24] = 1.0
    return jnp.asarray(pk, BF16), jnp.asarray(pq, BF16)


def _rotary_tables(seq):
    half = RET_HEAD_DIM // 2
    inv_freq = ROPE_BASE ** (-np.arange(half, dtype=np.float64) / half)
    ang = np.arange(seq, dtype=np.float64)[:, None] * inv_freq[None, :]
    cos, sin = np.cos(ang), np.sin(ang)
    return (jnp.asarray(np.concatenate([cos, cos], axis=-1), F32),
            jnp.asarray(np.concatenate([-sin, sin], axis=-1), F32))


def _retention_tables():
    c = RET_CHUNK
    log_gamma = np.log1p(-np.power(2.0, -5.0 - np.arange(RET_HEADS, dtype=np.float64)))
    idx = np.arange(c, dtype=np.float64)
    diff = idx[:, None] - idx[None, :]
    intra = np.where(diff >= 0, np.exp(log_gamma[:, None, None] * np.maximum(diff, 0.0)), 0.0)
    qdec = np.exp(log_gamma[:, None] * (idx + 1.0))[..., None]
    kdec = np.exp(log_gamma[:, None] * (c - 1.0 - idx))[..., None]
    cdec = np.exp(log_gamma * c)[:, None, None]
    bc = lambda a: jnp.asarray(np.broadcast_to(a, a.shape[:2] + (LANES,)), F32)
    return jnp.asarray(intra, F32), bc(qdec), bc(kdec), bc(cdec)


def kernel(x, c, w_ada, b_ada, ffn1_w_gate, ffn1_w_up, ffn1_w_down, ln1_g, ln1_b, w_in, fox_b_f, ret_gn_g,
           ret_gn_b, w_o, ln2_g, ln2_b, ffn2_w_gate, ffn2_w_up, ffn2_w_down, ln3_g, ln3_b):
    batch, seq, d = x.shape
    assert d == D_MODEL and seq % FFN_TILE == 0 and FFN_TILE % TOKEN_TILE == 0 and w_ada.shape[0] == DEPTH
    layer = 0

    c_pad = jnp.pad(c, ((0, 8 - batch), (0, 0)))
    mod3 = _ada(c_pad, w_ada[layer], b_ada[layer][None, :])[:batch].reshape(batch, N_MOD, d)

    x1 = _ffn(x, mod3, ffn1_w_gate[layer].astype(BF16), ffn1_w_up[layer].astype(BF16),
              ffn1_w_down[layer].astype(BF16), ln1_g[layer][None, :], ln1_b[layer][None, :], 0)

    w = w_in[layer]
    o_fl = 3 * D_FOX
    o_r = o_fl + FOX_HEADS
    q_scale = LOG2E * FOX_HEAD_DIM ** -0.5
    wqk = jnp.concatenate([w[:, :D_FOX] * q_scale, w[:, D_FOX:2 * D_FOX]], axis=1).astype(BF16)
    wv = w[:, 2 * D_FOX:3 * D_FOX].astype(BF16)
    wfl = jnp.pad(w[:, o_fl:o_r], ((0, 0), (0, LANES - FOX_HEADS))).astype(BF16)
    bfl = jnp.pad(fox_b_f[layer], (0, LANES - FOX_HEADS))[None, :]
    wr = w[:, o_r:o_r + 4 * D_RET].astype(BF16)
    triu = jnp.asarray(np.triu(np.ones((TOKEN_TILE, TOKEN_TILE), np.float32)), BF16)
    paugk, paugqt = _aug_placement()
    cos2, sin2 = _rotary_tables(seq)
    intra, qdec, kdec, cdec = _retention_tables()
    qpt, kp, fvt, ret = _inproj(x1, mod3, wqk, wv, wfl, bfl, wr, triu, paugk, paugqt, cos2, sin2,
                               intra, qdec, kdec, cdec, ret_gn_g[layer][None, :], ret_gn_b[layer][None, :])

    foxt = _fox(qpt, kp, fvt)

    return _mix_ffn(foxt, ret, x1, mod3, w_o[layer].astype(BF16), ln2_g[layer][None, :], ln2_b[layer][None, :],
                    ffn2_w_gate[layer].astype(BF16), ffn2_w_up[layer].astype(BF16),
                    ffn2_w_down[layer].astype(BF16), ln3_g[layer][None, :], ln3_b[layer][None, :])
```

```python
import functools
import math

import numpy as np
import jax
import jax.numpy as jnp
from jax import lax
from jax.experimental import pallas as pl
from jax.experimental.pallas import tpu as pltpu

F32 = jnp.float32
BF16 = jnp.bfloat16

D_MODEL = 1024
D_FF = 2816
D_FOX = 512
D_RET = 512
FOX_HEADS = 8
FOX_HEAD_DIM = 64
RET_HEADS = 4
RET_HEAD_DIM = 128
N_MOD = 9
ROPE_BASE = 10000.0
LN_EPS = 1e-5
DEPTH = 1
DEEPNORM_ALPHA = (2.0 * DEPTH) ** 0.25
FFN_RES_WEIGHT = 0.5
LOG2E = math.log2(math.e)

LANES = 128
BF16_ROWS = 16
TOKEN_TILE = 512
FFN_TILE = 1024
FFN_SUB_TILE = 256
FF_CHUNK = 256
RET_CHUNK = 256
ATTN_TILE = TOKEN_TILE
ATTN_HEADS = 2
ATTN_UNROLL = 2
AUG_COLS = 6
V_ROWS = FOX_HEAD_DIM + BF16_ROWS
VMEM_LIMIT = 56 * 2 ** 20

NT_DIMS = (((1,), (1,)), ((), ()))
TN_DIMS = (((0,), (0,)), ((), ()))


def _silu(v):
    return v / (1.0 + jnp.exp(-v))


def _layer_norm(y, g, b):
    mu = jnp.mean(y, axis=-1, keepdims=True)
    d = y - mu
    var = jnp.mean(d * d, axis=-1, keepdims=True)
    return d * lax.rsqrt(var + LN_EPS) * g + b


def _split3(v):
    hi = v.astype(BF16)
    r = v - hi.astype(F32)
    mid = r.astype(BF16)
    lo = (r - mid.astype(F32)).astype(BF16)
    return hi, mid, lo


def _stack_terms(v, tail, rows):
    hi, mid, lo = _split3(v)
    parts = [hi.astype(F32), mid.astype(F32), lo.astype(F32), tail]
    if rows > 32:
        parts.append(jnp.zeros((rows - 32, v.shape[1]), F32))
    return jnp.concatenate(parts, axis=0).astype(BF16)


def _const_spec(shape):
    zeros = (0,) * len(shape)
    return pl.BlockSpec(shape, lambda *_: zeros, pipeline_mode=pl.Buffered(1))


def _aug_lanes(head):
    half = FOX_HEAD_DIM
    pair, odd = divmod(head, 2)
    data_lo = half if odd else 0
    aug_lo = (0 if odd else half) + AUG_COLS * pair
    return data_lo, aug_lo


def _ada_kernel(c_ref, w_ref, b_ref, o_ref):
    ca = _silu(c_ref[...]).astype(BF16)
    o_ref[...] = jnp.dot(ca, w_ref[...].astype(BF16), preferred_element_type=F32) + b_ref[...]


def _ada(c_pad, w, b):
    rows, d = c_pad.shape
    n = w.shape[1]
    tn = 1024
    return pl.pallas_call(
        _ada_kernel,
        grid=(n // tn,),
        in_specs=[pl.BlockSpec((rows, d), lambda j: (0, 0)),
                  pl.BlockSpec((d, tn), lambda j: (0, j)),
                  pl.BlockSpec((1, tn), lambda j: (0, j))],
        out_specs=pl.BlockSpec((rows, tn), lambda j: (0, j)),
        out_shape=jax.ShapeDtypeStruct((rows, n), F32),
        compiler_params=pltpu.CompilerParams(dimension_semantics=("arbitrary",)),
        name="ada",
    )(c_pad, w, b)


def _swiglu_ln_tile(load_x, store_out, mod_ref, mod_base, wg_ref, wu_ref, wd_ref, lng_ref, lnb_ref, act_ref):
    n_sub = FFN_TILE // FFN_SUB_TILE
    n_chunks = D_FF // FF_CHUNK
    lead_finish, lead_next = 2, 6
    sh = mod_ref[0, mod_base:mod_base + 1, :]
    sc = mod_ref[0, mod_base + 1:mod_base + 2, :]
    g = mod_ref[0, mod_base + 2:mod_base + 3, :]
    xs, hs = {}, {}

    def rows_of(i):
        return slice(i * FFN_SUB_TILE, (i + 1) * FFN_SUB_TILE)

    def begin(i):
        xs[i] = load_x(i)
        hs[i] = (xs[i] * (1.0 + sc) + sh).astype(BF16)

    def chunks(i, first, last):
        for j in range(first, last):
            cols = slice(j * FF_CHUNK, (j + 1) * FF_CHUNK)
            gate = jnp.dot(hs[i], wg_ref[:, cols], preferred_element_type=F32)
            up = jnp.dot(hs[i], wu_ref[:, cols], preferred_element_type=F32)
            act_ref[rows_of(i), cols] = (_silu(gate) * up).astype(BF16)

    def finish(i):
        f = jnp.dot(act_ref[rows_of(i), :], wd_ref[...], preferred_element_type=F32)
        y = DEEPNORM_ALPHA * xs.pop(i) + (FFN_RES_WEIGHT * g) * f
        store_out(i, _layer_norm(y, lng_ref[...], lnb_ref[...]))

    begin(0)
    for i in range(n_sub):
        chunks(i, 0, lead_finish)
        if i > 0:
            finish(i - 1)
        chunks(i, lead_finish, lead_next)
        if i + 1 < n_sub:
            begin(i + 1)
        chunks(i, lead_next, n_chunks)
    finish(n_sub - 1)


def _ffn_kernel(x_ref, mod_ref, wg_ref, wu_ref, wd_ref, lng_ref, lnb_ref, wat_ref, wrt_ref,
                o_ref, wa_ref, wr_ref, act_ref, *, mod_base, q_blocks, q_scale):
    step = pl.program_id(0) * pl.num_programs(1) + pl.program_id(1)
    scale = jnp.where(step < q_blocks, q_scale, 1.0)
    wa_ref[...] = (wat_ref[...] * scale).T.astype(BF16)
    wr_ref[...] = wrt_ref[...].T.astype(BF16)

    def rows_of(i):
        return slice(i * FFN_SUB_TILE, (i + 1) * FFN_SUB_TILE)

    def load_x(i):
        return x_ref[0, rows_of(i)]

    def store_out(i, value):
        o_ref[0, rows_of(i)] = value

    _swiglu_ln_tile(load_x, store_out, mod_ref, mod_base, wg_ref, wu_ref, wd_ref, lng_ref, lnb_ref, act_ref)


def _mix_ffn_kernel(foxt_ref, ret_ref, x_ref, mod_ref, wo_ref, ln2g_ref, ln2b_ref,
                    wg_ref, wu_ref, wd_ref, lng_ref, lnb_ref, o_ref, act_ref):
    g2 = mod_ref[0, 5:6, :]

    def rows_of(i):
        return slice(i * FFN_SUB_TILE, (i + 1) * FFN_SUB_TILE)

    def load_x(i):
        tile, part = divmod(i * FFN_SUB_TILE, ATTN_TILE)
        fox_t = foxt_ref[0, tile, :, part:part + FFN_SUB_TILE]
        m = (lax.dot_general(fox_t, wo_ref[0:D_FOX, :], TN_DIMS, preferred_element_type=F32)
             + jnp.dot(ret_ref[0, rows_of(i)], wo_ref[D_FOX:D_FOX + D_RET, :], preferred_element_type=F32))
        return _layer_norm(DEEPNORM_ALPHA * x_ref[0, rows_of(i)] + g2 * m, ln2g_ref[...], ln2b_ref[...])

    def store_out(i, value):
        o_ref[0, rows_of(i)] = value

    _swiglu_ln_tile(load_x, store_out, mod_ref, 6, wg_ref, wu_ref, wd_ref, lng_ref, lnb_ref, act_ref)


def _tile_spec(tile, width):
    return pl.BlockSpec((1, tile, width), lambda b, t: (b, t, 0))


def _ffn_weight_specs():
    return [_const_spec((D_MODEL, D_FF)), _const_spec((D_MODEL, D_FF)), _const_spec((D_FF, D_MODEL)),
            _const_spec((1, D_MODEL)), _const_spec((1, D_MODEL))]


def _ffn(x, mod3, wg, wu, wd, lng, lnb, mod_base, wat, wrt, q_scale):
    b, s, d = x.shape
    tiles = s // FFN_TILE
    steps = b * tiles
    na, nr = wat.shape[0] // LANES, wrt.shape[0] // LANES
    assert wat.shape[0] % LANES == 0 and wrt.shape[0] % LANES == 0 and max(na, nr) <= steps

    def slab(n_slabs, transposed):
        if transposed:
            return pl.BlockSpec((d, LANES), lambda b, t: (0, jnp.minimum(b * tiles + t, n_slabs - 1)))
        return pl.BlockSpec((LANES, d), lambda b, t: (jnp.minimum(b * tiles + t, n_slabs - 1), 0))

    return pl.pallas_call(
        functools.partial(_ffn_kernel, mod_base=mod_base, q_blocks=D_FOX // LANES, q_scale=q_scale),
        grid=(b, tiles),
        in_specs=[_tile_spec(FFN_TILE, d), pl.BlockSpec((1, N_MOD, d), lambda b, t: (b, 0, 0))]
        + _ffn_weight_specs() + [slab(na, False), slab(nr, False)],
        out_specs=[_tile_spec(FFN_TILE, d), slab(na, True), slab(nr, True)],
        out_shape=[jax.ShapeDtypeStruct((b, s, d), F32), jax.ShapeDtypeStruct((d, wat.shape[0]), BF16),
                   jax.ShapeDtypeStruct((d, wrt.shape[0]), BF16)],
        scratch_shapes=[pltpu.VMEM((FFN_TILE, D_FF), BF16)],
        compiler_params=pltpu.CompilerParams(dimension_semantics=("arbitrary", "arbitrary"),
                                             vmem_limit_bytes=VMEM_LIMIT),
        name="ffn1",
    )(x, mod3, wg, wu, wd, lng, lnb, wat, wrt)


def _mix_ffn(foxt, ret, x1, mod3, wo, ln2g, ln2b, wg, wu, wd, lng, lnb):
    b, s, d = x1.shape
    per = FFN_TILE // ATTN_TILE
    return pl.pallas_call(
        _mix_ffn_kernel,
        grid=(b, s // FFN_TILE),
        in_specs=[pl.BlockSpec((1, per, D_FOX, ATTN_TILE), lambda b, t: (b, t, 0, 0)),
                  _tile_spec(FFN_TILE, D_RET), _tile_spec(FFN_TILE, d),
                  pl.BlockSpec((1, N_MOD, d), lambda b, t: (b, 0, 0)),
                  _const_spec((D_FOX + D_RET, d)), _const_spec((1, d)), _const_spec((1, d))] + _ffn_weight_specs(),
        out_specs=_tile_spec(FFN_TILE, d),
        out_shape=jax.ShapeDtypeStruct((b, s, d), F32),
        scratch_shapes=[pltpu.VMEM((FFN_TILE, D_FF), BF16)],
        compiler_params=pltpu.CompilerParams(dimension_semantics=("arbitrary", "arbitrary"),
                                             vmem_limit_bytes=VMEM_LIMIT),
        name="ffn2",
    )(foxt, ret, x1, mod3, wo, ln2g, ln2b, wg, wu, wd, lng, lnb)


def _inproj_kernel(x_ref, mod_ref, wa_ref, wfl_ref, bfl_ref, wr_ref, triu_ref, paugk_ref, paugqt_ref,
                   cos_ref, sin_ref, intra_ref, qdec_ref, kdec_ref, cdec_ref, gng_ref, gnb_ref,
                   qpt_ref, kp_ref, fvt_ref, ret_ref, carry_ref, state_ref):
    tm = TOKEN_TILE
    half = FOX_HEAD_DIM

    @pl.when(pl.program_id(1) == 0)
    def _():
        carry_ref[...] = jnp.zeros_like(carry_ref)
        state_ref[...] = jnp.zeros_like(state_ref)

    sh = mod_ref[0, 3:4, :]
    sc = mod_ref[0, 4:5, :]
    h = (x_ref[0] * (1.0 + sc) + sh).astype(BF16)

    r_all = jnp.dot(h, wr_ref[...], preferred_element_type=F32)
    cos2 = cos_ref[...]
    sin2 = sin_ref[...]
    c = RET_CHUNK
    for hd in range(RET_HEADS):
        lo_, hi_ = hd * RET_HEAD_DIM, (hd + 1) * RET_HEAD_DIM
        tq = r_all[:, lo_:hi_]
        tk = r_all[:, D_RET + lo_:D_RET + hi_]
        q_r = tq * cos2 + pltpu.roll(tq, RET_HEAD_DIM // 2, 1) * sin2
        k_r = (tk * cos2 + pltpu.roll(tk, RET_HEAD_DIM // 2, 1) * sin2) * (RET_HEAD_DIM ** -0.5)
        v = r_all[:, 2 * D_RET + lo_:2 * D_RET + hi_]
        gate = r_all[:, 3 * D_RET + lo_:3 * D_RET + hi_]
        state = state_ref[hd]
        for ci in range(tm // c):
            rows = slice(ci * c, (ci + 1) * c)
            qc = q_r[rows].astype(BF16)
            kc = k_r[rows]
            vc = v[rows].astype(BF16)
            s = lax.dot_general(qc, kc.astype(BF16), NT_DIMS, preferred_element_type=F32) * intra_ref[hd]
            o = (jnp.dot(s.astype(BF16), vc, preferred_element_type=F32)
                 + qdec_ref[hd] * jnp.dot(qc, state.astype(BF16), preferred_element_type=F32))
            state = state * cdec_ref[hd] + lax.dot_general(
                (kc * kdec_ref[hd]).astype(BF16), vc, TN_DIMS, preferred_element_type=F32)
            mu = jnp.mean(o, axis=-1, keepdims=True)
            d = o - mu
            var = jnp.mean(d * d, axis=-1, keepdims=True)
            yn = d * lax.rsqrt(var + LN_EPS) * gng_ref[:, lo_:hi_] + gnb_ref[:, lo_:hi_]
            ret_ref[0, rows, lo_:hi_] = (_silu(gate[rows]) * yn).astype(BF16)
        state_ref[hd] = state

    z = jnp.dot(h, wfl_ref[...], preferred_element_type=F32) + bfl_ref[...]
    logf = jnp.minimum(z, 0.0) - jnp.log1p(jnp.exp(-jnp.abs(z)))
    logf_t = logf.T[0:FOX_HEADS, :]
    terms = _stack_terms(logf_t, jnp.zeros((FOX_HEADS, tm), F32), 4 * FOX_HEADS)
    sums = jnp.dot(terms, triu_ref[...], preferred_element_type=F32)
    cum_t = (sums[0:8] + sums[8:16] + sums[16:24]) + carry_ref[:, 0:1]
    carry_ref[...] = jnp.broadcast_to(cum_t[:, tm - 1:tm], carry_ref.shape)

    row8 = lax.broadcasted_iota(jnp.int32, (FOX_HEADS, tm), 0)
    packed_t = _stack_terms(cum_t * LOG2E, jnp.where(row8 == 0, 1.0, 0.0), LANES)
    aug_k = lax.dot_general(packed_t, paugk_ref[...], TN_DIMS, preferred_element_type=F32)
    aug_qt = jnp.dot(paugqt_ref[...], packed_t, preferred_element_type=F32)

    qk = jnp.dot(h, wa_ref[:, 0:2 * D_FOX], preferred_element_type=F32)
    lane = lax.broadcasted_iota(jnp.int32, (tm, LANES), 1)
    row = lax.broadcasted_iota(jnp.int32, (LANES, tm), 0)
    q_t = [qk[:, pr * LANES:(pr + 1) * LANES].T for pr in range(FOX_HEADS // 2)]
    for hd in range(FOX_HEADS):
        data_lo, aug_lo = _aug_lanes(hd)
        src = (hd // 2) * LANES
        out = slice(hd * LANES, (hd + 1) * LANES)
        is_data = (lane >= data_lo) & (lane < data_lo + half)
        is_aug = (lane >= aug_lo) & (lane < aug_lo + AUG_COLS)
        kp_ref[0, :, out] = jnp.where(is_data, qk[:, D_FOX + src:D_FOX + src + LANES],
                                      jnp.where(is_aug, aug_k, 0.0)).astype(BF16)
        is_data_t = (row >= data_lo) & (row < data_lo + half)
        is_aug_t = (row >= aug_lo) & (row < aug_lo + AUG_COLS)
        qpt_ref[0, 0, out, :] = jnp.where(is_data_t, q_t[hd // 2],
                                          jnp.where(is_aug_t, aug_qt, 0.0)).astype(BF16)

    vt = jnp.dot(h, wa_ref[:, 2 * D_FOX:3 * D_FOX], preferred_element_type=F32).T.astype(BF16)
    for hd in range(FOX_HEADS):
        fvt_ref[0, 0, hd * V_ROWS:hd * V_ROWS + half, :] = vt[hd * half:(hd + 1) * half]
        fvt_ref[0, 0, hd * V_ROWS + half:(hd + 1) * V_ROWS, :] = jnp.ones((BF16_ROWS, tm), BF16)


def _inproj(x1, mod3, wa, wfl, bfl, wr, triu, paugk, paugqt, cos2, sin2, intra, qdec, kdec, cdec, gng, gnb):
    b, s, d = x1.shape
    tm = TOKEN_TILE
    nt = s // tm
    c = RET_CHUNK
    in_specs = [
        _tile_spec(tm, d), pl.BlockSpec((1, N_MOD, d), lambda b, t: (b, 0, 0)),
        _const_spec((d, 3 * D_FOX)), _const_spec((d, LANES)), _const_spec((1, LANES)),
        _const_spec((d, 4 * D_RET)), _const_spec((tm, tm)), _const_spec((LANES, LANES)), _const_spec((LANES, LANES)),
        pl.BlockSpec((tm, LANES), lambda b, t: (t, 0)), pl.BlockSpec((tm, LANES), lambda b, t: (t, 0)),
        _const_spec((RET_HEADS, c, c)), _const_spec((RET_HEADS, c, LANES)), _const_spec((RET_HEADS, c, LANES)),
        _const_spec((RET_HEADS, 1, LANES)), _const_spec((1, D_RET)), _const_spec((1, D_RET)),
    ]
    out_specs = [
        pl.BlockSpec((1, 1, FOX_HEADS * LANES, tm), lambda b, t: (b, t, 0, 0)), _tile_spec(tm, FOX_HEADS * LANES),
        pl.BlockSpec((1, 1, FOX_HEADS * V_ROWS, tm), lambda b, t: (b, t, 0, 0)),
        _tile_spec(tm, D_RET),
    ]
    out_shape = [
        jax.ShapeDtypeStruct((b, nt, FOX_HEADS * LANES, tm), BF16), jax.ShapeDtypeStruct((b, s, FOX_HEADS * LANES), BF16),
        jax.ShapeDtypeStruct((b, nt, FOX_HEADS * V_ROWS, tm), BF16),
        jax.ShapeDtypeStruct((b, s, D_RET), BF16),
    ]
    return pl.pallas_call(
        _inproj_kernel,
        grid=(b, nt),
        in_specs=in_specs, out_specs=out_specs, out_shape=out_shape,
        scratch_shapes=[pltpu.VMEM((FOX_HEADS, LANES), F32),
                        pltpu.VMEM((RET_HEADS, RET_HEAD_DIM, RET_HEAD_DIM), F32)],
        compiler_params=pltpu.CompilerParams(dimension_semantics=("arbitrary", "arbitrary"),
                                             vmem_limit_bytes=VMEM_LIMIT),
        name="inproj",
    )(x1, mod3, wa, wfl, bfl, wr, triu, paugk, paugqt, cos2, sin2, intra, qdec, kdec, cdec, gng, gnb)


def _fox_kernel(qpt_ref, kp_ref, fvt_ref, *rest):
    n_cast = (len(rest) - 6) // 2
    cast_in, o_ref, cast_out = rest[:n_cast], rest[n_cast], rest[n_cast + 1:2 * n_cast + 1]
    s_ref, p_ref, acc_ref, mask_ref, stat_ref = rest[2 * n_cast + 1:]
    for src, dst in zip(cast_in, cast_out):
        dst[...] = src[...].astype(BF16)
    _fox_attention(qpt_ref, kp_ref, fvt_ref, o_ref, s_ref, p_ref, acc_ref, mask_ref, stat_ref)


def _fox_attention(qpt_ref, kp_ref, fvt_ref, o_ref, s_ref, p_ref, acc_ref, mask_ref, stat_ref):
    t = ATTN_TILE
    nq = qpt_ref.shape[1]
    half = FOX_HEAD_DIM

    k_pos = lax.broadcasted_iota(jnp.int32, (t, t), 0)
    q_pos = lax.broadcasted_iota(jnp.int32, (t, t), 1)
    mask_ref[0] = jnp.zeros((t, t), F32)
    mask_ref[1] = jnp.where(k_pos <= q_pos, 0.0, -jnp.inf)
    acc_ref[...] = jnp.zeros_like(acc_ref)
    stat_ref[...] = jnp.zeros_like(stat_ref)

    unroll = ATTN_UNROLL
    row_m, row_alpha_new, row_alpha_old = 1, 1 + unroll, 1 + 2 * unroll
    steps = [(q, k) for q in range(nq) for k in range(q + 1)]
    n_groups = len(steps) // unroll
    assert len(steps) % unroll == 0 and n_groups >= 4 and 1 + 3 * unroll <= stat_ref.shape[1]

    def stat(hh, row):
        return stat_ref[hh, row:row + 1, :]

    def stage_a(hh, group):
        alpha_new = [stat(hh, row_alpha_new + u) for u in range(unroll)]
        m = stat(hh, 0)
        for u, (qa, ka) in enumerate(group):
            static = isinstance(ka, int)
            krows = pl.ds(ka * t, t) if static else pl.ds(pl.multiple_of(ka * t, t), t)
            diag = int(ka == qa) if static else (ka == qa).astype(jnp.int32)
            s = jnp.dot(kp_ref[0, krows, hh * LANES:(hh + 1) * LANES],
                        qpt_ref[0, qa, hh * LANES:(hh + 1) * LANES, :],
                        preferred_element_type=F32) + mask_ref[diag]
            s_ref[u * ATTN_HEADS + hh] = s
            m_prev = jnp.where(ka == 0, -jnp.inf, m)
            m = jnp.maximum(m_prev, jnp.max(s, axis=0, keepdims=True))
            stat_ref[hh, row_m + u:row_m + u + 1, :] = m
            stat_ref[hh, row_alpha_new + u:row_alpha_new + u + 1, :] = jnp.exp2(m_prev - m)
            stat_ref[hh, row_alpha_old + u:row_alpha_old + u + 1, :] = alpha_new[u]
        stat_ref[hh, 0:1, :] = m

    def stage_b(hh):
        for u in range(unroll):
            buf = u * ATTN_HEADS + hh
            p_ref[buf] = jnp.exp2(s_ref[buf] - stat(hh, row_m + u)).astype(BF16)

    def stage_c(hh, group, alpha_row):
        acc = acc_ref[hh]
        for u, (qc, kc) in enumerate(group):
            vt = fvt_ref[0, kc, hh * V_ROWS:(hh + 1) * V_ROWS, :]
            acc = stat(hh, alpha_row + u) * acc + jnp.dot(vt, p_ref[u * ATTN_HEADS + hh],
                                                          preferred_element_type=F32)
            if isinstance(kc, int):
                if kc != qc:
                    continue
                tile = qc
            else:
                tile = jnp.where(kc == qc, qc, nq)
            o_ref[0, tile, hh * half:(hh + 1) * half, :] = (acc[:half] / acc[half:half + 1]).astype(BF16)
        acc_ref[hh] = acc

    def group(g):
        return steps[g * unroll:(g + 1) * unroll]

    def next_step(q, k):
        wrap = k >= q
        return jnp.where(wrap, q + 1, q), jnp.where(wrap, 0, k + 1)

    def body(j, carry):
        a_first, b_group, c_group = carry
        a_group = []
        nxt = a_first
        for _ in range(unroll):
            a_group.append(nxt)
            nxt = next_step(*nxt)
        for hh in range(ATTN_HEADS):
            stage_c(hh, c_group, row_alpha_old)
            stage_b(hh)
            stage_a(hh, a_group)
        return nxt, tuple(a_group), b_group

    as_i32 = lambda grp: tuple((jnp.int32(q), jnp.int32(k)) for q, k in grp)
    for hh in range(ATTN_HEADS):
        stage_a(hh, group(0))
    for hh in range(ATTN_HEADS):
        stage_b(hh)
        stage_a(hh, group(1))
    lax.fori_loop(2, n_groups, body, (as_i32(group(2))[0], as_i32(group(1)), as_i32(group(0))))
    for hh in range(ATTN_HEADS):
        stage_c(hh, group(n_groups - 2), row_alpha_old)
        stage_b(hh)
    for hh in range(ATTN_HEADS):
        stage_c(hh, group(n_groups - 1), row_alpha_new)


def _fox(qpt, kp, fvt, f32_weights):
    b, s, _ = kp.shape
    t = ATTN_TILE
    nt = s // t
    g = ATTN_HEADS
    groups = FOX_HEADS // g
    steps = b * groups
    slab_specs = []
    for w in f32_weights:
        assert w.shape[0] % (steps * BF16_ROWS) == 0
        slab_specs.append(pl.BlockSpec((w.shape[0] // steps, w.shape[1]), lambda b, p: (b * groups + p, 0)))
    return pl.pallas_call(
        _fox_kernel,
        grid=(b, groups),
        in_specs=[pl.BlockSpec((1, nt, g * LANES, t), lambda b, p: (b, 0, p, 0)),
                  pl.BlockSpec((1, s, g * LANES), lambda b, p: (b, 0, p)),
                  pl.BlockSpec((1, nt, g * V_ROWS, t), lambda b, p: (b, 0, p, 0))] + slab_specs,
        out_specs=[pl.BlockSpec((1, nt + 1, g * FOX_HEAD_DIM, t), lambda b, p: (b, 0, p, 0))] + slab_specs,
        out_shape=[jax.ShapeDtypeStruct((b, nt + 1, D_FOX, t), BF16)]
        + [jax.ShapeDtypeStruct(w.shape, BF16) for w in f32_weights],
        scratch_shapes=[pltpu.VMEM((ATTN_UNROLL * g, t, t), F32), pltpu.VMEM((ATTN_UNROLL * g, t, t), BF16),
                        pltpu.VMEM((g, V_ROWS, t), F32), pltpu.VMEM((2, t, t), F32),
                        pltpu.VMEM((g, 8 * pl.cdiv(1 + 3 * ATTN_UNROLL, 8), t), F32)],
        compiler_params=pltpu.CompilerParams(dimension_semantics=("arbitrary", "arbitrary"),
                                             vmem_limit_bytes=VMEM_LIMIT),
        name="fox",
    )(qpt, kp, fvt, *f32_weights)


def _aug_placement():
    pk = np.zeros((LANES, LANES), np.float32)
    pq = np.zeros((LANES, LANES), np.float32)
    for head in range(FOX_HEADS):
        _, base = _aug_lanes(head)
        for j in range(3):
            pk[24, base + j] = 1.0
            pk[8 * j + head, base + 3 + j] = -1.0
            pq[base + j, 8 * j + head] = 1.0
            pq[base + 3 + j, 24] = 1.0
    return jnp.asarray(pk, BF16), jnp.asarray(pq, BF16)


def _rotary_tables(seq):
    half = RET_HEAD_DIM // 2
    inv_freq = ROPE_BASE ** (-np.arange(half, dtype=np.float64) / half)
    ang = np.arange(seq, dtype=np.float64)[:, None] * inv_freq[None, :]
    cos, sin = np.cos(ang), np.sin(ang)
    return (jnp.asarray(np.concatenate([cos, cos], axis=-1), F32),
            jnp.asarray(np.concatenate([-sin, sin], axis=-1), F32))


def _retention_tables():
    c = RET_CHUNK
    log_gamma = np.log1p(-np.power(2.0, -5.0 - np.arange(RET_HEADS, dtype=np.float64)))
    idx = np.arange(c, dtype=np.float64)
    diff = idx[:, None] - idx[None, :]
    intra = np.where(diff >= 0, np.exp(log_gamma[:, None, None] * np.maximum(diff, 0.0)), 0.0)
    qdec = np.exp(log_gamma[:, None] * (idx + 1.0))[..., None]
    kdec = np.exp(log_gamma[:, None] * (c - 1.0 - idx))[..., None]
    cdec = np.exp(log_gamma * c)[:, None, None]
    bc = lambda a: jnp.asarray(np.broadcast_to(a, a.shape[:2] + (LANES,)), F32)
    return jnp.asarray(intra, F32), bc(qdec), bc(kdec), bc(cdec)


def kernel(x, c, w_ada, b_ada, ffn1_w_gate, ffn1_w_up, ffn1_w_down, ln1_g, ln1_b, w_in, fox_b_f, ret_gn_g,
           ret_gn_b, w_o, ln2_g, ln2_b, ffn2_w_gate, ffn2_w_up, ffn2_w_down, ln3_g, ln3_b):
    batch, seq, d = x.shape
    assert d == D_MODEL and seq % FFN_TILE == 0 and FFN_TILE % TOKEN_TILE == 0 and w_ada.shape[0] == DEPTH
    layer = 0

    c_pad = jnp.pad(c, ((0, 8 - batch), (0, 0)))
    mod3 = _ada(c_pad, w_ada[layer], b_ada[layer][None, :])[:batch].reshape(batch, N_MOD, d)

    wt = w_in[layer].T
    o_fl = 3 * D_FOX
    o_r = o_fl + FOX_HEADS
    q_scale = LOG2E * FOX_HEAD_DIM ** -0.5
    x1, wa, wr = _ffn(x, mod3, ffn1_w_gate[layer].astype(BF16), ffn1_w_up[layer].astype(BF16),
                      ffn1_w_down[layer].astype(BF16), ln1_g[layer][None, :], ln1_b[layer][None, :], 0,
                      wt[:o_fl], wt[o_r:o_r + 4 * D_RET], q_scale)

    wfl = jnp.pad(wt[o_fl:o_r].T, ((0, 0), (0, LANES - FOX_HEADS))).astype(BF16)
    bfl = jnp.pad(fox_b_f[layer], (0, LANES - FOX_HEADS))[None, :]
    triu = jnp.asarray(np.triu(np.ones((TOKEN_TILE, TOKEN_TILE), np.float32)), BF16)
    paugk, paugqt = _aug_placement()
    cos2, sin2 = _rotary_tables(seq)
    intra, qdec, kdec, cdec = _retention_tables()
    qpt, kp, fvt, ret = _inproj(x1, mod3, wa, wfl, bfl, wr, triu, paugk, paugqt, cos2, sin2,
                               intra, qdec, kdec, cdec, ret_gn_g[layer][None, :], ret_gn_b[layer][None, :])

    foxt, wo, wg2, wu2, wd2 = _fox(qpt, kp, fvt, [w_o[layer], ffn2_w_gate[layer], ffn2_w_up[layer],
                                                  ffn2_w_down[layer]])

    return _mix_ffn(foxt, ret, x1, mod3, wo, ln2_g[layer][None, :], ln2_b[layer][None, :],
                    wg2, wu2, wd2, ln3_g[layer][None, :], ln3_b[layer][None, :])
```

```python
import functools
import math

import numpy as np
import jax
import jax.numpy as jnp
from jax import lax
from jax.experimental import pallas as pl
from jax.experimental.pallas import tpu as pltpu

F32 = jnp.float32
BF16 = jnp.bfloat16

D_MODEL = 1024
D_FF = 2816
D_FOX = 512
D_RET = 512
FOX_HEADS = 8
FOX_HEAD_DIM = 64
RET_HEADS = 4
RET_HEAD_DIM = 128
N_MOD = 9
ROPE_BASE = 10000.0
LN_EPS = 1e-5
DEPTH = 1
DEEPNORM_ALPHA = (2.0 * DEPTH) ** 0.25
FFN_RES_WEIGHT = 0.5
LOG2E = math.log2(math.e)

LANES = 128
BF16_ROWS = 16
TOKEN_TILE = 512
FFN_TILE = 1024
FFN_SUB_TILE = 256
FF_CHUNK = 256
RET_CHUNK = 256
ATTN_TILE = TOKEN_TILE
ATTN_HEADS = 2
ATTN_UNROLL = 2
AUG_COLS = 6
V_ROWS = FOX_HEAD_DIM + BF16_ROWS
VMEM_LIMIT = 56 * 2 ** 20

NT_DIMS = (((1,), (1,)), ((), ()))
TN_DIMS = (((0,), (0,)), ((), ()))


def _silu(v):
    return v / (1.0 + jnp.exp(-v))


def _layer_norm(y, g, b):
    mu = jnp.mean(y, axis=-1, keepdims=True)
    d = y - mu
    var = jnp.mean(d * d, axis=-1, keepdims=True)
    return d * lax.rsqrt(var + LN_EPS) * g + b


def _split3(v):
    hi = v.astype(BF16)
    r = v - hi.astype(F32)
    mid = r.astype(BF16)
    lo = (r - mid.astype(F32)).astype(BF16)
    return hi, mid, lo


def _stack_terms(v, tail, rows):
    hi, mid, lo = _split3(v)
    parts = [hi.astype(F32), mid.astype(F32), lo.astype(F32), tail]
    if rows > 32:
        parts.append(jnp.zeros((rows - 32, v.shape[1]), F32))
    return jnp.concatenate(parts, axis=0).astype(BF16)


def _const_spec(shape):
    zeros = (0,) * len(shape)
    return pl.BlockSpec(shape, lambda *_: zeros, pipeline_mode=pl.Buffered(1))


def _aug_lanes(head):
    half = FOX_HEAD_DIM
    pair, odd = divmod(head, 2)
    data_lo = half if odd else 0
    aug_lo = (0 if odd else half) + AUG_COLS * pair
    return data_lo, aug_lo


def _ada_kernel(c_ref, w_ref, b_ref, o_ref):
    ca = _silu(c_ref[...]).astype(BF16)
    o_ref[...] = jnp.dot(ca, w_ref[...].astype(BF16), preferred_element_type=F32) + b_ref[...]


def _ada(c_pad, w, b):
    rows, d = c_pad.shape
    n = w.shape[1]
    tn = 1024
    return pl.pallas_call(
        _ada_kernel,
        grid=(n // tn,),
        in_specs=[pl.BlockSpec((rows, d), lambda j: (0, 0)),
                  pl.BlockSpec((d, tn), lambda j: (0, j)),
                  pl.BlockSpec((1, tn), lambda j: (0, j))],
        out_specs=pl.BlockSpec((rows, tn), lambda j: (0, j)),
        out_shape=jax.ShapeDtypeStruct((rows, n), F32),
        compiler_params=pltpu.CompilerParams(dimension_semantics=("arbitrary",)),
        name="ada",
    )(c_pad, w, b)


def _swiglu_ln_tile(load_x, store_out, mod_ref, mod_base, wg_ref, wu_ref, wd_ref, lng_ref, lnb_ref, act_ref):
    n_sub = FFN_TILE // FFN_SUB_TILE
    n_chunks = D_FF // FF_CHUNK
    lead_finish, lead_next = 2, 6
    sh = mod_ref[0, mod_base:mod_base + 1, :]
    sc = mod_ref[0, mod_base + 1:mod_base + 2, :]
    g = mod_ref[0, mod_base + 2:mod_base + 3, :]
    xs, hs = {}, {}

    def rows_of(i):
        return slice(i * FFN_SUB_TILE, (i + 1) * FFN_SUB_TILE)

    def begin(i):
        xs[i] = load_x(i)
        hs[i] = (xs[i] * (1.0 + sc) + sh).astype(BF16)

    def chunks(i, first, last):
        for j in range(first, last):
            cols = slice(j * FF_CHUNK, (j + 1) * FF_CHUNK)
            gate = jnp.dot(hs[i], wg_ref[:, cols], preferred_element_type=F32)
            up = jnp.dot(hs[i], wu_ref[:, cols], preferred_element_type=F32)
            act_ref[rows_of(i), cols] = (_silu(gate) * up).astype(BF16)

    def finish(i):
        f = jnp.dot(act_ref[rows_of(i), :], wd_ref[...], preferred_element_type=F32)
        y = DEEPNORM_ALPHA * xs.pop(i) + (FFN_RES_WEIGHT * g) * f
        store_out(i, _layer_norm(y, lng_ref[...], lnb_ref[...]))

    begin(0)
    for i in range(n_sub):
        chunks(i, 0, lead_finish)
        if i > 0:
            finish(i - 1)
        chunks(i, lead_finish, lead_next)
        if i + 1 < n_sub:
            begin(i + 1)
        chunks(i, lead_next, n_chunks)
    finish(n_sub - 1)


def _ffn_kernel(x_ref, mod_ref, wg_ref, wu_ref, wd_ref, lng_ref, lnb_ref, win_ref,
                o_ref, wa_ref, wr_ref, act_ref, *, mod_base, q_scale):
    o_fl = 3 * D_FOX
    o_r = o_fl + FOX_HEADS
    w = win_ref[...]
    wa_ref[:, 0:D_FOX] = (w[:, 0:D_FOX] * q_scale).astype(BF16)
    wa_ref[:, D_FOX:o_fl] = w[:, D_FOX:o_fl].astype(BF16)
    wr_ref[...] = w[:, o_r:o_r + 4 * D_RET].astype(BF16)

    def rows_of(i):
        return slice(i * FFN_SUB_TILE, (i + 1) * FFN_SUB_TILE)

    def load_x(i):
        return x_ref[0, rows_of(i)]

    def store_out(i, value):
        o_ref[0, rows_of(i)] = value

    _swiglu_ln_tile(load_x, store_out, mod_ref, mod_base, wg_ref, wu_ref, wd_ref, lng_ref, lnb_ref, act_ref)


def _mix_ffn_kernel(foxt_ref, ret_ref, x_ref, mod_ref, wo_ref, ln2g_ref, ln2b_ref,
                    wg_ref, wu_ref, wd_ref, lng_ref, lnb_ref, o_ref, act_ref):
    g2 = mod_ref[0, 5:6, :]

    def rows_of(i):
        return slice(i * FFN_SUB_TILE, (i + 1) * FFN_SUB_TILE)

    def load_x(i):
        tile, part = divmod(i * FFN_SUB_TILE, ATTN_TILE)
        fox_t = foxt_ref[0, tile, :, part:part + FFN_SUB_TILE]
        m = (lax.dot_general(fox_t, wo_ref[0:D_FOX, :], TN_DIMS, preferred_element_type=F32)
             + jnp.dot(ret_ref[0, rows_of(i)], wo_ref[D_FOX:D_FOX + D_RET, :], preferred_element_type=F32))
        return _layer_norm(DEEPNORM_ALPHA * x_ref[0, rows_of(i)] + g2 * m, ln2g_ref[...], ln2b_ref[...])

    def store_out(i, value):
        o_ref[0, rows_of(i)] = value

    _swiglu_ln_tile(load_x, store_out, mod_ref, 6, wg_ref, wu_ref, wd_ref, lng_ref, lnb_ref, act_ref)


def _tile_spec(tile, width):
    return pl.BlockSpec((1, tile, width), lambda b, t: (b, t, 0))


def _ffn_weight_specs():
    return [_const_spec((D_MODEL, D_FF)), _const_spec((D_MODEL, D_FF)), _const_spec((D_FF, D_MODEL)),
            _const_spec((1, D_MODEL)), _const_spec((1, D_MODEL))]


def _ffn(x, mod3, wg, wu, wd, lng, lnb, mod_base, w_in, q_scale):
    b, s, d = x.shape
    tiles = s // FFN_TILE
    steps = b * tiles
    rows = d // steps
    assert d % (steps * BF16_ROWS) == 0

    def slab(width):
        return pl.BlockSpec((rows, width), lambda b, t: (b * tiles + t, 0))

    return pl.pallas_call(
        functools.partial(_ffn_kernel, mod_base=mod_base, q_scale=q_scale),
        grid=(b, tiles),
        in_specs=[_tile_spec(FFN_TILE, d), pl.BlockSpec((1, N_MOD, d), lambda b, t: (b, 0, 0))]
        + _ffn_weight_specs() + [slab(w_in.shape[1])],
        out_specs=[_tile_spec(FFN_TILE, d), slab(3 * D_FOX), slab(4 * D_RET)],
        out_shape=[jax.ShapeDtypeStruct((b, s, d), F32), jax.ShapeDtypeStruct((d, 3 * D_FOX), BF16),
                   jax.ShapeDtypeStruct((d, 4 * D_RET), BF16)],
        scratch_shapes=[pltpu.VMEM((FFN_TILE, D_FF), BF16)],
        compiler_params=pltpu.CompilerParams(dimension_semantics=("arbitrary", "arbitrary"),
                                             vmem_limit_bytes=VMEM_LIMIT),
        name="ffn1",
    )(x, mod3, wg, wu, wd, lng, lnb, w_in)


def _mix_ffn(foxt, ret, x1, mod3, wo, ln2g, ln2b, wg, wu, wd, lng, lnb):
    b, s, d = x1.shape
    per = FFN_TILE // ATTN_TILE
    return pl.pallas_call(
        _mix_ffn_kernel,
        grid=(b, s // FFN_TILE),
        in_specs=[pl.BlockSpec((1, per, D_FOX, ATTN_TILE), lambda b, t: (b, t, 0, 0)),
                  _tile_spec(FFN_TILE, D_RET), _tile_spec(FFN_TILE, d),
                  pl.BlockSpec((1, N_MOD, d), lambda b, t: (b, 0, 0)),
                  _const_spec((D_FOX + D_RET, d)), _const_spec((1, d)), _const_spec((1, d))] + _ffn_weight_specs(),
        out_specs=_tile_spec(FFN_TILE, d),
        out_shape=jax.ShapeDtypeStruct((b, s, d), F32),
        scratch_shapes=[pltpu.VMEM((FFN_TILE, D_FF), BF16)],
        compiler_params=pltpu.CompilerParams(dimension_semantics=("arbitrary", "arbitrary"),
                                             vmem_limit_bytes=VMEM_LIMIT),
        name="ffn2",
    )(foxt, ret, x1, mod3, wo, ln2g, ln2b, wg, wu, wd, lng, lnb)


def _inproj_kernel(x_ref, mod_ref, wa_ref, wfl_ref, bfl_ref, wr_ref, triu_ref, paugk_ref, paugqt_ref,
                   cos_ref, sin_ref, intra_ref, qdec_ref, kdec_ref, cdec_ref, gng_ref, gnb_ref,
                   qpt_ref, kp_ref, fvt_ref, ret_ref, carry_ref, state_ref):
    tm = TOKEN_TILE
    half = FOX_HEAD_DIM

    @pl.when(pl.program_id(1) == 0)
    def _():
        carry_ref[...] = jnp.zeros_like(carry_ref)
        state_ref[...] = jnp.zeros_like(state_ref)

    sh = mod_ref[0, 3:4, :]
    sc = mod_ref[0, 4:5, :]
    h = (x_ref[0] * (1.0 + sc) + sh).astype(BF16)

    r_all = jnp.dot(h, wr_ref[...], preferred_element_type=F32)
    cos2 = cos_ref[...]
    sin2 = sin_ref[...]
    c = RET_CHUNK
    for hd in range(RET_HEADS):
        lo_, hi_ = hd * RET_HEAD_DIM, (hd + 1) * RET_HEAD_DIM
        tq = r_all[:, lo_:hi_]
        tk = r_all[:, D_RET + lo_:D_RET + hi_]
        q_r = tq * cos2 + pltpu.roll(tq, RET_HEAD_DIM // 2, 1) * sin2
        k_r = (tk * cos2 + pltpu.roll(tk, RET_HEAD_DIM // 2, 1) * sin2) * (RET_HEAD_DIM ** -0.5)
        v = r_all[:, 2 * D_RET + lo_:2 * D_RET + hi_]
        gate = r_all[:, 3 * D_RET + lo_:3 * D_RET + hi_]
        state = state_ref[hd]
        for ci in range(tm // c):
            rows = slice(ci * c, (ci + 1) * c)
            qc = q_r[rows].astype(BF16)
            kc = k_r[rows]
            vc = v[rows].astype(BF16)
            s = lax.dot_general(qc, kc.astype(BF16), NT_DIMS, preferred_element_type=F32) * intra_ref[hd]
            o = (jnp.dot(s.astype(BF16), vc, preferred_element_type=F32)
                 + qdec_ref[hd] * jnp.dot(qc, state.astype(BF16), preferred_element_type=F32))
            state = state * cdec_ref[hd] + lax.dot_general(
                (kc * kdec_ref[hd]).astype(BF16), vc, TN_DIMS, preferred_element_type=F32)
            mu = jnp.mean(o, axis=-1, keepdims=True)
            d = o - mu
            var = jnp.mean(d * d, axis=-1, keepdims=True)
            yn = d * lax.rsqrt(var + LN_EPS) * gng_ref[:, lo_:hi_] + gnb_ref[:, lo_:hi_]
            ret_ref[0, rows, lo_:hi_] = (_silu(gate[rows]) * yn).astype(BF16)
        state_ref[hd] = state

    z = jnp.dot(h, wfl_ref[...], preferred_element_type=F32) + bfl_ref[...]
    logf = jnp.minimum(z, 0.0) - jnp.log1p(jnp.exp(-jnp.abs(z)))
    logf_t = logf.T[0:FOX_HEADS, :]
    terms = _stack_terms(logf_t, jnp.zeros((FOX_HEADS, tm), F32), 4 * FOX_HEADS)
    sums = jnp.dot(terms, triu_ref[...], preferred_element_type=F32)
    cum_t = (sums[0:8] + sums[8:16] + sums[16:24]) + carry_ref[:, 0:1]
    carry_ref[...] = jnp.broadcast_to(cum_t[:, tm - 1:tm], carry_ref.shape)

    row8 = lax.broadcasted_iota(jnp.int32, (FOX_HEADS, tm), 0)
    packed_t = _stack_terms(cum_t * LOG2E, jnp.where(row8 == 0, 1.0, 0.0), LANES)
    aug_k = lax.dot_general(packed_t, paugk_ref[...], TN_DIMS, preferred_element_type=F32)
    aug_qt = jnp.dot(paugqt_ref[...], packed_t, preferred_element_type=F32)

    qk = jnp.dot(h, wa_ref[:, 0:2 * D_FOX], preferred_element_type=F32)
    lane = lax.broadcasted_iota(jnp.int32, (tm, LANES), 1)
    row = lax.broadcasted_iota(jnp.int32, (LANES, tm), 0)
    q_t = [qk[:, pr * LANES:(pr + 1) * LANES].T for pr in range(FOX_HEADS // 2)]
    for hd in range(FOX_HEADS):
        data_lo, aug_lo = _aug_lanes(hd)
        src = (hd // 2) * LANES
        out = slice(hd * LANES, (hd + 1) * LANES)
        is_data = (lane >= data_lo) & (lane < data_lo + half)
        is_aug = (lane >= aug_lo) & (lane < aug_lo + AUG_COLS)
        kp_ref[0, :, out] = jnp.where(is_data, qk[:, D_FOX + src:D_FOX + src + LANES],
                                      jnp.where(is_aug, aug_k, 0.0)).astype(BF16)
        is_data_t = (row >= data_lo) & (row < data_lo + half)
        is_aug_t = (row >= aug_lo) & (row < aug_lo + AUG_COLS)
        qpt_ref[0, 0, out, :] = jnp.where(is_data_t, q_t[hd // 2],
                                          jnp.where(is_aug_t, aug_qt, 0.0)).astype(BF16)

    vt = jnp.dot(h, wa_ref[:, 2 * D_FOX:3 * D_FOX], preferred_element_type=F32).T.astype(BF16)
    for hd in range(FOX_HEADS):
        fvt_ref[0, 0, hd * V_ROWS:hd * V_ROWS + half, :] = vt[hd * half:(hd + 1) * half]
        fvt_ref[0, 0, hd * V_ROWS + half:(hd + 1) * V_ROWS, :] = jnp.ones((BF16_ROWS, tm), BF16)


def _inproj(x1, mod3, wa, wfl, bfl, wr, triu, paugk, paugqt, cos2, sin2, intra, qdec, kdec, cdec, gng, gnb):
    b, s, d = x1.shape
    tm = TOKEN_TILE
    nt = s // tm
    c = RET_CHUNK
    in_specs = [
        _tile_spec(tm, d), pl.BlockSpec((1, N_MOD, d), lambda b, t: (b, 0, 0)),
        _const_spec((d, 3 * D_FOX)), _const_spec((d, LANES)), _const_spec((1, LANES)),
        _const_spec((d, 4 * D_RET)), _const_spec((tm, tm)), _const_spec((LANES, LANES)), _const_spec((LANES, LANES)),
        pl.BlockSpec((tm, LANES), lambda b, t: (t, 0)), pl.BlockSpec((tm, LANES), lambda b, t: (t, 0)),
        _const_spec((RET_HEADS, c, c)), _const_spec((RET_HEADS, c, LANES)), _const_spec((RET_HEADS, c, LANES)),
        _const_spec((RET_HEADS, 1, LANES)), _const_spec((1, D_RET)), _const_spec((1, D_RET)),
    ]
    out_specs = [
        pl.BlockSpec((1, 1, FOX_HEADS * LANES, tm), lambda b, t: (b, t, 0, 0)), _tile_spec(tm, FOX_HEADS * LANES),
        pl.BlockSpec((1, 1, FOX_HEADS * V_ROWS, tm), lambda b, t: (b, t, 0, 0)),
        _tile_spec(tm, D_RET),
    ]
    out_shape = [
        jax.ShapeDtypeStruct((b, nt, FOX_HEADS * LANES, tm), BF16), jax.ShapeDtypeStruct((b, s, FOX_HEADS * LANES), BF16),
        jax.ShapeDtypeStruct((b, nt, FOX_HEADS * V_ROWS, tm), BF16),
        jax.ShapeDtypeStruct((b, s, D_RET), BF16),
    ]
    return pl.pallas_call(
        _inproj_kernel,
        grid=(b, nt),
        in_specs=in_specs, out_specs=out_specs, out_shape=out_shape,
        scratch_shapes=[pltpu.VMEM((FOX_HEADS, LANES), F32),
                        pltpu.VMEM((RET_HEADS, RET_HEAD_DIM, RET_HEAD_DIM), F32)],
        compiler_params=pltpu.CompilerParams(dimension_semantics=("arbitrary", "arbitrary"),
                                             vmem_limit_bytes=VMEM_LIMIT),
        name="inproj",
    )(x1, mod3, wa, wfl, bfl, wr, triu, paugk, paugqt, cos2, sin2, intra, qdec, kdec, cdec, gng, gnb)


def _fox_kernel(qpt_ref, kp_ref, fvt_ref, *rest):
    n_cast = (len(rest) - 6) // 2
    cast_in, o_ref, cast_out = rest[:n_cast], rest[n_cast], rest[n_cast + 1:2 * n_cast + 1]
    s_ref, p_ref, acc_ref, mask_ref, stat_ref = rest[2 * n_cast + 1:]
    for src, dst in zip(cast_in, cast_out):
        dst[...] = src[...].astype(BF16)
    _fox_attention(qpt_ref, kp_ref, fvt_ref, o_ref, s_ref, p_ref, acc_ref, mask_ref, stat_ref)


def _fox_attention(qpt_ref, kp_ref, fvt_ref, o_ref, s_ref, p_ref, acc_ref, mask_ref, stat_ref):
    t = ATTN_TILE
    nq = qpt_ref.shape[1]
    half = FOX_HEAD_DIM

    k_pos = lax.broadcasted_iota(jnp.int32, (t, t), 0)
    q_pos = lax.broadcasted_iota(jnp.int32, (t, t), 1)
    mask_ref[0] = jnp.zeros((t, t), F32)
    mask_ref[1] = jnp.where(k_pos <= q_pos, 0.0, -jnp.inf)
    acc_ref[...] = jnp.zeros_like(acc_ref)
    stat_ref[...] = jnp.zeros_like(stat_ref)

    unroll = ATTN_UNROLL
    row_m, row_alpha_new, row_alpha_old = 1, 1 + unroll, 1 + 2 * unroll
    steps = [(q, k) for q in range(nq) for k in range(q + 1)]
    n_groups = len(steps) // unroll
    assert len(steps) % unroll == 0 and n_groups >= 4 and 1 + 3 * unroll <= stat_ref.shape[1]

    def stat(hh, row):
        return stat_ref[hh, row:row + 1, :]

    def stage_a(hh, group):
        alpha_new = [stat(hh, row_alpha_new + u) for u in range(unroll)]
        m = stat(hh, 0)
        for u, (qa, ka) in enumerate(group):
            static = isinstance(ka, int)
            krows = pl.ds(ka * t, t) if static else pl.ds(pl.multiple_of(ka * t, t), t)
            diag = int(ka == qa) if static else (ka == qa).astype(jnp.int32)
            s = jnp.dot(kp_ref[0, krows, hh * LANES:(hh + 1) * LANES],
                        qpt_ref[0, qa, hh * LANES:(hh + 1) * LANES, :],
                        preferred_element_type=F32) + mask_ref[diag]
            s_ref[u * ATTN_HEADS + hh] = s
            m_prev = jnp.where(ka == 0, -jnp.inf, m)
            m = jnp.maximum(m_prev, jnp.max(s, axis=0, keepdims=True))
            stat_ref[hh, row_m + u:row_m + u + 1, :] = m
            stat_ref[hh, row_alpha_new + u:row_alpha_new + u + 1, :] = jnp.exp2(m_prev - m)
            stat_ref[hh, row_alpha_old + u:row_alpha_old + u + 1, :] = alpha_new[u]
        stat_ref[hh, 0:1, :] = m

    def stage_b(hh):
        for u in range(unroll):
            buf = u * ATTN_HEADS + hh
            p_ref[buf] = jnp.exp2(s_ref[buf] - stat(hh, row_m + u)).astype(BF16)

    def stage_c(hh, group, alpha_row):
        acc = acc_ref[hh]
        for u, (qc, kc) in enumerate(group):
            vt = fvt_ref[0, kc, hh * V_ROWS:(hh + 1) * V_ROWS, :]
            acc = stat(hh, alpha_row + u) * acc + jnp.dot(vt, p_ref[u * ATTN_HEADS + hh],
                                                          preferred_element_type=F32)
            if isinstance(kc, int):
                if kc != qc:
                    continue
                tile = qc
            else:
                tile = jnp.where(kc == qc, qc, nq)
            o_ref[0, tile, hh * half:(hh + 1) * half, :] = (acc[:half] / acc[half:half + 1]).astype(BF16)
        acc_ref[hh] = acc

    def group(g):
        return steps[g * unroll:(g + 1) * unroll]

    def next_step(q, k):
        wrap = k >= q
        return jnp.where(wrap, q + 1, q), jnp.where(wrap, 0, k + 1)

    def body(j, carry):
        a_first, b_group, c_group = carry
        a_group = []
        nxt = a_first
        for _ in range(unroll):
            a_group.append(nxt)
            nxt = next_step(*nxt)
        for hh in range(ATTN_HEADS):
            stage_c(hh, c_group, row_alpha_old)
            stage_b(hh)
            stage_a(hh, a_group)
        return nxt, tuple(a_group), b_group

    as_i32 = lambda grp: tuple((jnp.int32(q), jnp.int32(k)) for q, k in grp)
    for hh in range(ATTN_HEADS):
        stage_a(hh, group(0))
    for hh in range(ATTN_HEADS):
        stage_b(hh)
        stage_a(hh, group(1))
    lax.fori_loop(2, n_groups, body, (as_i32(group(2))[0], as_i32(group(1)), as_i32(group(0))))
    for hh in range(ATTN_HEADS):
        stage_c(hh, group(n_groups - 2), row_alpha_old)
        stage_b(hh)
    for hh in range(ATTN_HEADS):
        stage_c(hh, group(n_groups - 1), row_alpha_new)


def _fox(qpt, kp, fvt, f32_weights):
    b, s, _ = kp.shape
    t = ATTN_TILE
    nt = s // t
    g = ATTN_HEADS
    groups = FOX_HEADS // g
    steps = b * groups
    slab_specs = []
    for w in f32_weights:
        assert w.shape[0] % (steps * BF16_ROWS) == 0
        slab_specs.append(pl.BlockSpec((w.shape[0] // steps, w.shape[1]), lambda b, p: (b * groups + p, 0)))
    return pl.pallas_call(
        _fox_kernel,
        grid=(b, groups),
        in_specs=[pl.BlockSpec((1, nt, g * LANES, t), lambda b, p: (b, 0, p, 0)),
                  pl.BlockSpec((1, s, g * LANES), lambda b, p: (b, 0, p)),
                  pl.BlockSpec((1, nt, g * V_ROWS, t), lambda b, p: (b, 0, p, 0))] + slab_specs,
        out_specs=[pl.BlockSpec((1, nt + 1, g * FOX_HEAD_DIM, t), lambda b, p: (b, 0, p, 0))] + slab_specs,
        out_shape=[jax.ShapeDtypeStruct((b, nt + 1, D_FOX, t), BF16)]
        + [jax.ShapeDtypeStruct(w.shape, BF16) for w in f32_weights],
        scratch_shapes=[pltpu.VMEM((ATTN_UNROLL * g, t, t), F32), pltpu.VMEM((ATTN_UNROLL * g, t, t), BF16),
                        pltpu.VMEM((g, V_ROWS, t), F32), pltpu.VMEM((2, t, t), F32),
                        pltpu.VMEM((g, 8 * pl.cdiv(1 + 3 * ATTN_UNROLL, 8), t), F32)],
        compiler_params=pltpu.CompilerParams(dimension_semantics=("arbitrary", "arbitrary"),
                                             vmem_limit_bytes=VMEM_LIMIT),
        name="fox",
    )(qpt, kp, fvt, *f32_weights)


def _aug_placement():
    pk = np.zeros((LANES, LANES), np.float32)
    pq = np.zeros((LANES, LANES), np.float32)
    for head in range(FOX_HEADS):
        _, base = _aug_lanes(head)
        for j in range(3):
            pk[24, base + j] = 1.0
            pk[8 * j + head, base + 3 + j] = -1.0
            pq[base + j, 8 * j + head] = 1.0
            pq[base + 3 + j, 24] = 1.0
    return jnp.asarray(pk, BF16), jnp.asarray(pq, BF16)


def _rotary_tables(seq):
    half = RET_HEAD_DIM // 2
    inv_freq = ROPE_BASE ** (-np.arange(half, dtype=np.float64) / half)
    ang = np.arange(seq, dtype=np.float64)[:, None] * inv_freq[None, :]
    cos, sin = np.cos(ang), np.sin(ang)
    return (jnp.asarray(np.concatenate([cos, cos], axis=-1), F32),
            jnp.asarray(np.concatenate([-sin, sin], axis=-1), F32))


def _retention_tables():
    c = RET_CHUNK
    log_gamma = np.log1p(-np.power(2.0, -5.0 - np.arange(RET_HEADS, dtype=np.float64)))
    idx = np.arange(c, dtype=np.float64)
    diff = idx[:, None] - idx[None, :]
    intra = np.where(diff >= 0, np.exp(log_gamma[:, None, None] * np.maximum(diff, 0.0)), 0.0)
    qdec = np.exp(log_gamma[:, None] * (idx + 1.0))[..., None]
    kdec = np.exp(log_gamma[:, None] * (c - 1.0 - idx))[..., None]
    cdec = np.exp(log_gamma * c)[:, None, None]
    bc = lambda a: jnp.asarray(np.broadcast_to(a, a.shape[:2] + (LANES,)), F32)
    return jnp.asarray(intra, F32), bc(qdec), bc(kdec), bc(cdec)


def kernel(x, c, w_ada, b_ada, ffn1_w_gate, ffn1_w_up, ffn1_w_down, ln1_g, ln1_b, w_in, fox_b_f, ret_gn_g,
           ret_gn_b, w_o, ln2_g, ln2_b, ffn2_w_gate, ffn2_w_up, ffn2_w_down, ln3_g, ln3_b):
    batch, seq, d = x.shape
    assert d == D_MODEL and seq % FFN_TILE == 0 and FFN_TILE % TOKEN_TILE == 0 and w_ada.shape[0] == DEPTH
    layer = 0

    c_pad = jnp.pad(c, ((0, 8 - batch), (0, 0)))
    mod3 = _ada(c_pad, w_ada[layer], b_ada[layer][None, :])[:batch].reshape(batch, N_MOD, d)

    w = w_in[layer]
    o_fl = 3 * D_FOX
    o_r = o_fl + FOX_HEADS
    q_scale = LOG2E * FOX_HEAD_DIM ** -0.5
    x1, wa, wr = _ffn(x, mod3, ffn1_w_gate[layer].astype(BF16), ffn1_w_up[layer].astype(BF16),
                      ffn1_w_down[layer].astype(BF16), ln1_g[layer][None, :], ln1_b[layer][None, :], 0,
                      w, q_scale)

    wfl = jnp.pad(w[:, o_fl:o_r], ((0, 0), (0, LANES - FOX_HEADS))).astype(BF16)
    bfl = jnp.pad(fox_b_f[layer], (0, LANES - FOX_HEADS))[None, :]
    triu = jnp.asarray(np.triu(np.ones((TOKEN_TILE, TOKEN_TILE), np.float32)), BF16)
    paugk, paugqt = _aug_placement()
    cos2, sin2 = _rotary_tables(seq)
    intra, qdec, kdec, cdec = _retention_tables()
    qpt, kp, fvt, ret = _inproj(x1, mod3, wa, wfl, bfl, wr, triu, paugk, paugqt, cos2, sin2,
                               intra, qdec, kdec, cdec, ret_gn_g[layer][None, :], ret_gn_b[layer][None, :])

    foxt, wo, wg2, wu2, wd2 = _fox(qpt, kp, fvt, [w_o[layer], ffn2_w_gate[layer], ffn2_w_up[layer],
                                                  ffn2_w_down[layer]])

    return _mix_ffn(foxt, ret, x1, mod3, wo, ln2_g[layer][None, :], ln2_b[layer][None, :],
                    wg2, wu2, wd2, ln3_g[layer][None, :], ln3_b[layer][None, :])
```

```python
import functools
import math

import numpy as np
import jax
import jax.numpy as jnp
from jax import lax
from jax.experimental import pallas as pl
from jax.experimental.pallas import tpu as pltpu

F32 = jnp.float32
BF16 = jnp.bfloat16

D_MODEL = 1024
D_FF = 2816
D_FOX = 512
D_RET = 512
FOX_HEADS = 8
FOX_HEAD_DIM = 64
RET_HEADS = 4
RET_HEAD_DIM = 128
N_MOD = 9
ROPE_BASE = 10000.0
LN_EPS = 1e-5
DEPTH = 1
DEEPNORM_ALPHA = (2.0 * DEPTH) ** 0.25
FFN_RES_WEIGHT = 0.5
LOG2E = math.log2(math.e)

LANES = 128
BF16_ROWS = 16
TOKEN_TILE = 512
FFN_TILE = 1024
FFN_GROUPS = (256, 256, 256, 256)
FF_CHUNK = 256
RET_CHUNK = 256
ATTN_TILE = TOKEN_TILE
ATTN_HEADS = 2
ATTN_UNROLL = 2
AUG_COLS = 6
V_ROWS = FOX_HEAD_DIM + BF16_ROWS
VMEM_LIMIT = 56 * 2 ** 20

NT_DIMS = (((1,), (1,)), ((), ()))
TN_DIMS = (((0,), (0,)), ((), ()))


def _silu(v):
    return v / (1.0 + jnp.exp(-v))


def _layer_norm(y, g, b):
    mu = jnp.mean(y, axis=-1, keepdims=True)
    d = y - mu
    var = jnp.mean(d * d, axis=-1, keepdims=True)
    return d * lax.rsqrt(var + LN_EPS) * g + b


def _split3(v):
    hi = v.astype(BF16)
    r = v - hi.astype(F32)
    mid = r.astype(BF16)
    lo = (r - mid.astype(F32)).astype(BF16)
    return hi, mid, lo


def _stack_terms(v, tail, rows):
    hi, mid, lo = _split3(v)
    parts = [hi.astype(F32), mid.astype(F32), lo.astype(F32), tail]
    if rows > 32:
        parts.append(jnp.zeros((rows - 32, v.shape[1]), F32))
    return jnp.concatenate(parts, axis=0).astype(BF16)


def _const_spec(shape):
    zeros = (0,) * len(shape)
    return pl.BlockSpec(shape, lambda *_: zeros, pipeline_mode=pl.Buffered(1))


def _aug_lanes(head):
    half = FOX_HEAD_DIM
    pair, odd = divmod(head, 2)
    data_lo = half if odd else 0
    aug_lo = (0 if odd else half) + AUG_COLS * pair
    return data_lo, aug_lo


def _ada_kernel(c_ref, w_ref, b_ref, o_ref):
    ca = _silu(c_ref[...]).astype(BF16)
    o_ref[...] = jnp.dot(ca, w_ref[...].astype(BF16), preferred_element_type=F32) + b_ref[...]


def _ada(c_pad, w, b):
    rows, d = c_pad.shape
    n = w.shape[1]
    tn = 1024
    return pl.pallas_call(
        _ada_kernel,
        grid=(n // tn,),
        in_specs=[pl.BlockSpec((rows, d), lambda j: (0, 0)),
                  pl.BlockSpec((d, tn), lambda j: (0, j)),
                  pl.BlockSpec((1, tn), lambda j: (0, j))],
        out_specs=pl.BlockSpec((rows, tn), lambda j: (0, j)),
        out_shape=jax.ShapeDtypeStruct((rows, n), F32),
        compiler_params=pltpu.CompilerParams(dimension_semantics=("arbitrary",)),
        name="ada",
    )(c_pad, w, b)


def _ffn_group_rows(i):
    start = sum(FFN_GROUPS[:i])
    return slice(start, start + FFN_GROUPS[i])


def _swiglu_ln_tile(load_x, store_out, mod_ref, mod_base, gate_w, up_w, down_w, lng_ref, lnb_ref, act_ref):
    n_sub = len(FFN_GROUPS)
    n_chunks = D_FF // FF_CHUNK
    lead_finish, lead_next = 2, 6
    sh = mod_ref[0, mod_base:mod_base + 1, :]
    sc = mod_ref[0, mod_base + 1:mod_base + 2, :]
    g = mod_ref[0, mod_base + 2:mod_base + 3, :]
    xs, hs = {}, {}
    rows_of = _ffn_group_rows

    def begin(i):
        xs[i] = load_x(i)
        hs[i] = (xs[i] * (1.0 + sc) + sh).astype(BF16)

    def chunks(i, first, last):
        for j in range(first, last):
            cols = slice(j * FF_CHUNK, (j + 1) * FF_CHUNK)
            gate = jnp.dot(hs[i], gate_w(j), preferred_element_type=F32)
            up = jnp.dot(hs[i], up_w(j), preferred_element_type=F32)
            act_ref[rows_of(i), cols] = (_silu(gate) * up).astype(BF16)

    def finish(i):
        f = jnp.dot(act_ref[rows_of(i), :], down_w(), preferred_element_type=F32)
        y = DEEPNORM_ALPHA * xs.pop(i) + (FFN_RES_WEIGHT * g) * f
        store_out(i, _layer_norm(y, lng_ref[...], lnb_ref[...]))

    begin(0)
    for i in range(n_sub):
        chunks(i, 0, lead_finish)
        if i > 0:
            finish(i - 1)
        chunks(i, lead_finish, lead_next)
        if i + 1 < n_sub:
            begin(i + 1)
        chunks(i, lead_next, n_chunks)
    finish(n_sub - 1)


def _resident_ffn_weights(wg_ref, wu_ref, wd_ref):
    def cols(j):
        return slice(j * FF_CHUNK, (j + 1) * FF_CHUNK)
    return (lambda j: wg_ref[:, cols(j)]), (lambda j: wu_ref[:, cols(j)]), (lambda: wd_ref[...])


def _ffn_kernel(x_ref, mod_ref, wgc_ref, wuc_ref, wdc_ref, lng_ref, lnb_ref, win_ref,
                o_ref, wa_ref, wr_ref, act_ref, wg_s, wu_s, wd_s, *, mod_base, q_scale):
    step = pl.program_id(0)
    n_chunks = D_FF // FF_CHUNK

    @pl.when(step < n_chunks)
    def _():
        wg_s[step] = wgc_ref[...].astype(BF16)
        wu_s[step] = wuc_ref[...].astype(BF16)
        wd_s[step] = wdc_ref[...].astype(BF16)

    @pl.when(step >= n_chunks)
    def _():
        o_fl = 3 * D_FOX
        o_r = o_fl + FOX_HEADS
        w = win_ref[...]
        wa_ref[:, 0:D_FOX] = (w[:, 0:D_FOX] * q_scale).astype(BF16)
        wa_ref[:, D_FOX:o_fl] = w[:, D_FOX:o_fl].astype(BF16)
        wr_ref[...] = w[:, o_r:o_r + 4 * D_RET].astype(BF16)
        rows_of = _ffn_group_rows

        def load_x(i):
            return x_ref[0, rows_of(i)]

        def store_out(i, value):
            o_ref[0, rows_of(i)] = value

        _swiglu_ln_tile(load_x, store_out, mod_ref, mod_base, lambda j: wg_s[j], lambda j: wu_s[j],
                        lambda: wd_s[...].reshape(D_FF, D_MODEL), lng_ref, lnb_ref, act_ref)


def _mix_ffn_kernel(foxt_ref, ret_ref, x_ref, mod_ref, wo_ref, ln2g_ref, ln2b_ref,
                    wg_ref, wu_ref, wd_ref, lng_ref, lnb_ref, o_ref, act_ref):
    g2 = mod_ref[0, 5:6, :]
    rows_of = _ffn_group_rows

    def load_x(i):
        tile, part = divmod(rows_of(i).start, ATTN_TILE)
        fox_t = foxt_ref[0, tile, :, part:part + FFN_GROUPS[i]]
        m = (lax.dot_general(fox_t, wo_ref[0:D_FOX, :], TN_DIMS, preferred_element_type=F32)
             + jnp.dot(ret_ref[0, rows_of(i)], wo_ref[D_FOX:D_FOX + D_RET, :], preferred_element_type=F32))
        return _layer_norm(DEEPNORM_ALPHA * x_ref[0, rows_of(i)] + g2 * m, ln2g_ref[...], ln2b_ref[...])

    def store_out(i, value):
        o_ref[0, rows_of(i)] = value

    _swiglu_ln_tile(load_x, store_out, mod_ref, 6, *_resident_ffn_weights(wg_ref, wu_ref, wd_ref),
                    lng_ref, lnb_ref, act_ref)


def _tile_spec(tile, width):
    return pl.BlockSpec((1, tile, width), lambda b, t: (b, t, 0))


def _ffn_weight_specs():
    return [_const_spec((D_MODEL, D_FF)), _const_spec((D_MODEL, D_FF)), _const_spec((D_FF, D_MODEL)),
            _const_spec((1, D_MODEL)), _const_spec((1, D_MODEL))]


def _ffn(x, mod3, wg, wu, wd, lng, lnb, mod_base, w_in, q_scale):
    b, s, d = x.shape
    tiles = s // FFN_TILE
    steps = b * tiles
    rows = d // steps
    n_chunks = D_FF // FF_CHUNK
    assert d % (steps * BF16_ROWS) == 0

    def tile_of(i):
        return jnp.maximum(i - n_chunks, 0)

    def chunk_of(i):
        return jnp.minimum(i, n_chunks - 1)

    def slab(width):
        return pl.BlockSpec((rows, width), lambda i: (tile_of(i), 0))

    tile_spec = pl.BlockSpec((1, FFN_TILE, d), lambda i: (tile_of(i) // tiles, tile_of(i) % tiles, 0))
    return pl.pallas_call(
        functools.partial(_ffn_kernel, mod_base=mod_base, q_scale=q_scale),
        grid=(n_chunks + steps,),
        in_specs=[tile_spec, pl.BlockSpec((1, N_MOD, d), lambda i: (tile_of(i) // tiles, 0, 0)),
                  pl.BlockSpec((d, FF_CHUNK), lambda i: (0, chunk_of(i))),
                  pl.BlockSpec((d, FF_CHUNK), lambda i: (0, chunk_of(i))),
                  pl.BlockSpec((FF_CHUNK, d), lambda i: (chunk_of(i), 0)),
                  _const_spec((1, d)), _const_spec((1, d)), slab(w_in.shape[1])],
        out_specs=[tile_spec, slab(3 * D_FOX), slab(4 * D_RET)],
        out_shape=[jax.ShapeDtypeStruct((b, s, d), F32), jax.ShapeDtypeStruct((d, 3 * D_FOX), BF16),
                   jax.ShapeDtypeStruct((d, 4 * D_RET), BF16)],
        scratch_shapes=[pltpu.VMEM((FFN_TILE, D_FF), BF16),
                        pltpu.VMEM((n_chunks, d, FF_CHUNK), BF16), pltpu.VMEM((n_chunks, d, FF_CHUNK), BF16),
                        pltpu.VMEM((n_chunks, FF_CHUNK, d), BF16)],
        compiler_params=pltpu.CompilerParams(dimension_semantics=("arbitrary",), vmem_limit_bytes=VMEM_LIMIT),
        name="ffn1",
    )(x, mod3, wg, wu, wd, lng, lnb, w_in)


def _mix_ffn(foxt, ret, x1, mod3, wo, ln2g, ln2b, wg, wu, wd, lng, lnb):
    b, s, d = x1.shape
    per = FFN_TILE // ATTN_TILE
    return pl.pallas_call(
        _mix_ffn_kernel,
        grid=(b, s // FFN_TILE),
        in_specs=[pl.BlockSpec((1, per, D_FOX, ATTN_TILE), lambda b, t: (b, t, 0, 0)),
                  _tile_spec(FFN_TILE, D_RET), _tile_spec(FFN_TILE, d),
                  pl.BlockSpec((1, N_MOD, d), lambda b, t: (b, 0, 0)),
                  _const_spec((D_FOX + D_RET, d)), _const_spec((1, d)), _const_spec((1, d))] + _ffn_weight_specs(),
        out_specs=_tile_spec(FFN_TILE, d),
        out_shape=jax.ShapeDtypeStruct((b, s, d), F32),
        scratch_shapes=[pltpu.VMEM((FFN_TILE, D_FF), BF16)],
        compiler_params=pltpu.CompilerParams(dimension_semantics=("arbitrary", "arbitrary"),
                                             vmem_limit_bytes=VMEM_LIMIT),
        name="ffn2",
    )(foxt, ret, x1, mod3, wo, ln2g, ln2b, wg, wu, wd, lng, lnb)


def _inproj_kernel(x_ref, mod_ref, wa_ref, wfl_ref, bfl_ref, wr_ref, triu_ref, paugk_ref, paugqt_ref,
                   cos_ref, sin_ref, intra_ref, qdec_ref, kdec_ref, cdec_ref, gng_ref, gnb_ref,
                   qpt_ref, kp_ref, fvt_ref, ret_ref, carry_ref, state_ref):
    tm = TOKEN_TILE
    half = FOX_HEAD_DIM

    @pl.when(pl.program_id(1) == 0)
    def _():
        carry_ref[...] = jnp.zeros_like(carry_ref)
        state_ref[...] = jnp.zeros_like(state_ref)

    sh = mod_ref[0, 3:4, :]
    sc = mod_ref[0, 4:5, :]
    h = (x_ref[0] * (1.0 + sc) + sh).astype(BF16)

    r_all = jnp.dot(h, wr_ref[...], preferred_element_type=F32)
    cos2 = cos_ref[...]
    sin2 = sin_ref[...]
    c = RET_CHUNK
    for hd in range(RET_HEADS):
        lo_, hi_ = hd * RET_HEAD_DIM, (hd + 1) * RET_HEAD_DIM
        tq = r_all[:, lo_:hi_]
        tk = r_all[:, D_RET + lo_:D_RET + hi_]
        q_r = tq * cos2 + pltpu.roll(tq, RET_HEAD_DIM // 2, 1) * sin2
        k_r = (tk * cos2 + pltpu.roll(tk, RET_HEAD_DIM // 2, 1) * sin2) * (RET_HEAD_DIM ** -0.5)
        v = r_all[:, 2 * D_RET + lo_:2 * D_RET + hi_]
        gate = r_all[:, 3 * D_RET + lo_:3 * D_RET + hi_]
        state = state_ref[hd]
        for ci in range(tm // c):
            rows = slice(ci * c, (ci + 1) * c)
            qc = q_r[rows].astype(BF16)
            kc = k_r[rows]
            vc = v[rows].astype(BF16)
            s = lax.dot_general(qc, kc.astype(BF16), NT_DIMS, preferred_element_type=F32) * intra_ref[hd]
            o = (jnp.dot(s.astype(BF16), vc, preferred_element_type=F32)
                 + qdec_ref[hd] * jnp.dot(qc, state.astype(BF16), preferred_element_type=F32))
            state = state * cdec_ref[hd] + lax.dot_general(
                (kc * kdec_ref[hd]).astype(BF16), vc, TN_DIMS, preferred_element_type=F32)
            mu = jnp.mean(o, axis=-1, keepdims=True)
            d = o - mu
            var = jnp.mean(d * d, axis=-1, keepdims=True)
            yn = d * lax.rsqrt(var + LN_EPS) * gng_ref[:, lo_:hi_] + gnb_ref[:, lo_:hi_]
            ret_ref[0, rows, lo_:hi_] = (_silu(gate[rows]) * yn).astype(BF16)
        state_ref[hd] = state

    z = jnp.dot(h, wfl_ref[...], preferred_element_type=F32) + bfl_ref[...]
    logf = jnp.minimum(z, 0.0) - jnp.log1p(jnp.exp(-jnp.abs(z)))
    logf_t = logf.T[0:FOX_HEADS, :]
    terms = _stack_terms(logf_t, jnp.zeros((FOX_HEADS, tm), F32), 4 * FOX_HEADS)
    sums = jnp.dot(terms, triu_ref[...], preferred_element_type=F32)
    cum_t = (sums[0:8] + sums[8:16] + sums[16:24]) + carry_ref[:, 0:1]
    carry_ref[...] = jnp.broadcast_to(cum_t[:, tm - 1:tm], carry_ref.shape)

    row8 = lax.broadcasted_iota(jnp.int32, (FOX_HEADS, tm), 0)
    packed_t = _stack_terms(cum_t * LOG2E, jnp.where(row8 == 0, 1.0, 0.0), LANES)
    aug_k = lax.dot_general(packed_t, paugk_ref[...], TN_DIMS, preferred_element_type=F32)
    aug_qt = jnp.dot(paugqt_ref[...], packed_t, preferred_element_type=F32)

    qk = jnp.dot(h, wa_ref[:, 0:2 * D_FOX], preferred_element_type=F32)
    lane = lax.broadcasted_iota(jnp.int32, (tm, LANES), 1)
    row = lax.broadcasted_iota(jnp.int32, (LANES, tm), 0)
    q_t = [qk[:, pr * LANES:(pr + 1) * LANES].T for pr in range(FOX_HEADS // 2)]
    for hd in range(FOX_HEADS):
        data_lo, aug_lo = _aug_lanes(hd)
        src = (hd // 2) * LANES
        out = slice(hd * LANES, (hd + 1) * LANES)
        is_data = (lane >= data_lo) & (lane < data_lo + half)
        is_aug = (lane >= aug_lo) & (lane < aug_lo + AUG_COLS)
        kp_ref[0, :, out] = jnp.where(is_data, qk[:, D_FOX + src:D_FOX + src + LANES],
                                      jnp.where(is_aug, aug_k, 0.0)).astype(BF16)
        is_data_t = (row >= data_lo) & (row < data_lo + half)
        is_aug_t = (row >= aug_lo) & (row < aug_lo + AUG_COLS)
        qpt_ref[0, 0, out, :] = jnp.where(is_data_t, q_t[hd // 2],
                                          jnp.where(is_aug_t, aug_qt, 0.0)).astype(BF16)

    vt = jnp.dot(h, wa_ref[:, 2 * D_FOX:3 * D_FOX], preferred_element_type=F32).T.astype(BF16)
    for hd in range(FOX_HEADS):
        fvt_ref[0, 0, hd * V_ROWS:hd * V_ROWS + half, :] = vt[hd * half:(hd + 1) * half]
        fvt_ref[0, 0, hd * V_ROWS + half:(hd + 1) * V_ROWS, :] = jnp.ones((BF16_ROWS, tm), BF16)


def _inproj(x1, mod3, wa, wfl, bfl, wr, triu, paugk, paugqt, cos2, sin2, intra, qdec, kdec, cdec, gng, gnb):
    b, s, d = x1.shape
    tm = TOKEN_TILE
    nt = s // tm
    c = RET_CHUNK
    in_specs = [
        _tile_spec(tm, d), pl.BlockSpec((1, N_MOD, d), lambda b, t: (b, 0, 0)),
        _const_spec((d, 3 * D_FOX)), _const_spec((d, LANES)), _const_spec((1, LANES)),
        _const_spec((d, 4 * D_RET)), _const_spec((tm, tm)), _const_spec((LANES, LANES)), _const_spec((LANES, LANES)),
        pl.BlockSpec((tm, LANES), lambda b, t: (t, 0)), pl.BlockSpec((tm, LANES), lambda b, t: (t, 0)),
        _const_spec((RET_HEADS, c, c)), _const_spec((RET_HEADS, c, LANES)), _const_spec((RET_HEADS, c, LANES)),
        _const_spec((RET_HEADS, 1, LANES)), _const_spec((1, D_RET)), _const_spec((1, D_RET)),
    ]
    out_specs = [
        pl.BlockSpec((1, 1, FOX_HEADS * LANES, tm), lambda b, t: (b, t, 0, 0)), _tile_spec(tm, FOX_HEADS * LANES),
        pl.BlockSpec((1, 1, FOX_HEADS * V_ROWS, tm), lambda b, t: (b, t, 0, 0)),
        _tile_spec(tm, D_RET),
    ]
    out_shape = [
        jax.ShapeDtypeStruct((b, nt, FOX_HEADS * LANES, tm), BF16), jax.ShapeDtypeStruct((b, s, FOX_HEADS * LANES), BF16),
        jax.ShapeDtypeStruct((b, nt, FOX_HEADS * V_ROWS, tm), BF16),
        jax.ShapeDtypeStruct((b, s, D_RET), BF16),
    ]
    return pl.pallas_call(
        _inproj_kernel,
        grid=(b, nt),
        in_specs=in_specs, out_specs=out_specs, out_shape=out_shape,
        scratch_shapes=[pltpu.VMEM((FOX_HEADS, LANES), F32),
                        pltpu.VMEM((RET_HEADS, RET_HEAD_DIM, RET_HEAD_DIM), F32)],
        compiler_params=pltpu.CompilerParams(dimension_semantics=("arbitrary", "arbitrary"),
                                             vmem_limit_bytes=VMEM_LIMIT),
        name="inproj",
    )(x1, mod3, wa, wfl, bfl, wr, triu, paugk, paugqt, cos2, sin2, intra, qdec, kdec, cdec, gng, gnb)


def _fox_kernel(qpt_ref, kp_ref, fvt_ref, *rest):
    n_cast = (len(rest) - 6) // 2
    cast_in, o_ref, cast_out = rest[:n_cast], rest[n_cast], rest[n_cast + 1:2 * n_cast + 1]
    s_ref, p_ref, acc_ref, mask_ref, stat_ref = rest[2 * n_cast + 1:]
    for src, dst in zip(cast_in, cast_out):
        dst[...] = src[...].astype(BF16)
    _fox_attention(qpt_ref, kp_ref, fvt_ref, o_ref, s_ref, p_ref, acc_ref, mask_ref, stat_ref)


def _fox_attention(qpt_ref, kp_ref, fvt_ref, o_ref, s_ref, p_ref, acc_ref, mask_ref, stat_ref):
    t = ATTN_TILE
    nq = qpt_ref.shape[1]
    half = FOX_HEAD_DIM

    k_pos = lax.broadcasted_iota(jnp.int32, (t, t), 0)
    q_pos = lax.broadcasted_iota(jnp.int32, (t, t), 1)
    mask_ref[0] = jnp.zeros((t, t), F32)
    mask_ref[1] = jnp.where(k_pos <= q_pos, 0.0, -jnp.inf)
    acc_ref[...] = jnp.zeros_like(acc_ref)
    stat_ref[...] = jnp.zeros_like(stat_ref)

    unroll = ATTN_UNROLL
    row_m, row_alpha_new, row_alpha_old = 1, 1 + unroll, 1 + 2 * unroll
    steps = [(q, k) for q in range(nq) for k in range(q + 1)]
    n_groups = len(steps) // unroll
    assert len(steps) % unroll == 0 and n_groups >= 4 and 1 + 3 * unroll <= stat_ref.shape[1]

    def stat(hh, row):
        return stat_ref[hh, row:row + 1, :]

    def stage_a(hh, group):
        alpha_new = [stat(hh, row_alpha_new + u) for u in range(unroll)]
        m = stat(hh, 0)
        for u, (qa, ka) in enumerate(group):
            static = isinstance(ka, int)
            krows = pl.ds(ka * t, t) if static else pl.ds(pl.multiple_of(ka * t, t), t)
            diag = int(ka == qa) if static else (ka == qa).astype(jnp.int32)
            s = jnp.dot(kp_ref[0, krows, hh * LANES:(hh + 1) * LANES],
                        qpt_ref[0, qa, hh * LANES:(hh + 1) * LANES, :],
                        preferred_element_type=F32) + mask_ref[diag]
            s_ref[u * ATTN_HEADS + hh] = s
            m_prev = jnp.where(ka == 0, -jnp.inf, m)
            m = jnp.maximum(m_prev, jnp.max(s, axis=0, keepdims=True))
            stat_ref[hh, row_m + u:row_m + u + 1, :] = m
            stat_ref[hh, row_alpha_new + u:row_alpha_new + u + 1, :] = jnp.exp2(m_prev - m)
            stat_ref[hh, row_alpha_old + u:row_alpha_old + u + 1, :] = alpha_new[u]
        stat_ref[hh, 0:1, :] = m

    def stage_b(hh):
        for u in range(unroll):
            buf = u * ATTN_HEADS + hh
            p_ref[buf] = jnp.exp2(s_ref[buf] - stat(hh, row_m + u)).astype(BF16)

    def stage_c(hh, group, alpha_row):
        acc = acc_ref[hh]
        for u, (qc, kc) in enumerate(group):
            vt = fvt_ref[0, kc, hh * V_ROWS:(hh + 1) * V_ROWS, :]
            acc = stat(hh, alpha_row + u) * acc + jnp.dot(vt, p_ref[u * ATTN_HEADS + hh],
                                                          preferred_element_type=F32)
            if isinstance(kc, int):
                if kc != qc:
                    continue
                tile = qc
            else:
                tile = jnp.where(kc == qc, qc, nq)
            o_ref[0, tile, hh * half:(hh + 1) * half, :] = (acc[:half] / acc[half:half + 1]).astype(BF16)
        acc_ref[hh] = acc

    def group(g):
        return steps[g * unroll:(g + 1) * unroll]

    def next_step(q, k):
        wrap = k >= q
        return jnp.where(wrap, q + 1, q), jnp.where(wrap, 0, k + 1)

    def body(j, carry):
        a_first, b_group, c_group = carry
        a_group = []
        nxt = a_first
        for _ in range(unroll):
            a_group.append(nxt)
            nxt = next_step(*nxt)
        for hh in range(ATTN_HEADS):
            stage_c(hh, c_group, row_alpha_old)
            stage_b(hh)
            stage_a(hh, a_group)
        return nxt, tuple(a_group), b_group

    as_i32 = lambda grp: tuple((jnp.int32(q), jnp.int32(k)) for q, k in grp)
    for hh in range(ATTN_HEADS):
        stage_a(hh, group(0))
    for hh in range(ATTN_HEADS):
        stage_b(hh)
        stage_a(hh, group(1))
    lax.fori_loop(2, n_groups, body, (as_i32(group(2))[0], as_i32(group(1)), as_i32(group(0))))
    for hh in range(ATTN_HEADS):
        stage_c(hh, group(n_groups - 2), row_alpha_old)
        stage_b(hh)
    for hh in range(ATTN_HEADS):
        stage_c(hh, group(n_groups - 1), row_alpha_new)


def _fox(qpt, kp, fvt, f32_weights):
    b, s, _ = kp.shape
    t = ATTN_TILE
    nt = s // t
    g = ATTN_HEADS
    groups = FOX_HEADS // g
    steps = b * groups
    slab_specs = []
    for w in f32_weights:
        assert w.shape[0] % (steps * BF16_ROWS) == 0
        slab_specs.append(pl.BlockSpec((w.shape[0] // steps, w.shape[1]), lambda b, p: (b * groups + p, 0)))
    return pl.pallas_call(
        _fox_kernel,
        grid=(b, groups),
        in_specs=[pl.BlockSpec((1, nt, g * LANES, t), lambda b, p: (b, 0, p, 0)),
                  pl.BlockSpec((1, s, g * LANES), lambda b, p: (b, 0, p)),
                  pl.BlockSpec((1, nt, g * V_ROWS, t), lambda b, p: (b, 0, p, 0))] + slab_specs,
        out_specs=[pl.BlockSpec((1, nt + 1, g * FOX_HEAD_DIM, t), lambda b, p: (b, 0, p, 0))] + slab_specs,
        out_shape=[jax.ShapeDtypeStruct((b, nt + 1, D_FOX, t), BF16)]
        + [jax.ShapeDtypeStruct(w.shape, BF16) for w in f32_weights],
        scratch_shapes=[pltpu.VMEM((ATTN_UNROLL * g, t, t), F32), pltpu.VMEM((ATTN_UNROLL * g, t, t), BF16),
                        pltpu.VMEM((g, V_ROWS, t), F32), pltpu.VMEM((2, t, t), F32),
                        pltpu.VMEM((g, 8 * pl.cdiv(1 + 3 * ATTN_UNROLL, 8), t), F32)],
        compiler_params=pltpu.CompilerParams(dimension_semantics=("arbitrary", "arbitrary"),
                                             vmem_limit_bytes=VMEM_LIMIT),
        name="fox",
    )(qpt, kp, fvt, *f32_weights)


def _aug_placement():
    pk = np.zeros((LANES, LANES), np.float32)
    pq = np.zeros((LANES, LANES), np.float32)
    for head in range(FOX_HEADS):
        _, base = _aug_lanes(head)
        for j in range(3):
            pk[24, base + j] = 1.0
            pk[8 * j + head, base + 3 + j] = -1.0
            pq[base + j, 8 * j + head] = 1.0
            pq[base + 3 + j, 24] = 1.0
    return jnp.asarray(pk, BF16), jnp.asarray(pq, BF16)


def _rotary_tables(seq):
    half = RET_HEAD_DIM // 2
    inv_freq = ROPE_BASE ** (-np.arange(half, dtype=np.float64) / half)
    ang = np.arange(seq, dtype=np.float64)[:, None] * inv_freq[None, :]
    cos, sin = np.cos(ang), np.sin(ang)
    return (jnp.asarray(np.concatenate([cos, cos], axis=-1), F32),
            jnp.asarray(np.concatenate([-sin, sin], axis=-1), F32))


def _retention_tables():
    c = RET_CHUNK
    log_gamma = np.log1p(-np.power(2.0, -5.0 - np.arange(RET_HEADS, dtype=np.float64)))
    idx = np.arange(c, dtype=np.float64)
    diff = idx[:, None] - idx[None, :]
    intra = np.where(diff >= 0, np.exp(log_gamma[:, None, None] * np.maximum(diff, 0.0)), 0.0)
    qdec = np.exp(log_gamma[:, None] * (idx + 1.0))[..., None]
    kdec = np.exp(log_gamma[:, None] * (c - 1.0 - idx))[..., None]
    cdec = np.exp(log_gamma * c)[:, None, None]
    bc = lambda a: jnp.asarray(np.broadcast_to(a, a.shape[:2] + (LANES,)), F32)
    return jnp.asarray(intra, F32), bc(qdec), bc(kdec), bc(cdec)


def kernel(x, c, w_ada, b_ada, ffn1_w_gate, ffn1_w_up, ffn1_w_down, ln1_g, ln1_b, w_in, fox_b_f, ret_gn_g,
           ret_gn_b, w_o, ln2_g, ln2_b, ffn2_w_gate, ffn2_w_up, ffn2_w_down, ln3_g, ln3_b):
    batch, seq, d = x.shape
    assert d == D_MODEL and seq % FFN_TILE == 0 and FFN_TILE % TOKEN_TILE == 0 and w_ada.shape[0] == DEPTH
    assert sum(FFN_GROUPS) == FFN_TILE and all(
        _ffn_group_rows(i).start // ATTN_TILE == (_ffn_group_rows(i).stop - 1) // ATTN_TILE
        for i in range(len(FFN_GROUPS)))
    layer = 0

    c_pad = jnp.pad(c, ((0, 8 - batch), (0, 0)))
    mod3 = _ada(c_pad, w_ada[layer], b_ada[layer][None, :])[:batch].reshape(batch, N_MOD, d)

    w = w_in[layer]
    o_fl = 3 * D_FOX
    o_r = o_fl + FOX_HEADS
    q_scale = LOG2E * FOX_HEAD_DIM ** -0.5
    x1, wa, wr = _ffn(x, mod3, ffn1_w_gate[layer], ffn1_w_up[layer], ffn1_w_down[layer],
                      ln1_g[layer][None, :], ln1_b[layer][None, :], 0, w, q_scale)

    wfl = jnp.pad(w[:, o_fl:o_r], ((0, 0), (0, LANES - FOX_HEADS))).astype(BF16)
    bfl = jnp.pad(fox_b_f[layer], (0, LANES - FOX_HEADS))[None, :]
    triu = jnp.asarray(np.triu(np.ones((TOKEN_TILE, TOKEN_TILE), np.float32)), BF16)
    paugk, paugqt = _aug_placement()
    cos2, sin2 = _rotary_tables(seq)
    intra, qdec, kdec, cdec = _retention_tables()
    qpt, kp, fvt, ret = _inproj(x1, mod3, wa, wfl, bfl, wr, triu, paugk, paugqt, cos2, sin2,
                               intra, qdec, kdec, cdec, ret_gn_g[layer][None, :], ret_gn_b[layer][None, :])

    foxt, wo, wg2, wu2, wd2 = _fox(qpt, kp, fvt, [w_o[layer], ffn2_w_gate[layer], ffn2_w_up[layer],
                                                  ffn2_w_down[layer]])

    return _mix_ffn(foxt, ret, x1, mod3, wo, ln2_g[layer][None, :], ln2_b[layer][None, :],
                    wg2, wu2, wd2, ln3_g[layer][None, :], ln3_b[layer][None, :])
```

```python
import functools
import math

import numpy as np
import jax
import jax.numpy as jnp
from jax import lax
from jax.experimental import pallas as pl
from jax.experimental.pallas import tpu as pltpu

F32 = jnp.float32
BF16 = jnp.bfloat16

D_MODEL = 1024
D_FF = 2816
D_FOX = 512
D_RET = 512
FOX_HEADS = 8
FOX_HEAD_DIM = 64
RET_HEADS = 4
RET_HEAD_DIM = 128
N_MOD = 9
ROPE_BASE = 10000.0
LN_EPS = 1e-5
DEPTH = 1
DEEPNORM_ALPHA = (2.0 * DEPTH) ** 0.25
FFN_RES_WEIGHT = 0.5
LOG2E = math.log2(math.e)

LANES = 128
BF16_ROWS = 16
ADA_COLS = 2304
TOKEN_TILE = 512
FFN_TILE = 1024
FFN_GROUPS = (256, 256, 256, 256)
FF_CHUNK = 256
RET_CHUNK = 256
ATTN_TILE = TOKEN_TILE
ATTN_HEADS = 2
ATTN_UNROLL = 2
AUG_COLS = 6
V_ROWS = FOX_HEAD_DIM + BF16_ROWS
V7X_VMEM_BYTES = 64 * 2 ** 20
COMPILER_TEMP_BYTES = 12 * 2 ** 20

NT_DIMS = (((1,), (1,)), ((), ()))
TN_DIMS = (((0,), (0,)), ((), ()))


def _silu(v):
    return v / (1.0 + jnp.exp(-v))


def _layer_norm(y, g, b):
    mu = jnp.mean(y, axis=-1, keepdims=True)
    d = y - mu
    var = jnp.mean(d * d, axis=-1, keepdims=True)
    return d * lax.rsqrt(var + LN_EPS) * g + b


def _split3(v):
    hi = v.astype(BF16)
    r = v - hi.astype(F32)
    mid = r.astype(BF16)
    lo = (r - mid.astype(F32)).astype(BF16)
    return hi, mid, lo


def _stack_terms(v, tail, rows):
    hi, mid, lo = _split3(v)
    parts = [hi.astype(F32), mid.astype(F32), lo.astype(F32), tail]
    if rows > 32:
        parts.append(jnp.zeros((rows - 32, v.shape[1]), F32))
    return jnp.concatenate(parts, axis=0).astype(BF16)


def _const_spec(shape):
    zeros = (0,) * len(shape)
    return pl.BlockSpec(shape, lambda *_: zeros, pipeline_mode=pl.Buffered(1))


def _pallas_call(kernel, *, name, grid, in_specs, out_specs, out_shape, scratch_shapes, operands):
    out_specs_l = out_specs if isinstance(out_specs, (list, tuple)) else [out_specs]
    out_shape_l = out_shape if isinstance(out_shape, (list, tuple)) else [out_shape]

    def window(spec, dtype):
        single = spec.pipeline_mode is not None and spec.pipeline_mode.buffer_count == 1
        return (1 if single else 2) * math.prod(spec.block_shape) * jnp.dtype(dtype).itemsize

    declared = (sum(window(sp, op.dtype) for sp, op in zip(in_specs, operands))
                + sum(window(sp, sh.dtype) for sp, sh in zip(out_specs_l, out_shape_l))
                + sum(math.prod(sc.shape) * jnp.dtype(sc.dtype).itemsize for sc in scratch_shapes))
    limit = min(declared + COMPILER_TEMP_BYTES, V7X_VMEM_BYTES)
    return pl.pallas_call(
        kernel, grid=grid, in_specs=in_specs, out_specs=out_specs, out_shape=out_shape,
        scratch_shapes=scratch_shapes,
        compiler_params=pltpu.CompilerParams(dimension_semantics=("arbitrary",) * len(grid),
                                             vmem_limit_bytes=limit),
        name=name,
    )(*operands)


def _aug_lanes(head):
    half = FOX_HEAD_DIM
    pair, odd = divmod(head, 2)
    data_lo = half if odd else 0
    aug_lo = (0 if odd else half) + AUG_COLS * pair
    return data_lo, aug_lo


def _ada_kernel(c_ref, w_ref, b_ref, o_ref):
    ca = _silu(c_ref[...]).astype(BF16)
    o_ref[...] = jnp.dot(ca, w_ref[...].astype(BF16), preferred_element_type=F32) + b_ref[...]


def _ada(c_pad, w, b):
    rows, d = c_pad.shape
    n = w.shape[1]
    tn = ADA_COLS
    assert n % tn == 0
    return _pallas_call(
        _ada_kernel,
        name="ada", operands=[c_pad, w, b],
        grid=(n // tn,),
        in_specs=[pl.BlockSpec((rows, d), lambda j: (0, 0)),
                  pl.BlockSpec((d, tn), lambda j: (0, j)),
                  pl.BlockSpec((1, tn), lambda j: (0, j))],
        out_specs=pl.BlockSpec((rows, tn), lambda j: (0, j)),
        out_shape=jax.ShapeDtypeStruct((rows, n), F32),
        scratch_shapes=[])


def _ffn_group_rows(i):
    start = sum(FFN_GROUPS[:i])
    return slice(start, start + FFN_GROUPS[i])


def _swiglu_ln_tile(load_x, store_out, mod_ref, mod_base, gate_w, up_w, down_w, lng_ref, lnb_ref, act_ref):
    n_sub = len(FFN_GROUPS)
    n_chunks = D_FF // FF_CHUNK
    lead_finish, lead_next = 2, 6
    sh = mod_ref[0, mod_base:mod_base + 1, :]
    sc = mod_ref[0, mod_base + 1:mod_base + 2, :]
    g = mod_ref[0, mod_base + 2:mod_base + 3, :]
    xs, hs = {}, {}
    rows_of = _ffn_group_rows

    def begin(i):
        xs[i] = load_x(i)
        hs[i] = (xs[i] * (1.0 + sc) + sh).astype(BF16)

    def chunks(i, first, last):
        for j in range(first, last):
            cols = slice(j * FF_CHUNK, (j + 1) * FF_CHUNK)
            gate = jnp.dot(hs[i], gate_w(j), preferred_element_type=F32)
            up = jnp.dot(hs[i], up_w(j), preferred_element_type=F32)
            act_ref[rows_of(i), cols] = (_silu(gate) * up).astype(BF16)

    def finish(i):
        f = jnp.dot(act_ref[rows_of(i), :], down_w(), preferred_element_type=F32)
        y = DEEPNORM_ALPHA * xs.pop(i) + (FFN_RES_WEIGHT * g) * f
        store_out(i, _layer_norm(y, lng_ref[...], lnb_ref[...]))

    begin(0)
    for i in range(n_sub):
        chunks(i, 0, lead_finish)
        if i > 0:
            finish(i - 1)
        chunks(i, lead_finish, lead_next)
        if i + 1 < n_sub:
            begin(i + 1)
        chunks(i, lead_next, n_chunks)
    finish(n_sub - 1)


def _resident_ffn_weights(wg_ref, wu_ref, wd_ref):
    def cols(j):
        return slice(j * FF_CHUNK, (j + 1) * FF_CHUNK)
    return (lambda j: wg_ref[:, cols(j)]), (lambda j: wu_ref[:, cols(j)]), (lambda: wd_ref[...])


def _ffn_kernel(x_ref, mod_ref, wgc_ref, wuc_ref, wdc_ref, lng_ref, lnb_ref, win_ref,
                o_ref, wa_ref, wr_ref, act_ref, wg_s, wu_s, wd_s, *, mod_base, q_scale):
    step = pl.program_id(0)
    n_chunks = D_FF // FF_CHUNK

    @pl.when(step < n_chunks)
    def _():
        wg_s[step] = wgc_ref[...].astype(BF16)
        wu_s[step] = wuc_ref[...].astype(BF16)
        wd_s[step] = wdc_ref[...].astype(BF16)

    @pl.when(step >= n_chunks)
    def _():
        o_fl = 3 * D_FOX
        o_r = o_fl + FOX_HEADS
        w = win_ref[...]
        wa_ref[:, 0:D_FOX] = (w[:, 0:D_FOX] * q_scale).astype(BF16)
        wa_ref[:, D_FOX:o_fl] = w[:, D_FOX:o_fl].astype(BF16)
        wr_ref[...] = w[:, o_r:o_r + 4 * D_RET].astype(BF16)
        rows_of = _ffn_group_rows

        def load_x(i):
            return x_ref[0, rows_of(i)]

        def store_out(i, value):
            o_ref[0, rows_of(i)] = value

        _swiglu_ln_tile(load_x, store_out, mod_ref, mod_base, lambda j: wg_s[j], lambda j: wu_s[j],
                        lambda: wd_s[...].reshape(D_FF, D_MODEL), lng_ref, lnb_ref, act_ref)


def _mix_ffn_kernel(foxt_ref, ret_ref, x_ref, mod_ref, wo_ref, ln2g_ref, ln2b_ref,
                    wg_ref, wu_ref, wd_ref, lng_ref, lnb_ref, o_ref, act_ref):
    g2 = mod_ref[0, 5:6, :]
    rows_of = _ffn_group_rows

    def load_x(i):
        tile, part = divmod(rows_of(i).start, ATTN_TILE)
        fox_t = foxt_ref[0, tile, :, part:part + FFN_GROUPS[i]]
        m = (lax.dot_general(fox_t, wo_ref[0:D_FOX, :], TN_DIMS, preferred_element_type=F32)
             + jnp.dot(ret_ref[0, rows_of(i)], wo_ref[D_FOX:D_FOX + D_RET, :], preferred_element_type=F32))
        return _layer_norm(DEEPNORM_ALPHA * x_ref[0, rows_of(i)] + g2 * m, ln2g_ref[...], ln2b_ref[...])

    def store_out(i, value):
        o_ref[0, rows_of(i)] = value

    _swiglu_ln_tile(load_x, store_out, mod_ref, 6, *_resident_ffn_weights(wg_ref, wu_ref, wd_ref),
                    lng_ref, lnb_ref, act_ref)


def _tile_spec(tile, width):
    return pl.BlockSpec((1, tile, width), lambda b, t: (b, t, 0))


def _ffn_weight_specs():
    return [_const_spec((D_MODEL, D_FF)), _const_spec((D_MODEL, D_FF)), _const_spec((D_FF, D_MODEL)),
            _const_spec((1, D_MODEL)), _const_spec((1, D_MODEL))]


def _ffn(x, mod3, wg, wu, wd, lng, lnb, mod_base, w_in, q_scale):
    b, s, d = x.shape
    tiles = s // FFN_TILE
    steps = b * tiles
    rows = d // steps
    n_chunks = D_FF // FF_CHUNK
    assert d % (steps * BF16_ROWS) == 0

    def tile_of(i):
        return jnp.maximum(i - n_chunks, 0)

    def chunk_of(i):
        return jnp.minimum(i, n_chunks - 1)

    def slab(width):
        return pl.BlockSpec((rows, width), lambda i: (tile_of(i), 0))

    tile_spec = pl.BlockSpec((1, FFN_TILE, d), lambda i: (tile_of(i) // tiles, tile_of(i) % tiles, 0))
    return _pallas_call(
        functools.partial(_ffn_kernel, mod_base=mod_base, q_scale=q_scale),
        name="ffn1", operands=[x, mod3, wg, wu, wd, lng, lnb, w_in],
        grid=(n_chunks + steps,),
        in_specs=[tile_spec, pl.BlockSpec((1, N_MOD, d), lambda i: (tile_of(i) // tiles, 0, 0)),
                  pl.BlockSpec((d, FF_CHUNK), lambda i: (0, chunk_of(i))),
                  pl.BlockSpec((d, FF_CHUNK), lambda i: (0, chunk_of(i))),
                  pl.BlockSpec((FF_CHUNK, d), lambda i: (chunk_of(i), 0)),
                  _const_spec((1, d)), _const_spec((1, d)), slab(w_in.shape[1])],
        out_specs=[tile_spec, slab(3 * D_FOX), slab(4 * D_RET)],
        out_shape=[jax.ShapeDtypeStruct((b, s, d), F32), jax.ShapeDtypeStruct((d, 3 * D_FOX), BF16),
                   jax.ShapeDtypeStruct((d, 4 * D_RET), BF16)],
        scratch_shapes=[pltpu.VMEM((FFN_TILE, D_FF), BF16),
                        pltpu.VMEM((n_chunks, d, FF_CHUNK), BF16), pltpu.VMEM((n_chunks, d, FF_CHUNK), BF16),
                        pltpu.VMEM((n_chunks, FF_CHUNK, d), BF16)])


def _mix_ffn(foxt, ret, x1, mod3, wo, ln2g, ln2b, wg, wu, wd, lng, lnb):
    b, s, d = x1.shape
    per = FFN_TILE // ATTN_TILE
    return _pallas_call(
        _mix_ffn_kernel,
        name="ffn2", operands=[foxt, ret, x1, mod3, wo, ln2g, ln2b, wg, wu, wd, lng, lnb],
        grid=(b, s // FFN_TILE),
        in_specs=[pl.BlockSpec((1, per, D_FOX, ATTN_TILE), lambda b, t: (b, t, 0, 0)),
                  _tile_spec(FFN_TILE, D_RET), _tile_spec(FFN_TILE, d),
                  pl.BlockSpec((1, N_MOD, d), lambda b, t: (b, 0, 0)),
                  _const_spec((D_FOX + D_RET, d)), _const_spec((1, d)), _const_spec((1, d))] + _ffn_weight_specs(),
        out_specs=_tile_spec(FFN_TILE, d),
        out_shape=jax.ShapeDtypeStruct((b, s, d), F32),
        scratch_shapes=[pltpu.VMEM((FFN_TILE, D_FF), BF16)])


def _inproj_kernel(x_ref, mod_ref, wa_ref, wfl_ref, bfl_ref, wr_ref, triu_ref, paugk_ref, paugqt_ref,
                   cos_ref, sin_ref, intra_ref, qdec_ref, kdec_ref, cdec_ref, gng_ref, gnb_ref,
                   qpt_ref, kp_ref, fvt_ref, ret_ref, carry_ref, state_ref):
    tm = TOKEN_TILE
    half = FOX_HEAD_DIM

    @pl.when(pl.program_id(1) == 0)
    def _():
        carry_ref[...] = jnp.zeros_like(carry_ref)
        state_ref[...] = jnp.zeros_like(state_ref)

    sh = mod_ref[0, 3:4, :]
    sc = mod_ref[0, 4:5, :]
    h = (x_ref[0] * (1.0 + sc) + sh).astype(BF16)

    r_all = jnp.dot(h, wr_ref[...], preferred_element_type=F32)
    cos2 = cos_ref[...]
    sin2 = sin_ref[...]
    c = RET_CHUNK
    for hd in range(RET_HEADS):
        lo_, hi_ = hd * RET_HEAD_DIM, (hd + 1) * RET_HEAD_DIM
        tq = r_all[:, lo_:hi_]
        tk = r_all[:, D_RET + lo_:D_RET + hi_]
        q_r = tq * cos2 + pltpu.roll(tq, RET_HEAD_DIM // 2, 1) * sin2
        k_r = (tk * cos2 + pltpu.roll(tk, RET_HEAD_DIM // 2, 1) * sin2) * (RET_HEAD_DIM ** -0.5)
        v = r_all[:, 2 * D_RET + lo_:2 * D_RET + hi_]
        gate = r_all[:, 3 * D_RET + lo_:3 * D_RET + hi_]
        state = state_ref[hd]
        for ci in range(tm // c):
            rows = slice(ci * c, (ci + 1) * c)
            qc = q_r[rows].astype(BF16)
            kc = k_r[rows]
            vc = v[rows].astype(BF16)
            s = lax.dot_general(qc, kc.astype(BF16), NT_DIMS, preferred_element_type=F32) * intra_ref[hd]
            o = (jnp.dot(s.astype(BF16), vc, preferred_element_type=F32)
                 + qdec_ref[hd] * jnp.dot(qc, state.astype(BF16), preferred_element_type=F32))
            state = state * cdec_ref[hd] + lax.dot_general(
                (kc * kdec_ref[hd]).astype(BF16), vc, TN_DIMS, preferred_element_type=F32)
            mu = jnp.mean(o, axis=-1, keepdims=True)
            d = o - mu
            var = jnp.mean(d * d, axis=-1, keepdims=True)
            yn = d * lax.rsqrt(var + LN_EPS) * gng_ref[:, lo_:hi_] + gnb_ref[:, lo_:hi_]
            ret_ref[0, rows, lo_:hi_] = (_silu(gate[rows]) * yn).astype(BF16)
        state_ref[hd] = state

    z = jnp.dot(h, wfl_ref[...], preferred_element_type=F32) + bfl_ref[...]
    logf = jnp.minimum(z, 0.0) - jnp.log1p(jnp.exp(-jnp.abs(z)))
    logf_t = logf.T[0:FOX_HEADS, :]
    terms = _stack_terms(logf_t, jnp.zeros((FOX_HEADS, tm), F32), 4 * FOX_HEADS)
    sums = jnp.dot(terms, triu_ref[...], preferred_element_type=F32)
    cum_t = (sums[0:8] + sums[8:16] + sums[16:24]) + carry_ref[:, 0:1]
    carry_ref[...] = jnp.broadcast_to(cum_t[:, tm - 1:tm], carry_ref.shape)

    row8 = lax.broadcasted_iota(jnp.int32, (FOX_HEADS, tm), 0)
    packed_t = _stack_terms(cum_t * LOG2E, jnp.where(row8 == 0, 1.0, 0.0), LANES)
    aug_k = lax.dot_general(packed_t, paugk_ref[...], TN_DIMS, preferred_element_type=F32)
    aug_qt = jnp.dot(paugqt_ref[...], packed_t, preferred_element_type=F32)

    qk = jnp.dot(h, wa_ref[:, 0:2 * D_FOX], preferred_element_type=F32)
    lane = lax.broadcasted_iota(jnp.int32, (tm, LANES), 1)
    row = lax.broadcasted_iota(jnp.int32, (LANES, tm), 0)
    q_t = [qk[:, pr * LANES:(pr + 1) * LANES].T for pr in range(FOX_HEADS // 2)]
    for hd in range(FOX_HEADS):
        data_lo, aug_lo = _aug_lanes(hd)
        src = (hd // 2) * LANES
        out = slice(hd * LANES, (hd + 1) * LANES)
        is_data = (lane >= data_lo) & (lane < data_lo + half)
        is_aug = (lane >= aug_lo) & (lane < aug_lo + AUG_COLS)
        kp_ref[0, :, out] = jnp.where(is_data, qk[:, D_FOX + src:D_FOX + src + LANES],
                                      jnp.where(is_aug, aug_k, 0.0)).astype(BF16)
        is_data_t = (row >= data_lo) & (row < data_lo + half)
        is_aug_t = (row >= aug_lo) & (row < aug_lo + AUG_COLS)
        qpt_ref[0, 0, out, :] = jnp.where(is_data_t, q_t[hd // 2],
                                          jnp.where(is_aug_t, aug_qt, 0.0)).astype(BF16)

    vt = jnp.dot(h, wa_ref[:, 2 * D_FOX:3 * D_FOX], preferred_element_type=F32).T.astype(BF16)
    for hd in range(FOX_HEADS):
        fvt_ref[0, 0, hd * V_ROWS:hd * V_ROWS + half, :] = vt[hd * half:(hd + 1) * half]
        fvt_ref[0, 0, hd * V_ROWS + half:(hd + 1) * V_ROWS, :] = jnp.ones((BF16_ROWS, tm), BF16)


def _inproj(x1, mod3, wa, wfl, bfl, wr, triu, paugk, paugqt, cos2, sin2, intra, qdec, kdec, cdec, gng, gnb):
    b, s, d = x1.shape
    tm = TOKEN_TILE
    nt = s // tm
    c = RET_CHUNK
    in_specs = [
        _tile_spec(tm, d), pl.BlockSpec((1, N_MOD, d), lambda b, t: (b, 0, 0)),
        _const_spec((d, 3 * D_FOX)), _const_spec((d, LANES)), _const_spec((1, LANES)),
        _const_spec((d, 4 * D_RET)), _const_spec((tm, tm)), _const_spec((LANES, LANES)), _const_spec((LANES, LANES)),
        pl.BlockSpec((tm, LANES), lambda b, t: (t, 0)), pl.BlockSpec((tm, LANES), lambda b, t: (t, 0)),
        _const_spec((RET_HEADS, c, c)), _const_spec((RET_HEADS, c, LANES)), _const_spec((RET_HEADS, c, LANES)),
        _const_spec((RET_HEADS, 1, LANES)), _const_spec((1, D_RET)), _const_spec((1, D_RET)),
    ]
    out_specs = [
        pl.BlockSpec((1, 1, FOX_HEADS * LANES, tm), lambda b, t: (b, t, 0, 0)), _tile_spec(tm, FOX_HEADS * LANES),
        pl.BlockSpec((1, 1, FOX_HEADS * V_ROWS, tm), lambda b, t: (b, t, 0, 0)),
        _tile_spec(tm, D_RET),
    ]
    out_shape = [
        jax.ShapeDtypeStruct((b, nt, FOX_HEADS * LANES, tm), BF16), jax.ShapeDtypeStruct((b, s, FOX_HEADS * LANES), BF16),
        jax.ShapeDtypeStruct((b, nt, FOX_HEADS * V_ROWS, tm), BF16),
        jax.ShapeDtypeStruct((b, s, D_RET), BF16),
    ]
    return _pallas_call(
        _inproj_kernel,
        name="inproj",
        operands=[x1, mod3, wa, wfl, bfl, wr, triu, paugk, paugqt, cos2, sin2, intra, qdec, kdec, cdec, gng, gnb],
        grid=(b, nt),
        in_specs=in_specs, out_specs=out_specs, out_shape=out_shape,
        scratch_shapes=[pltpu.VMEM((FOX_HEADS, LANES), F32),
                        pltpu.VMEM((RET_HEADS, RET_HEAD_DIM, RET_HEAD_DIM), F32)])


def _fox_kernel(qpt_ref, kp_ref, fvt_ref, *rest):
    n_cast = (len(rest) - 6) // 2
    cast_in, o_ref, cast_out = rest[:n_cast], rest[n_cast], rest[n_cast + 1:2 * n_cast + 1]
    s_ref, p_ref, acc_ref, mask_ref, stat_ref = rest[2 * n_cast + 1:]
    for src, dst in zip(cast_in, cast_out):
        dst[...] = src[...].astype(BF16)
    _fox_attention(qpt_ref, kp_ref, fvt_ref, o_ref, s_ref, p_ref, acc_ref, mask_ref, stat_ref)


def _fox_attention(qpt_ref, kp_ref, fvt_ref, o_ref, s_ref, p_ref, acc_ref, mask_ref, stat_ref):
    t = ATTN_TILE
    nq = qpt_ref.shape[1]
    half = FOX_HEAD_DIM

    k_pos = lax.broadcasted_iota(jnp.int32, (t, t), 0)
    q_pos = lax.broadcasted_iota(jnp.int32, (t, t), 1)
    mask_ref[0] = jnp.zeros((t, t), F32)
    mask_ref[1] = jnp.where(k_pos <= q_pos, 0.0, -jnp.inf)
    acc_ref[...] = jnp.zeros_like(acc_ref)
    stat_ref[...] = jnp.zeros_like(stat_ref)

    unroll = ATTN_UNROLL
    row_m, row_alpha_new, row_alpha_old = 1, 1 + unroll, 1 + 2 * unroll
    steps = [(q, k) for q in range(nq) for k in range(q + 1)]
    n_groups = len(steps) // unroll
    assert len(steps) % unroll == 0 and n_groups >= 4 and 1 + 3 * unroll <= stat_ref.shape[1]

    def stat(hh, row):
        return stat_ref[hh, row:row + 1, :]

    def stage_a(hh, group):
        alpha_new = [stat(hh, row_alpha_new + u) for u in range(unroll)]
        m = stat(hh, 0)
        for u, (qa, ka) in enumerate(group):
            static = isinstance(ka, int)
            krows = pl.ds(ka * t, t) if static else pl.ds(pl.multiple_of(ka * t, t), t)
            diag = int(ka == qa) if static else (ka == qa).astype(jnp.int32)
            s = jnp.dot(kp_ref[0, krows, hh * LANES:(hh + 1) * LANES],
                        qpt_ref[0, qa, hh * LANES:(hh + 1) * LANES, :],
                        preferred_element_type=F32) + mask_ref[diag]
            s_ref[u * ATTN_HEADS + hh] = s
            m_prev = jnp.where(ka == 0, -jnp.inf, m)
            m = jnp.maximum(m_prev, jnp.max(s, axis=0, keepdims=True))
            stat_ref[hh, row_m + u:row_m + u + 1, :] = m
            stat_ref[hh, row_alpha_new + u:row_alpha_new + u + 1, :] = jnp.exp2(m_prev - m)
            stat_ref[hh, row_alpha_old + u:row_alpha_old + u + 1, :] = alpha_new[u]
        stat_ref[hh, 0:1, :] = m

    def stage_b(hh):
        for u in range(unroll):
            buf = u * ATTN_HEADS + hh
            p_ref[buf] = jnp.exp2(s_ref[buf] - stat(hh, row_m + u)).astype(BF16)

    def stage_c(hh, group, alpha_row):
        acc = acc_ref[hh]
        for u, (qc, kc) in enumerate(group):
            vt = fvt_ref[0, kc, hh * V_ROWS:(hh + 1) * V_ROWS, :]
            acc = stat(hh, alpha_row + u) * acc + jnp.dot(vt, p_ref[u * ATTN_HEADS + hh],
                                                          preferred_element_type=F32)
            if isinstance(kc, int):
                if kc != qc:
                    continue
                tile = qc
            else:
                tile = jnp.where(kc == qc, qc, nq)
            o_ref[0, tile, hh * half:(hh + 1) * half, :] = (acc[:half] / acc[half:half + 1]).astype(BF16)
        acc_ref[hh] = acc

    def group(g):
        return steps[g * unroll:(g + 1) * unroll]

    def next_step(q, k):
        wrap = k >= q
        return jnp.where(wrap, q + 1, q), jnp.where(wrap, 0, k + 1)

    def body(j, carry):
        a_first, b_group, c_group = carry
        a_group = []
        nxt = a_first
        for _ in range(unroll):
            a_group.append(nxt)
            nxt = next_step(*nxt)
        for hh in range(ATTN_HEADS):
            stage_c(hh, c_group, row_alpha_old)
            stage_b(hh)
            stage_a(hh, a_group)
        return nxt, tuple(a_group), b_group

    as_i32 = lambda grp: tuple((jnp.int32(q), jnp.int32(k)) for q, k in grp)
    for hh in range(ATTN_HEADS):
        stage_a(hh, group(0))
    for hh in range(ATTN_HEADS):
        stage_b(hh)
        stage_a(hh, group(1))
    lax.fori_loop(2, n_groups, body, (as_i32(group(2))[0], as_i32(group(1)), as_i32(group(0))))
    for hh in range(ATTN_HEADS):
        stage_c(hh, group(n_groups - 2), row_alpha_old)
        stage_b(hh)
    for hh in range(ATTN_HEADS):
        stage_c(hh, group(n_groups - 1), row_alpha_new)


def _fox(qpt, kp, fvt, f32_weights):
    b, s, _ = kp.shape
    t = ATTN_TILE
    nt = s // t
    g = ATTN_HEADS
    groups = FOX_HEADS // g
    steps = b * groups
    slab_specs = []
    for w in f32_weights:
        assert w.shape[0] % (steps * BF16_ROWS) == 0
        slab_specs.append(pl.BlockSpec((w.shape[0] // steps, w.shape[1]), lambda b, p: (b * groups + p, 0)))
    return _pallas_call(
        _fox_kernel,
        name="fox", operands=[qpt, kp, fvt, *f32_weights],
        grid=(b, groups),
        in_specs=[pl.BlockSpec((1, nt, g * LANES, t), lambda b, p: (b, 0, p, 0)),
                  pl.BlockSpec((1, s, g * LANES), lambda b, p: (b, 0, p)),
                  pl.BlockSpec((1, nt, g * V_ROWS, t), lambda b, p: (b, 0, p, 0))] + slab_specs,
        out_specs=[pl.BlockSpec((1, nt + 1, g * FOX_HEAD_DIM, t), lambda b, p: (b, 0, p, 0))] + slab_specs,
        out_shape=[jax.ShapeDtypeStruct((b, nt + 1, D_FOX, t), BF16)]
        + [jax.ShapeDtypeStruct(w.shape, BF16) for w in f32_weights],
        scratch_shapes=[pltpu.VMEM((ATTN_UNROLL * g, t, t), F32), pltpu.VMEM((ATTN_UNROLL * g, t, t), BF16),
                        pltpu.VMEM((g, V_ROWS, t), F32), pltpu.VMEM((2, t, t), F32),
                        pltpu.VMEM((g, 8 * pl.cdiv(1 + 3 * ATTN_UNROLL, 8), t), F32)])


def _aug_placement():
    pk = np.zeros((LANES, LANES), np.float32)
    pq = np.zeros((LANES, LANES), np.float32)
    for head in range(FOX_HEADS):
        _, base = _aug_lanes(head)
        for j in range(3):
            pk[24, base + j] = 1.0
            pk[8 * j + head, base + 3 + j] = -1.0
            pq[base + j, 8 * j + head] = 1.0
            pq[base + 3 + j, 24] = 1.0
    return jnp.asarray(pk, BF16), jnp.asarray(pq, BF16)


def _rotary_tables(seq):
    half = RET_HEAD_DIM // 2
    inv_freq = ROPE_BASE ** (-np.arange(half, dtype=np.float64) / half)
    ang = np.arange(seq, dtype=np.float64)[:, None] * inv_freq[None, :]
    cos, sin = np.cos(ang), np.sin(ang)
    return (jnp.asarray(np.concatenate([cos, cos], axis=-1), F32),
            jnp.asarray(np.concatenate([-sin, sin], axis=-1), F32))


def _retention_tables():
    c = RET_CHUNK
    log_gamma = np.log1p(-np.power(2.0, -5.0 - np.arange(RET_HEADS, dtype=np.float64)))
    idx = np.arange(c, dtype=np.float64)
    diff = idx[:, None] - idx[None, :]
    intra = np.where(diff >= 0, np.exp(log_gamma[:, None, None] * np.maximum(diff, 0.0)), 0.0)
    qdec = np.exp(log_gamma[:, None] * (idx + 1.0))[..., None]
    kdec = np.exp(log_gamma[:, None] * (c - 1.0 - idx))[..., None]
    cdec = np.exp(log_gamma * c)[:, None, None]
    bc = lambda a: jnp.asarray(np.broadcast_to(a, a.shape[:2] + (LANES,)), F32)
    return jnp.asarray(intra, F32), bc(qdec), bc(kdec), bc(cdec)


def kernel(x, c, w_ada, b_ada, ffn1_w_gate, ffn1_w_up, ffn1_w_down, ln1_g, ln1_b, w_in, fox_b_f, ret_gn_g,
           ret_gn_b, w_o, ln2_g, ln2_b, ffn2_w_gate, ffn2_w_up, ffn2_w_down, ln3_g, ln3_b):
    batch, seq, d = x.shape
    assert d == D_MODEL and seq % FFN_TILE == 0 and FFN_TILE % TOKEN_TILE == 0 and w_ada.shape[0] == DEPTH
    assert sum(FFN_GROUPS) == FFN_TILE and all(
        _ffn_group_rows(i).start // ATTN_TILE == (_ffn_group_rows(i).stop - 1) // ATTN_TILE
        for i in range(len(FFN_GROUPS)))
    layer = 0

    c_pad = jnp.pad(c, ((0, 8 - batch), (0, 0)))
    mod3 = _ada(c_pad, w_ada[layer], b_ada[layer][None, :])[:batch].reshape(batch, N_MOD, d)

    w = w_in[layer]
    o_fl = 3 * D_FOX
    o_r = o_fl + FOX_HEADS
    q_scale = LOG2E * FOX_HEAD_DIM ** -0.5
    x1, wa, wr = _ffn(x, mod3, ffn1_w_gate[layer], ffn1_w_up[layer], ffn1_w_down[layer],
                      ln1_g[layer][None, :], ln1_b[layer][None, :], 0, w, q_scale)

    wfl = jnp.pad(w[:, o_fl:o_r], ((0, 0), (0, LANES - FOX_HEADS))).astype(BF16)
    bfl = jnp.pad(fox_b_f[layer], (0, LANES - FOX_HEADS))[None, :]
    triu = jnp.asarray(np.triu(np.ones((TOKEN_TILE, TOKEN_TILE), np.float32)), BF16)
    paugk, paugqt = _aug_placement()
    cos2, sin2 = _rotary_tables(seq)
    intra, qdec, kdec, cdec = _retention_tables()
    qpt, kp, fvt, ret = _inproj(x1, mod3, wa, wfl, bfl, wr, triu, paugk, paugqt, cos2, sin2,
                               intra, qdec, kdec, cdec, ret_gn_g[layer][None, :], ret_gn_b[layer][None, :])

    foxt, wo, wg2, wu2, wd2 = _fox(qpt, kp, fvt, [w_o[layer], ffn2_w_gate[layer], ffn2_w_up[layer],
                                                  ffn2_w_down[layer]])

    return _mix_ffn(foxt, ret, x1, mod3, wo, ln2_g[layer][None, :], ln2_b[layer][None, :],
                    wg2, wu2, wd2, ln3_g[layer][None, :], ln3_b[layer][None, :])
```

```python
import functools
import math

import numpy as np
import jax
import jax.numpy as jnp
from jax import lax
from jax.experimental import pallas as pl
from jax.experimental.pallas import tpu as pltpu

F32 = jnp.float32
BF16 = jnp.bfloat16

D_MODEL = 1024
D_FF = 2816
D_FOX = 512
D_RET = 512
FOX_HEADS = 8
FOX_HEAD_DIM = 64
RET_HEADS = 4
RET_HEAD_DIM = 128
N_MOD = 9
ROPE_BASE = 10000.0
LN_EPS = 1e-5
DEPTH = 1
DEEPNORM_ALPHA = (2.0 * DEPTH) ** 0.25
FFN_RES_WEIGHT = 0.5
LOG2E = math.log2(math.e)

LANES = 128
BF16_ROWS = 16
ADA_COLS = 2304
TOKEN_TILE = 512
FFN_TILE = 1024
FFN_GROUPS = (256, 256, 256, 256)
FF_CHUNK = 256
RET_CHUNK = 256
ATTN_TILE = TOKEN_TILE
ATTN_HEADS = 2
ATTN_UNROLL = 2
AUG_COLS = 6
V_ROWS = FOX_HEAD_DIM + BF16_ROWS
V7X_VMEM_BYTES = 64 * 2 ** 20
VMEM_LIMIT_BYTES = V7X_VMEM_BYTES * 7 // 8

NT_DIMS = (((1,), (1,)), ((), ()))
TN_DIMS = (((0,), (0,)), ((), ()))


def _silu(v):
    return v / (1.0 + jnp.exp(-v))


def _layer_norm(y, g, b):
    mu = jnp.mean(y, axis=-1, keepdims=True)
    d = y - mu
    var = jnp.mean(d * d, axis=-1, keepdims=True)
    return d * lax.rsqrt(var + LN_EPS) * g + b


def _split3(v):
    hi = v.astype(BF16)
    r = v - hi.astype(F32)
    mid = r.astype(BF16)
    lo = (r - mid.astype(F32)).astype(BF16)
    return hi, mid, lo


def _stack_terms(v, tail, rows):
    hi, mid, lo = _split3(v)
    parts = [hi.astype(F32), mid.astype(F32), lo.astype(F32), tail]
    if rows > 32:
        parts.append(jnp.zeros((rows - 32, v.shape[1]), F32))
    return jnp.concatenate(parts, axis=0).astype(BF16)


def _const_spec(shape):
    zeros = (0,) * len(shape)
    return pl.BlockSpec(shape, lambda *_: zeros, pipeline_mode=pl.Buffered(1))


def _pallas_call(kernel, *, name, grid, in_specs, out_specs, out_shape, scratch_shapes, operands):
    out_specs_l = out_specs if isinstance(out_specs, (list, tuple)) else [out_specs]
    out_shape_l = out_shape if isinstance(out_shape, (list, tuple)) else [out_shape]

    def window(spec, dtype):
        single = spec.pipeline_mode is not None and spec.pipeline_mode.buffer_count == 1
        return (1 if single else 2) * math.prod(spec.block_shape) * jnp.dtype(dtype).itemsize

    declared = (sum(window(sp, op.dtype) for sp, op in zip(in_specs, operands))
                + sum(window(sp, sh.dtype) for sp, sh in zip(out_specs_l, out_shape_l))
                + sum(math.prod(sc.shape) * jnp.dtype(sc.dtype).itemsize for sc in scratch_shapes))
    assert declared <= VMEM_LIMIT_BYTES, (name, declared)
    return pl.pallas_call(
        kernel, grid=grid, in_specs=in_specs, out_specs=out_specs, out_shape=out_shape,
        scratch_shapes=scratch_shapes,
        compiler_params=pltpu.CompilerParams(dimension_semantics=("arbitrary",) * len(grid),
                                             vmem_limit_bytes=VMEM_LIMIT_BYTES),
        name=name,
    )(*operands)


def _aug_lanes(head):
    half = FOX_HEAD_DIM
    pair, odd = divmod(head, 2)
    data_lo = half if odd else 0
    aug_lo = (0 if odd else half) + AUG_COLS * pair
    return data_lo, aug_lo


def _ada_kernel(c_ref, w_ref, b_ref, o_ref):
    ca = _silu(c_ref[...]).astype(BF16)
    o_ref[...] = jnp.dot(ca, w_ref[...].astype(BF16), preferred_element_type=F32) + b_ref[...]


def _ada(c_pad, w, b):
    rows, d = c_pad.shape
    n = w.shape[1]
    tn = ADA_COLS
    assert n % tn == 0
    return _pallas_call(
        _ada_kernel,
        name="ada", operands=[c_pad, w, b],
        grid=(n // tn,),
        in_specs=[pl.BlockSpec((rows, d), lambda j: (0, 0)),
                  pl.BlockSpec((d, tn), lambda j: (0, j)),
                  pl.BlockSpec((1, tn), lambda j: (0, j))],
        out_specs=pl.BlockSpec((rows, tn), lambda j: (0, j)),
        out_shape=jax.ShapeDtypeStruct((rows, n), F32),
        scratch_shapes=[])


def _ffn_group_rows(i):
    start = sum(FFN_GROUPS[:i])
    return slice(start, start + FFN_GROUPS[i])


def _swiglu_ln_tile(load_x, store_out, mod_ref, mod_base, gate_w, up_w, down_w, lng_ref, lnb_ref, act_ref):
    n_sub = len(FFN_GROUPS)
    n_chunks = D_FF // FF_CHUNK
    lead_finish, lead_next = 2, 6
    sh = mod_ref[0, mod_base:mod_base + 1, :]
    sc = mod_ref[0, mod_base + 1:mod_base + 2, :]
    g = mod_ref[0, mod_base + 2:mod_base + 3, :]
    xs, hs = {}, {}
    rows_of = _ffn_group_rows

    def begin(i):
        xs[i] = load_x(i)
        hs[i] = (xs[i] * (1.0 + sc) + sh).astype(BF16)

    def chunks(i, first, last):
        for j in range(first, last):
            cols = slice(j * FF_CHUNK, (j + 1) * FF_CHUNK)
            gate = jnp.dot(hs[i], gate_w(j), preferred_element_type=F32)
            up = jnp.dot(hs[i], up_w(j), preferred_element_type=F32)
            act_ref[rows_of(i), cols] = (_silu(gate) * up).astype(BF16)

    def finish(i):
        f = jnp.dot(act_ref[rows_of(i), :], down_w(), preferred_element_type=F32)
        y = DEEPNORM_ALPHA * xs.pop(i) + (FFN_RES_WEIGHT * g) * f
        store_out(i, _layer_norm(y, lng_ref[...], lnb_ref[...]))

    begin(0)
    for i in range(n_sub):
        chunks(i, 0, lead_finish)
        if i > 0:
            finish(i - 1)
        chunks(i, lead_finish, lead_next)
        if i + 1 < n_sub:
            begin(i + 1)
        chunks(i, lead_next, n_chunks)
    finish(n_sub - 1)


def _resident_ffn_weights(wg_ref, wu_ref, wd_ref):
    def cols(j):
        return slice(j * FF_CHUNK, (j + 1) * FF_CHUNK)
    return (lambda j: wg_ref[:, cols(j)]), (lambda j: wu_ref[:, cols(j)]), (lambda: wd_ref[...])


def _ffn_kernel(x_ref, mod_ref, wgc_ref, wuc_ref, wdc_ref, lng_ref, lnb_ref, win_ref,
                o_ref, wa_ref, wr_ref, act_ref, wg_s, wu_s, wd_s, *, mod_base, q_scale):
    step = pl.program_id(0)
    n_chunks = D_FF // FF_CHUNK

    @pl.when(step < n_chunks)
    def _():
        wg_s[step] = wgc_ref[...].astype(BF16)
        wu_s[step] = wuc_ref[...].astype(BF16)
        wd_s[step] = wdc_ref[...].astype(BF16)

    @pl.when(step >= n_chunks)
    def _():
        o_fl = 3 * D_FOX
        o_r = o_fl + FOX_HEADS
        w = win_ref[...]
        wa_ref[:, 0:D_FOX] = (w[:, 0:D_FOX] * q_scale).astype(BF16)
        wa_ref[:, D_FOX:o_fl] = w[:, D_FOX:o_fl].astype(BF16)
        wr_ref[...] = w[:, o_r:o_r + 4 * D_RET].astype(BF16)
        rows_of = _ffn_group_rows

        def load_x(i):
            return x_ref[0, rows_of(i)]

        def store_out(i, value):
            o_ref[0, rows_of(i)] = value

        _swiglu_ln_tile(load_x, store_out, mod_ref, mod_base, lambda j: wg_s[j], lambda j: wu_s[j],
                        lambda: wd_s[...].reshape(D_FF, D_MODEL), lng_ref, lnb_ref, act_ref)


def _mix_ffn_kernel(foxt_ref, ret_ref, x_ref, mod_ref, wo_ref, ln2g_ref, ln2b_ref,
                    wg_ref, wu_ref, wd_ref, lng_ref, lnb_ref, o_ref, act_ref):
    g2 = mod_ref[0, 5:6, :]
    rows_of = _ffn_group_rows

    def load_x(i):
        tile, part = divmod(rows_of(i).start, ATTN_TILE)
        fox_t = foxt_ref[0, tile, :, part:part + FFN_GROUPS[i]]
        m = (lax.dot_general(fox_t, wo_ref[0:D_FOX, :], TN_DIMS, preferred_element_type=F32)
             + jnp.dot(ret_ref[0, rows_of(i)], wo_ref[D_FOX:D_FOX + D_RET, :], preferred_element_type=F32))
        return _layer_norm(DEEPNORM_ALPHA * x_ref[0, rows_of(i)] + g2 * m, ln2g_ref[...], ln2b_ref[...])

    def store_out(i, value):
        o_ref[0, rows_of(i)] = value

    _swiglu_ln_tile(load_x, store_out, mod_ref, 6, *_resident_ffn_weights(wg_ref, wu_ref, wd_ref),
                    lng_ref, lnb_ref, act_ref)


def _tile_spec(tile, width):
    return pl.BlockSpec((1, tile, width), lambda b, t: (b, t, 0))


def _ffn_weight_specs():
    return [_const_spec((D_MODEL, D_FF)), _const_spec((D_MODEL, D_FF)), _const_spec((D_FF, D_MODEL)),
            _const_spec((1, D_MODEL)), _const_spec((1, D_MODEL))]


def _ffn(x, mod3, wg, wu, wd, lng, lnb, mod_base, w_in, q_scale):
    b, s, d = x.shape
    tiles = s // FFN_TILE
    steps = b * tiles
    rows = d // steps
    n_chunks = D_FF // FF_CHUNK
    assert d % (steps * BF16_ROWS) == 0

    def tile_of(i):
        return jnp.maximum(i - n_chunks, 0)

    def chunk_of(i):
        return jnp.minimum(i, n_chunks - 1)

    def slab(width):
        return pl.BlockSpec((rows, width), lambda i: (tile_of(i), 0))

    tile_spec = pl.BlockSpec((1, FFN_TILE, d), lambda i: (tile_of(i) // tiles, tile_of(i) % tiles, 0))
    return _pallas_call(
        functools.partial(_ffn_kernel, mod_base=mod_base, q_scale=q_scale),
        name="ffn1", operands=[x, mod3, wg, wu, wd, lng, lnb, w_in],
        grid=(n_chunks + steps,),
        in_specs=[tile_spec, pl.BlockSpec((1, N_MOD, d), lambda i: (tile_of(i) // tiles, 0, 0)),
                  pl.BlockSpec((d, FF_CHUNK), lambda i: (0, chunk_of(i))),
                  pl.BlockSpec((d, FF_CHUNK), lambda i: (0, chunk_of(i))),
                  pl.BlockSpec((FF_CHUNK, d), lambda i: (chunk_of(i), 0)),
                  _const_spec((1, d)), _const_spec((1, d)), slab(w_in.shape[1])],
        out_specs=[tile_spec, slab(3 * D_FOX), slab(4 * D_RET)],
        out_shape=[jax.ShapeDtypeStruct((b, s, d), F32), jax.ShapeDtypeStruct((d, 3 * D_FOX), BF16),
                   jax.ShapeDtypeStruct((d, 4 * D_RET), BF16)],
        scratch_shapes=[pltpu.VMEM((FFN_TILE, D_FF), BF16),
                        pltpu.VMEM((n_chunks, d, FF_CHUNK), BF16), pltpu.VMEM((n_chunks, d, FF_CHUNK), BF16),
                        pltpu.VMEM((n_chunks, FF_CHUNK, d), BF16)])


def _mix_ffn(foxt, ret, x1, mod3, wo, ln2g, ln2b, wg, wu, wd, lng, lnb):
    b, s, d = x1.shape
    per = FFN_TILE // ATTN_TILE
    return _pallas_call(
        _mix_ffn_kernel,
        name="ffn2", operands=[foxt, ret, x1, mod3, wo, ln2g, ln2b, wg, wu, wd, lng, lnb],
        grid=(b, s // FFN_TILE),
        in_specs=[pl.BlockSpec((1, per, D_FOX, ATTN_TILE), lambda b, t: (b, t, 0, 0)),
                  _tile_spec(FFN_TILE, D_RET), _tile_spec(FFN_TILE, d),
                  pl.BlockSpec((1, N_MOD, d), lambda b, t: (b, 0, 0)),
                  _const_spec((D_FOX + D_RET, d)), _const_spec((1, d)), _const_spec((1, d))] + _ffn_weight_specs(),
        out_specs=_tile_spec(FFN_TILE, d),
        out_shape=jax.ShapeDtypeStruct((b, s, d), F32),
        scratch_shapes=[pltpu.VMEM((FFN_TILE, D_FF), BF16)])


def _inproj_kernel(x_ref, mod_ref, wa_ref, wfl_ref, bfl_ref, wr_ref, triu_ref, paugk_ref, paugqt_ref,
                   cos_ref, sin_ref, intra_ref, qdec_ref, kdec_ref, cdec_ref, gng_ref, gnb_ref,
                   qpt_ref, kp_ref, fvt_ref, ret_ref, carry_ref, state_ref):
    tm = TOKEN_TILE
    half = FOX_HEAD_DIM

    @pl.when(pl.program_id(1) == 0)
    def _():
        carry_ref[...] = jnp.zeros_like(carry_ref)
        state_ref[...] = jnp.zeros_like(state_ref)

    sh = mod_ref[0, 3:4, :]
    sc = mod_ref[0, 4:5, :]
    h = (x_ref[0] * (1.0 + sc) + sh).astype(BF16)

    r_all = jnp.dot(h, wr_ref[...], preferred_element_type=F32)
    cos2 = cos_ref[...]
    sin2 = sin_ref[...]
    c = RET_CHUNK
    for hd in range(RET_HEADS):
        lo_, hi_ = hd * RET_HEAD_DIM, (hd + 1) * RET_HEAD_DIM
        tq = r_all[:, lo_:hi_]
        tk = r_all[:, D_RET + lo_:D_RET + hi_]
        q_r = tq * cos2 + pltpu.roll(tq, RET_HEAD_DIM // 2, 1) * sin2
        k_r = (tk * cos2 + pltpu.roll(tk, RET_HEAD_DIM // 2, 1) * sin2) * (RET_HEAD_DIM ** -0.5)
        v = r_all[:, 2 * D_RET + lo_:2 * D_RET + hi_]
        gate = r_all[:, 3 * D_RET + lo_:3 * D_RET + hi_]
        state = state_ref[hd]
        for ci in range(tm // c):
            rows = slice(ci * c, (ci + 1) * c)
            qc = q_r[rows].astype(BF16)
            kc = k_r[rows]
            vc = v[rows].astype(BF16)
            s = lax.dot_general(qc, kc.astype(BF16), NT_DIMS, preferred_element_type=F32) * intra_ref[hd]
            o = (jnp.dot(s.astype(BF16), vc, preferred_element_type=F32)
                 + qdec_ref[hd] * jnp.dot(qc, state.astype(BF16), preferred_element_type=F32))
            state = state * cdec_ref[hd] + lax.dot_general(
                (kc * kdec_ref[hd]).astype(BF16), vc, TN_DIMS, preferred_element_type=F32)
            mu = jnp.mean(o, axis=-1, keepdims=True)
            d = o - mu
            var = jnp.mean(d * d, axis=-1, keepdims=True)
            yn = d * lax.rsqrt(var + LN_EPS) * gng_ref[:, lo_:hi_] + gnb_ref[:, lo_:hi_]
            ret_ref[0, rows, lo_:hi_] = (_silu(gate[rows]) * yn).astype(BF16)
        state_ref[hd] = state

    z = jnp.dot(h, wfl_ref[...], preferred_element_type=F32) + bfl_ref[...]
    logf = jnp.minimum(z, 0.0) - jnp.log1p(jnp.exp(-jnp.abs(z)))
    logf_t = logf.T[0:FOX_HEADS, :]
    terms = _stack_terms(logf_t, jnp.zeros((FOX_HEADS, tm), F32), 4 * FOX_HEADS)
    sums = jnp.dot(terms, triu_ref[...], preferred_element_type=F32)
    cum_t = (sums[0:8] + sums[8:16] + sums[16:24]) + carry_ref[:, 0:1]
    carry_ref[...] = jnp.broadcast_to(cum_t[:, tm - 1:tm], carry_ref.shape)

    row8 = lax.broadcasted_iota(jnp.int32, (FOX_HEADS, tm), 0)
    packed_t = _stack_terms(cum_t * LOG2E, jnp.where(row8 == 0, 1.0, 0.0), LANES)
    aug_k = lax.dot_general(packed_t, paugk_ref[...], TN_DIMS, preferred_element_type=F32)
    aug_qt = jnp.dot(paugqt_ref[...], packed_t, preferred_element_type=F32)

    qk = jnp.dot(h, wa_ref[:, 0:2 * D_FOX], preferred_element_type=F32)
    lane = lax.broadcasted_iota(jnp.int32, (tm, LANES), 1)
    row = lax.broadcasted_iota(jnp.int32, (LANES, tm), 0)
    q_t = [qk[:, pr * LANES:(pr + 1) * LANES].T for pr in range(FOX_HEADS // 2)]
    for hd in range(FOX_HEADS):
        data_lo, aug_lo = _aug_lanes(hd)
        src = (hd // 2) * LANES
        out = slice(hd * LANES, (hd + 1) * LANES)
        is_data = (lane >= data_lo) & (lane < data_lo + half)
        is_aug = (lane >= aug_lo) & (lane < aug_lo + AUG_COLS)
        kp_ref[0, :, out] = jnp.where(is_data, qk[:, D_FOX + src:D_FOX + src + LANES],
                                      jnp.where(is_aug, aug_k, 0.0)).astype(BF16)
        is_data_t = (row >= data_lo) & (row < data_lo + half)
        is_aug_t = (row >= aug_lo) & (row < aug_lo + AUG_COLS)
        qpt_ref[0, 0, out, :] = jnp.where(is_data_t, q_t[hd // 2],
                                          jnp.where(is_aug_t, aug_qt, 0.0)).astype(BF16)

    vt = jnp.dot(h, wa_ref[:, 2 * D_FOX:3 * D_FOX], preferred_element_type=F32).T.astype(BF16)
    for hd in range(FOX_HEADS):
        fvt_ref[0, 0, hd * V_ROWS:hd * V_ROWS + half, :] = vt[hd * half:(hd + 1) * half]
        fvt_ref[0, 0, hd * V_ROWS + half:(hd + 1) * V_ROWS, :] = jnp.ones((BF16_ROWS, tm), BF16)


def _inproj(x1, mod3, wa, wfl, bfl, wr, triu, paugk, paugqt, cos2, sin2, intra, qdec, kdec, cdec, gng, gnb):
    b, s, d = x1.shape
    tm = TOKEN_TILE
    nt = s // tm
    c = RET_CHUNK
    in_specs = [
        _tile_spec(tm, d), pl.BlockSpec((1, N_MOD, d), lambda b, t: (b, 0, 0)),
        _const_spec((d, 3 * D_FOX)), _const_spec((d, LANES)), _const_spec((1, LANES)),
        _const_spec((d, 4 * D_RET)), _const_spec((tm, tm)), _const_spec((LANES, LANES)), _const_spec((LANES, LANES)),
        pl.BlockSpec((tm, LANES), lambda b, t: (t, 0)), pl.BlockSpec((tm, LANES), lambda b, t: (t, 0)),
        _const_spec((RET_HEADS, c, c)), _const_spec((RET_HEADS, c, LANES)), _const_spec((RET_HEADS, c, LANES)),
        _const_spec((RET_HEADS, 1, LANES)), _const_spec((1, D_RET)), _const_spec((1, D_RET)),
    ]
    out_specs = [
        pl.BlockSpec((1, 1, FOX_HEADS * LANES, tm), lambda b, t: (b, t, 0, 0)), _tile_spec(tm, FOX_HEADS * LANES),
        pl.BlockSpec((1, 1, FOX_HEADS * V_ROWS, tm), lambda b, t: (b, t, 0, 0)),
        _tile_spec(tm, D_RET),
    ]
    out_shape = [
        jax.ShapeDtypeStruct((b, nt, FOX_HEADS * LANES, tm), BF16), jax.ShapeDtypeStruct((b, s, FOX_HEADS * LANES), BF16),
        jax.ShapeDtypeStruct((b, nt, FOX_HEADS * V_ROWS, tm), BF16),
        jax.ShapeDtypeStruct((b, s, D_RET), BF16),
    ]
    return _pallas_call(
        _inproj_kernel,
        name="inproj",
        operands=[x1, mod3, wa, wfl, bfl, wr, triu, paugk, paugqt, cos2, sin2, intra, qdec, kdec, cdec, gng, gnb],
        grid=(b, nt),
        in_specs=in_specs, out_specs=out_specs, out_shape=out_shape,
        scratch_shapes=[pltpu.VMEM((FOX_HEADS, LANES), F32),
                        pltpu.VMEM((RET_HEADS, RET_HEAD_DIM, RET_HEAD_DIM), F32)])


def _fox_kernel(qpt_ref, kp_ref, fvt_ref, *rest):
    n_cast = (len(rest) - 6) // 2
    cast_in, o_ref, cast_out = rest[:n_cast], rest[n_cast], rest[n_cast + 1:2 * n_cast + 1]
    s_ref, p_ref, acc_ref, mask_ref, stat_ref = rest[2 * n_cast + 1:]
    for src, dst in zip(cast_in, cast_out):
        dst[...] = src[...].astype(BF16)
    _fox_attention(qpt_ref, kp_ref, fvt_ref, o_ref, s_ref, p_ref, acc_ref, mask_ref, stat_ref)


def _fox_attention(qpt_ref, kp_ref, fvt_ref, o_ref, s_ref, p_ref, acc_ref, mask_ref, stat_ref):
    t = ATTN_TILE
    nq = qpt_ref.shape[1]
    half = FOX_HEAD_DIM

    k_pos = lax.broadcasted_iota(jnp.int32, (t, t), 0)
    q_pos = lax.broadcasted_iota(jnp.int32, (t, t), 1)
    mask_ref[0] = jnp.zeros((t, t), F32)
    mask_ref[1] = jnp.where(k_pos <= q_pos, 0.0, -jnp.inf)
    acc_ref[...] = jnp.zeros_like(acc_ref)
    stat_ref[...] = jnp.zeros_like(stat_ref)

    unroll = ATTN_UNROLL
    row_m, row_alpha_new, row_alpha_old = 1, 1 + unroll, 1 + 2 * unroll
    steps = [(q, k) for q in range(nq) for k in range(q + 1)]
    n_groups = len(steps) // unroll
    assert len(steps) % unroll == 0 and n_groups >= 4 and 1 + 3 * unroll <= stat_ref.shape[1]

    def stat(hh, row):
        return stat_ref[hh, row:row + 1, :]

    def stage_a(hh, group):
        alpha_new = [stat(hh, row_alpha_new + u) for u in range(unroll)]
        m = stat(hh, 0)
        for u, (qa, ka) in enumerate(group):
            static = isinstance(ka, int)
            krows = pl.ds(ka * t, t) if static else pl.ds(pl.multiple_of(ka * t, t), t)
            diag = int(ka == qa) if static else (ka == qa).astype(jnp.int32)
            s = jnp.dot(kp_ref[0, krows, hh * LANES:(hh + 1) * LANES],
                        qpt_ref[0, qa, hh * LANES:(hh + 1) * LANES, :],
                        preferred_element_type=F32) + mask_ref[diag]
            s_ref[u * ATTN_HEADS + hh] = s
            m_prev = jnp.where(ka == 0, -jnp.inf, m)
            m = jnp.maximum(m_prev, jnp.max(s, axis=0, keepdims=True))
            stat_ref[hh, row_m + u:row_m + u + 1, :] = m
            stat_ref[hh, row_alpha_new + u:row_alpha_new + u + 1, :] = jnp.exp2(m_prev - m)
            stat_ref[hh, row_alpha_old + u:row_alpha_old + u + 1, :] = alpha_new[u]
        stat_ref[hh, 0:1, :] = m

    def stage_b(hh):
        for u in range(unroll):
            buf = u * ATTN_HEADS + hh
            p_ref[buf] = jnp.exp2(s_ref[buf] - stat(hh, row_m + u)).astype(BF16)

    def stage_c(hh, group, alpha_row):
        acc = acc_ref[hh]
        for u, (qc, kc) in enumerate(group):
            vt = fvt_ref[0, kc, hh * V_ROWS:(hh + 1) * V_ROWS, :]
            acc = stat(hh, alpha_row + u) * acc + jnp.dot(vt, p_ref[u * ATTN_HEADS + hh],
                                                          preferred_element_type=F32)
            if isinstance(kc, int):
                if kc != qc:
                    continue
                tile = qc
            else:
                tile = jnp.where(kc == qc, qc, nq)
            o_ref[0, tile, hh * half:(hh + 1) * half, :] = (acc[:half] / acc[half:half + 1]).astype(BF16)
        acc_ref[hh] = acc

    def group(g):
        return steps[g * unroll:(g + 1) * unroll]

    def next_step(q, k):
        wrap = k >= q
        return jnp.where(wrap, q + 1, q), jnp.where(wrap, 0, k + 1)

    def body(j, carry):
        a_first, b_group, c_group = carry
        a_group = []
        nxt = a_first
        for _ in range(unroll):
            a_group.append(nxt)
            nxt = next_step(*nxt)
        for hh in range(ATTN_HEADS):
            stage_c(hh, c_group, row_alpha_old)
            stage_b(hh)
            stage_a(hh, a_group)
        return nxt, tuple(a_group), b_group

    as_i32 = lambda grp: tuple((jnp.int32(q), jnp.int32(k)) for q, k in grp)
    for hh in range(ATTN_HEADS):
        stage_a(hh, group(0))
    for hh in range(ATTN_HEADS):
        stage_b(hh)
        stage_a(hh, group(1))
    lax.fori_loop(2, n_groups, body, (as_i32(group(2))[0], as_i32(group(1)), as_i32(group(0))))
    for hh in range(ATTN_HEADS):
        stage_c(hh, group(n_groups - 2), row_alpha_old)
        stage_b(hh)
    for hh in range(ATTN_HEADS):
        stage_c(hh, group(n_groups - 1), row_alpha_new)


def _fox(qpt, kp, fvt, f32_weights):
    b, s, _ = kp.shape
    t = ATTN_TILE
    nt = s // t
    g = ATTN_HEADS
    groups = FOX_HEADS // g
    steps = b * groups
    slab_specs = []
    for w in f32_weights:
        assert w.shape[0] % (steps * BF16_ROWS) == 0
        slab_specs.append(pl.BlockSpec((w.shape[0] // steps, w.shape[1]), lambda b, p: (b * groups + p, 0)))
    return _pallas_call(
        _fox_kernel,
        name="fox", operands=[qpt, kp, fvt, *f32_weights],
        grid=(b, groups),
        in_specs=[pl.BlockSpec((1, nt, g * LANES, t), lambda b, p: (b, 0, p, 0)),
                  pl.BlockSpec((1, s, g * LANES), lambda b, p: (b, 0, p)),
                  pl.BlockSpec((1, nt, g * V_ROWS, t), lambda b, p: (b, 0, p, 0))] + slab_specs,
        out_specs=[pl.BlockSpec((1, nt + 1, g * FOX_HEAD_DIM, t), lambda b, p: (b, 0, p, 0))] + slab_specs,
        out_shape=[jax.ShapeDtypeStruct((b, nt + 1, D_FOX, t), BF16)]
        + [jax.ShapeDtypeStruct(w.shape, BF16) for w in f32_weights],
        scratch_shapes=[pltpu.VMEM((ATTN_UNROLL * g, t, t), F32), pltpu.VMEM((ATTN_UNROLL * g, t, t), BF16),
                        pltpu.VMEM((g, V_ROWS, t), F32), pltpu.VMEM((2, t, t), F32),
                        pltpu.VMEM((g, 8 * pl.cdiv(1 + 3 * ATTN_UNROLL, 8), t), F32)])


def _aug_placement():
    pk = np.zeros((LANES, LANES), np.float32)
    pq = np.zeros((LANES, LANES), np.float32)
    for head in range(FOX_HEADS):
        _, base = _aug_lanes(head)
        for j in range(3):
            pk[24, base + j] = 1.0
            pk[8 * j + head, base + 3 + j] = -1.0
            pq[base + j, 8 * j + head] = 1.0
            pq[base + 3 + j, 24] = 1.0
    return jnp.asarray(pk, BF16), jnp.asarray(pq, BF16)


def _rotary_tables(seq):
    half = RET_HEAD_DIM // 2
    inv_freq = ROPE_BASE ** (-np.arange(half, dtype=np.float64) / half)
    ang = np.arange(seq, dtype=np.float64)[:, None] * inv_freq[None, :]
    cos, sin = np.cos(ang), np.sin(ang)
    return (jnp.asarray(np.concatenate([cos, cos], axis=-1), F32),
            jnp.asarray(np.concatenate([-sin, sin], axis=-1), F32))


def _retention_tables():
    c = RET_CHUNK
    log_gamma = np.log1p(-np.power(2.0, -5.0 - np.arange(RET_HEADS, dtype=np.float64)))
    idx = np.arange(c, dtype=np.float64)
    diff = idx[:, None] - idx[None, :]
    intra = np.where(diff >= 0, np.exp(log_gamma[:, None, None] * np.maximum(diff, 0.0)), 0.0)
    qdec = np.exp(log_gamma[:, None] * (idx + 1.0))[..., None]
    kdec = np.exp(log_gamma[:, None] * (c - 1.0 - idx))[..., None]
    cdec = np.exp(log_gamma * c)[:, None, None]
    bc = lambda a: jnp.asarray(np.broadcast_to(a, a.shape[:2] + (LANES,)), F32)
    return jnp.asarray(intra, F32), bc(qdec), bc(kdec), bc(cdec)


def kernel(x, c, w_ada, b_ada, ffn1_w_gate, ffn1_w_up, ffn1_w_down, ln1_g, ln1_b, w_in, fox_b_f, ret_gn_g,
           ret_gn_b, w_o, ln2_g, ln2_b, ffn2_w_gate, ffn2_w_up, ffn2_w_down, ln3_g, ln3_b):
    batch, seq, d = x.shape
    assert d == D_MODEL and seq % FFN_TILE == 0 and FFN_TILE % TOKEN_TILE == 0 and w_ada.shape[0] == DEPTH
    assert sum(FFN_GROUPS) == FFN_TILE and all(
        _ffn_group_rows(i).start // ATTN_TILE == (_ffn_group_rows(i).stop - 1) // ATTN_TILE
        for i in range(len(FFN_GROUPS)))
    layer = 0

    c_pad = jnp.pad(c, ((0, 8 - batch), (0, 0)))
    mod3 = _ada(c_pad, w_ada[layer], b_ada[layer][None, :])[:batch].reshape(batch, N_MOD, d)

    w = w_in[layer]
    o_fl = 3 * D_FOX
    o_r = o_fl + FOX_HEADS
    q_scale = LOG2E * FOX_HEAD_DIM ** -0.5
    x1, wa, wr = _ffn(x, mod3, ffn1_w_gate[layer], ffn1_w_up[layer], ffn1_w_down[layer],
                      ln1_g[layer][None, :], ln1_b[layer][None, :], 0, w, q_scale)

    wfl = jnp.pad(w[:, o_fl:o_r], ((0, 0), (0, LANES - FOX_HEADS))).astype(BF16)
    bfl = jnp.pad(fox_b_f[layer], (0, LANES - FOX_HEADS))[None, :]
    triu = jnp.asarray(np.triu(np.ones((TOKEN_TILE, TOKEN_TILE), np.float32)), BF16)
    paugk, paugqt = _aug_placement()
    cos2, sin2 = _rotary_tables(seq)
    intra, qdec, kdec, cdec = _retention_tables()
    qpt, kp, fvt, ret = _inproj(x1, mod3, wa, wfl, bfl, wr, triu, paugk, paugqt, cos2, sin2,
                               intra, qdec, kdec, cdec, ret_gn_g[layer][None, :], ret_gn_b[layer][None, :])

    foxt, wo, wg2, wu2, wd2 = _fox(qpt, kp, fvt, [w_o[layer], ffn2_w_gate[layer], ffn2_w_up[layer],
                                                  ffn2_w_down[layer]])

    return _mix_ffn(foxt, ret, x1, mod3, wo, ln2_g[layer][None, :], ln2_b[layer][None, :],
                    wg2, wu2, wd2, ln3_g[layer][None, :], ln3_b[layer][None, :])
```

```python
import functools
import math

import numpy as np
import jax
import jax.numpy as jnp
from jax import lax
from jax.experimental import pallas as pl
from jax.experimental.pallas import tpu as pltpu

F32 = jnp.float32
BF16 = jnp.bfloat16

D_MODEL = 1024
D_FF = 2816
D_FOX = 512
D_RET = 512
FOX_HEADS = 8
FOX_HEAD_DIM = 64
RET_HEADS = 4
RET_HEAD_DIM = 128
N_MOD = 9
ROPE_BASE = 10000.0
LN_EPS = 1e-5
DEPTH = 1
DEEPNORM_ALPHA = (2.0 * DEPTH) ** 0.25
FFN_RES_WEIGHT = 0.5
LOG2E = math.log2(math.e)

LANES = 128
BF16_ROWS = 16
ADA_COLS = 2304
TOKEN_TILE = 512
FFN_TILE = 1024
FFN1_GROUPS = (256, 256, 256, 256)
FFN2_GROUPS = (512, 512)
FF_CHUNK = 256
RET_CHUNK = 256
ATTN_TILE = TOKEN_TILE
ATTN_HEADS = 2
ATTN_UNROLL = 2
AUG_COLS = 6
V_ROWS = FOX_HEAD_DIM + BF16_ROWS
V7X_VMEM_BYTES = 64 * 2 ** 20
VMEM_LIMIT_BYTES = V7X_VMEM_BYTES * 7 // 8

NT_DIMS = (((1,), (1,)), ((), ()))
TN_DIMS = (((0,), (0,)), ((), ()))


def _silu(v):
    return v / (1.0 + jnp.exp(-v))


def _layer_norm(y, g, b):
    mu = jnp.mean(y, axis=-1, keepdims=True)
    d = y - mu
    var = jnp.mean(d * d, axis=-1, keepdims=True)
    return d * lax.rsqrt(var + LN_EPS) * g + b


def _split3(v):
    hi = v.astype(BF16)
    r = v - hi.astype(F32)
    mid = r.astype(BF16)
    lo = (r - mid.astype(F32)).astype(BF16)
    return hi, mid, lo


def _stack_terms(v, tail, rows):
    hi, mid, lo = _split3(v)
    parts = [hi.astype(F32), mid.astype(F32), lo.astype(F32), tail]
    if rows > 32:
        parts.append(jnp.zeros((rows - 32, v.shape[1]), F32))
    return jnp.concatenate(parts, axis=0).astype(BF16)


def _const_spec(shape):
    zeros = (0,) * len(shape)
    return pl.BlockSpec(shape, lambda *_: zeros, pipeline_mode=pl.Buffered(1))


def _pallas_call(kernel, *, name, grid, in_specs, out_specs, out_shape, scratch_shapes, operands):
    out_specs_l = out_specs if isinstance(out_specs, (list, tuple)) else [out_specs]
    out_shape_l = out_shape if isinstance(out_shape, (list, tuple)) else [out_shape]

    def window(spec, dtype):
        single = spec.pipeline_mode is not None and spec.pipeline_mode.buffer_count == 1
        return (1 if single else 2) * math.prod(spec.block_shape) * jnp.dtype(dtype).itemsize

    declared = (sum(window(sp, op.dtype) for sp, op in zip(in_specs, operands))
                + sum(window(sp, sh.dtype) for sp, sh in zip(out_specs_l, out_shape_l))
                + sum(math.prod(sc.shape) * jnp.dtype(sc.dtype).itemsize for sc in scratch_shapes))
    assert declared <= VMEM_LIMIT_BYTES, (name, declared)
    return pl.pallas_call(
        kernel, grid=grid, in_specs=in_specs, out_specs=out_specs, out_shape=out_shape,
        scratch_shapes=scratch_shapes,
        compiler_params=pltpu.CompilerParams(dimension_semantics=("arbitrary",) * len(grid),
                                             vmem_limit_bytes=VMEM_LIMIT_BYTES),
        name=name,
    )(*operands)


def _aug_lanes(head):
    half = FOX_HEAD_DIM
    pair, odd = divmod(head, 2)
    data_lo = half if odd else 0
    aug_lo = (0 if odd else half) + AUG_COLS * pair
    return data_lo, aug_lo


def _ada_kernel(c_ref, w_ref, b_ref, o_ref):
    ca = _silu(c_ref[...]).astype(BF16)
    o_ref[...] = jnp.dot(ca, w_ref[...].astype(BF16), preferred_element_type=F32) + b_ref[...]


def _ada(c_pad, w, b):
    rows, d = c_pad.shape
    n = w.shape[1]
    tn = ADA_COLS
    assert n % tn == 0
    return _pallas_call(
        _ada_kernel,
        name="ada", operands=[c_pad, w, b],
        grid=(n // tn,),
        in_specs=[pl.BlockSpec((rows, d), lambda j: (0, 0)),
                  pl.BlockSpec((d, tn), lambda j: (0, j)),
                  pl.BlockSpec((1, tn), lambda j: (0, j))],
        out_specs=pl.BlockSpec((rows, tn), lambda j: (0, j)),
        out_shape=jax.ShapeDtypeStruct((rows, n), F32),
        scratch_shapes=[])


def _ffn_group_rows(groups, i):
    start = sum(groups[:i])
    return slice(start, start + groups[i])


def _swiglu_ln_tile(groups, load_x, store_out, mod_ref, mod_base, gate_w, up_w, down_w, lng_ref, lnb_ref,
                    act_ref):
    n_sub = len(groups)
    n_chunks = D_FF // FF_CHUNK
    lead_finish, lead_next = 2, 6
    sh = mod_ref[0, mod_base:mod_base + 1, :]
    sc = mod_ref[0, mod_base + 1:mod_base + 2, :]
    g = mod_ref[0, mod_base + 2:mod_base + 3, :]
    xs, hs = {}, {}
    rows_of = functools.partial(_ffn_group_rows, groups)

    def begin(i):
        xs[i] = load_x(i)
        hs[i] = (xs[i] * (1.0 + sc) + sh).astype(BF16)

    def chunks(i, first, last):
        for j in range(first, last):
            cols = slice(j * FF_CHUNK, (j + 1) * FF_CHUNK)
            gate = jnp.dot(hs[i], gate_w(j), preferred_element_type=F32)
            up = jnp.dot(hs[i], up_w(j), preferred_element_type=F32)
            act_ref[rows_of(i), cols] = (_silu(gate) * up).astype(BF16)

    def finish(i):
        f = jnp.dot(act_ref[rows_of(i), :], down_w(), preferred_element_type=F32)
        y = DEEPNORM_ALPHA * xs.pop(i) + (FFN_RES_WEIGHT * g) * f
        store_out(i, _layer_norm(y, lng_ref[...], lnb_ref[...]))

    begin(0)
    for i in range(n_sub):
        chunks(i, 0, lead_finish)
        if i > 0:
            finish(i - 1)
        chunks(i, lead_finish, lead_next)
        if i + 1 < n_sub:
            begin(i + 1)
        chunks(i, lead_next, n_chunks)
    finish(n_sub - 1)


def _resident_ffn_weights(wg_ref, wu_ref, wd_ref):
    def cols(j):
        return slice(j * FF_CHUNK, (j + 1) * FF_CHUNK)
    return (lambda j: wg_ref[:, cols(j)]), (lambda j: wu_ref[:, cols(j)]), (lambda: wd_ref[...])


def _ffn_kernel(x_ref, mod_ref, wgc_ref, wuc_ref, wdc_ref, lng_ref, lnb_ref, win_ref,
                o_ref, wa_ref, wr_ref, act_ref, wg_s, wu_s, wd_s, *, mod_base, q_scale):
    step = pl.program_id(0)
    n_chunks = D_FF // FF_CHUNK

    @pl.when(step < n_chunks)
    def _():
        wg_s[step] = wgc_ref[...].astype(BF16)
        wu_s[step] = wuc_ref[...].astype(BF16)
        wd_s[step] = wdc_ref[...].astype(BF16)

    @pl.when(step >= n_chunks)
    def _():
        o_fl = 3 * D_FOX
        o_r = o_fl + FOX_HEADS
        w = win_ref[...]
        wa_ref[:, 0:D_FOX] = (w[:, 0:D_FOX] * q_scale).astype(BF16)
        wa_ref[:, D_FOX:o_fl] = w[:, D_FOX:o_fl].astype(BF16)
        wr_ref[...] = w[:, o_r:o_r + 4 * D_RET].astype(BF16)
        rows_of = functools.partial(_ffn_group_rows, FFN1_GROUPS)

        def load_x(i):
            return x_ref[0, rows_of(i)]

        def store_out(i, value):
            o_ref[0, rows_of(i)] = value

        _swiglu_ln_tile(FFN1_GROUPS, load_x, store_out, mod_ref, mod_base, lambda j: wg_s[j], lambda j: wu_s[j],
                        lambda: wd_s[...].reshape(D_FF, D_MODEL), lng_ref, lnb_ref, act_ref)


def _mix_ffn_kernel(foxt_ref, ret_ref, x_ref, mod_ref, wo_ref, ln2g_ref, ln2b_ref,
                    wg_ref, wu_ref, wd_ref, lng_ref, lnb_ref, o_ref, act_ref):
    g2 = mod_ref[0, 5:6, :]
    rows_of = functools.partial(_ffn_group_rows, FFN2_GROUPS)

    def load_x(i):
        tile, part = divmod(rows_of(i).start, ATTN_TILE)
        fox_t = foxt_ref[0, tile, :, part:part + FFN2_GROUPS[i]]
        m = (lax.dot_general(fox_t, wo_ref[0:D_FOX, :], TN_DIMS, preferred_element_type=F32)
             + jnp.dot(ret_ref[0, rows_of(i)], wo_ref[D_FOX:D_FOX + D_RET, :], preferred_element_type=F32))
        return _layer_norm(DEEPNORM_ALPHA * x_ref[0, rows_of(i)] + g2 * m, ln2g_ref[...], ln2b_ref[...])

    def store_out(i, value):
        o_ref[0, rows_of(i)] = value

    _swiglu_ln_tile(FFN2_GROUPS, load_x, store_out, mod_ref, 6, *_resident_ffn_weights(wg_ref, wu_ref, wd_ref),
                    lng_ref, lnb_ref, act_ref)


def _tile_spec(tile, width):
    return pl.BlockSpec((1, tile, width), lambda b, t: (b, t, 0))


def _ffn_weight_specs():
    return [_const_spec((D_MODEL, D_FF)), _const_spec((D_MODEL, D_FF)), _const_spec((D_FF, D_MODEL)),
            _const_spec((1, D_MODEL)), _const_spec((1, D_MODEL))]


def _ffn(x, mod3, wg, wu, wd, lng, lnb, mod_base, w_in, q_scale):
    b, s, d = x.shape
    tiles = s // FFN_TILE
    steps = b * tiles
    rows = d // steps
    n_chunks = D_FF // FF_CHUNK
    assert d % (steps * BF16_ROWS) == 0

    def tile_of(i):
        return jnp.maximum(i - n_chunks, 0)

    def chunk_of(i):
        return jnp.minimum(i, n_chunks - 1)

    def slab(width):
        return pl.BlockSpec((rows, width), lambda i: (tile_of(i), 0))

    tile_spec = pl.BlockSpec((1, FFN_TILE, d), lambda i: (tile_of(i) // tiles, tile_of(i) % tiles, 0))
    return _pallas_call(
        functools.partial(_ffn_kernel, mod_base=mod_base, q_scale=q_scale),
        name="ffn1", operands=[x, mod3, wg, wu, wd, lng, lnb, w_in],
        grid=(n_chunks + steps,),
        in_specs=[tile_spec, pl.BlockSpec((1, N_MOD, d), lambda i: (tile_of(i) // tiles, 0, 0)),
                  pl.BlockSpec((d, FF_CHUNK), lambda i: (0, chunk_of(i))),
                  pl.BlockSpec((d, FF_CHUNK), lambda i: (0, chunk_of(i))),
                  pl.BlockSpec((FF_CHUNK, d), lambda i: (chunk_of(i), 0)),
                  _const_spec((1, d)), _const_spec((1, d)), slab(w_in.shape[1])],
        out_specs=[tile_spec, slab(3 * D_FOX), slab(4 * D_RET)],
        out_shape=[jax.ShapeDtypeStruct((b, s, d), F32), jax.ShapeDtypeStruct((d, 3 * D_FOX), BF16),
                   jax.ShapeDtypeStruct((d, 4 * D_RET), BF16)],
        scratch_shapes=[pltpu.VMEM((FFN_TILE, D_FF), BF16),
                        pltpu.VMEM((n_chunks, d, FF_CHUNK), BF16), pltpu.VMEM((n_chunks, d, FF_CHUNK), BF16),
                        pltpu.VMEM((n_chunks, FF_CHUNK, d), BF16)])


def _mix_ffn(foxt, ret, x1, mod3, wo, ln2g, ln2b, wg, wu, wd, lng, lnb):
    b, s, d = x1.shape
    per = FFN_TILE // ATTN_TILE
    return _pallas_call(
        _mix_ffn_kernel,
        name="ffn2", operands=[foxt, ret, x1, mod3, wo, ln2g, ln2b, wg, wu, wd, lng, lnb],
        grid=(b, s // FFN_TILE),
        in_specs=[pl.BlockSpec((1, per, D_FOX, ATTN_TILE), lambda b, t: (b, t, 0, 0)),
                  _tile_spec(FFN_TILE, D_RET), _tile_spec(FFN_TILE, d),
                  pl.BlockSpec((1, N_MOD, d), lambda b, t: (b, 0, 0)),
                  _const_spec((D_FOX + D_RET, d)), _const_spec((1, d)), _const_spec((1, d))] + _ffn_weight_specs(),
        out_specs=_tile_spec(FFN_TILE, d),
        out_shape=jax.ShapeDtypeStruct((b, s, d), F32),
        scratch_shapes=[pltpu.VMEM((FFN_TILE, D_FF), BF16)])


def _inproj_kernel(x_ref, mod_ref, wa_ref, wfl_ref, bfl_ref, wr_ref, triu_ref, paugk_ref, paugqt_ref,
                   cos_ref, sin_ref, intra_ref, qdec_ref, kdec_ref, cdec_ref, gng_ref, gnb_ref,
                   qpt_ref, kp_ref, fvt_ref, ret_ref, carry_ref, state_ref):
    tm = TOKEN_TILE
    half = FOX_HEAD_DIM

    @pl.when(pl.program_id(1) == 0)
    def _():
        carry_ref[...] = jnp.zeros_like(carry_ref)
        state_ref[...] = jnp.zeros_like(state_ref)

    sh = mod_ref[0, 3:4, :]
    sc = mod_ref[0, 4:5, :]
    h = (x_ref[0] * (1.0 + sc) + sh).astype(BF16)

    r_all = jnp.dot(h, wr_ref[...], preferred_element_type=F32)
    cos2 = cos_ref[...]
    sin2 = sin_ref[...]
    c = RET_CHUNK
    for hd in range(RET_HEADS):
        lo_, hi_ = hd * RET_HEAD_DIM, (hd + 1) * RET_HEAD_DIM
        tq = r_all[:, lo_:hi_]
        tk = r_all[:, D_RET + lo_:D_RET + hi_]
        q_r = tq * cos2 + pltpu.roll(tq, RET_HEAD_DIM // 2, 1) * sin2
        k_r = (tk * cos2 + pltpu.roll(tk, RET_HEAD_DIM // 2, 1) * sin2) * (RET_HEAD_DIM ** -0.5)
        v = r_all[:, 2 * D_RET + lo_:2 * D_RET + hi_]
        gate = r_all[:, 3 * D_RET + lo_:3 * D_RET + hi_]
        state = state_ref[hd]
        for ci in range(tm // c):
            rows = slice(ci * c, (ci + 1) * c)
            qc = q_r[rows].astype(BF16)
            kc = k_r[rows]
            vc = v[rows].astype(BF16)
            s = lax.dot_general(qc, kc.astype(BF16), NT_DIMS, preferred_element_type=F32) * intra_ref[hd]
            o = (jnp.dot(s.astype(BF16), vc, preferred_element_type=F32)
                 + qdec_ref[hd] * jnp.dot(qc, state.astype(BF16), preferred_element_type=F32))
            state = state * cdec_ref[hd] + lax.dot_general(
                (kc * kdec_ref[hd]).astype(BF16), vc, TN_DIMS, preferred_element_type=F32)
            mu = jnp.mean(o, axis=-1, keepdims=True)
            d = o - mu
            var = jnp.mean(d * d, axis=-1, keepdims=True)
            yn = d * lax.rsqrt(var + LN_EPS) * gng_ref[:, lo_:hi_] + gnb_ref[:, lo_:hi_]
            ret_ref[0, rows, lo_:hi_] = (_silu(gate[rows]) * yn).astype(BF16)
        state_ref[hd] = state

    z = jnp.dot(h, wfl_ref[...], preferred_element_type=F32) + bfl_ref[...]
    logf = jnp.minimum(z, 0.0) - jnp.log1p(jnp.exp(-jnp.abs(z)))
    logf_t = logf.T[0:FOX_HEADS, :]
    terms = _stack_terms(logf_t, jnp.zeros((FOX_HEADS, tm), F32), 4 * FOX_HEADS)
    sums = jnp.dot(terms, triu_ref[...], preferred_element_type=F32)
    cum_t = (sums[0:8] + sums[8:16] + sums[16:24]) + carry_ref[:, 0:1]
    carry_ref[...] = jnp.broadcast_to(cum_t[:, tm - 1:tm], carry_ref.shape)

    row8 = lax.broadcasted_iota(jnp.int32, (FOX_HEADS, tm), 0)
    packed_t = _stack_terms(cum_t * LOG2E, jnp.where(row8 == 0, 1.0, 0.0), LANES)
    aug_k = lax.dot_general(packed_t, paugk_ref[...], TN_DIMS, preferred_element_type=F32)
    aug_qt = jnp.dot(paugqt_ref[...], packed_t, preferred_element_type=F32)

    qk = jnp.dot(h, wa_ref[:, 0:2 * D_FOX], preferred_element_type=F32)
    lane = lax.broadcasted_iota(jnp.int32, (tm, LANES), 1)
    row = lax.broadcasted_iota(jnp.int32, (LANES, tm), 0)
    q_t = [qk[:, pr * LANES:(pr + 1) * LANES].T for pr in range(FOX_HEADS // 2)]
    for hd in range(FOX_HEADS):
        data_lo, aug_lo = _aug_lanes(hd)
        src = (hd // 2) * LANES
        out = slice(hd * LANES, (hd + 1) * LANES)
        is_data = (lane >= data_lo) & (lane < data_lo + half)
        is_aug = (lane >= aug_lo) & (lane < aug_lo + AUG_COLS)
        kp_ref[0, :, out] = jnp.where(is_data, qk[:, D_FOX + src:D_FOX + src + LANES],
                                      jnp.where(is_aug, aug_k, 0.0)).astype(BF16)
        is_data_t = (row >= data_lo) & (row < data_lo + half)
        is_aug_t = (row >= aug_lo) & (row < aug_lo + AUG_COLS)
        qpt_ref[0, 0, out, :] = jnp.where(is_data_t, q_t[hd // 2],
                                          jnp.where(is_aug_t, aug_qt, 0.0)).astype(BF16)

    vt = jnp.dot(h, wa_ref[:, 2 * D_FOX:3 * D_FOX], preferred_element_type=F32).T.astype(BF16)
    for hd in range(FOX_HEADS):
        fvt_ref[0, 0, hd * V_ROWS:hd * V_ROWS + half, :] = vt[hd * half:(hd + 1) * half]
        fvt_ref[0, 0, hd * V_ROWS + half:(hd + 1) * V_ROWS, :] = jnp.ones((BF16_ROWS, tm), BF16)


def _inproj(x1, mod3, wa, wfl, bfl, wr, triu, paugk, paugqt, cos2, sin2, intra, qdec, kdec, cdec, gng, gnb):
    b, s, d = x1.shape
    tm = TOKEN_TILE
    nt = s // tm
    c = RET_CHUNK
    in_specs = [
        _tile_spec(tm, d), pl.BlockSpec((1, N_MOD, d), lambda b, t: (b, 0, 0)),
        _const_spec((d, 3 * D_FOX)), _const_spec((d, LANES)), _const_spec((1, LANES)),
        _const_spec((d, 4 * D_RET)), _const_spec((tm, tm)), _const_spec((LANES, LANES)), _const_spec((LANES, LANES)),
        pl.BlockSpec((tm, LANES), lambda b, t: (t, 0)), pl.BlockSpec((tm, LANES), lambda b, t: (t, 0)),
        _const_spec((RET_HEADS, c, c)), _const_spec((RET_HEADS, c, LANES)), _const_spec((RET_HEADS, c, LANES)),
        _const_spec((RET_HEADS, 1, LANES)), _const_spec((1, D_RET)), _const_spec((1, D_RET)),
    ]
    out_specs = [
        pl.BlockSpec((1, 1, FOX_HEADS * LANES, tm), lambda b, t: (b, t, 0, 0)), _tile_spec(tm, FOX_HEADS * LANES),
        pl.BlockSpec((1, 1, FOX_HEADS * V_ROWS, tm), lambda b, t: (b, t, 0, 0)),
        _tile_spec(tm, D_RET),
    ]
    out_shape = [
        jax.ShapeDtypeStruct((b, nt, FOX_HEADS * LANES, tm), BF16), jax.ShapeDtypeStruct((b, s, FOX_HEADS * LANES), BF16),
        jax.ShapeDtypeStruct((b, nt, FOX_HEADS * V_ROWS, tm), BF16),
        jax.ShapeDtypeStruct((b, s, D_RET), BF16),
    ]
    return _pallas_call(
        _inproj_kernel,
        name="inproj",
        operands=[x1, mod3, wa, wfl, bfl, wr, triu, paugk, paugqt, cos2, sin2, intra, qdec, kdec, cdec, gng, gnb],
        grid=(b, nt),
        in_specs=in_specs, out_specs=out_specs, out_shape=out_shape,
        scratch_shapes=[pltpu.VMEM((FOX_HEADS, LANES), F32),
                        pltpu.VMEM((RET_HEADS, RET_HEAD_DIM, RET_HEAD_DIM), F32)])


def _fox_kernel(qpt_ref, kp_ref, fvt_ref, *rest):
    n_cast = (len(rest) - 6) // 2
    cast_in, o_ref, cast_out = rest[:n_cast], rest[n_cast], rest[n_cast + 1:2 * n_cast + 1]
    s_ref, p_ref, acc_ref, mask_ref, stat_ref = rest[2 * n_cast + 1:]
    for src, dst in zip(cast_in, cast_out):
        dst[...] = src[...].astype(BF16)
    _fox_attention(qpt_ref, kp_ref, fvt_ref, o_ref, s_ref, p_ref, acc_ref, mask_ref, stat_ref)


def _fox_attention(qpt_ref, kp_ref, fvt_ref, o_ref, s_ref, p_ref, acc_ref, mask_ref, stat_ref):
    t = ATTN_TILE
    nq = qpt_ref.shape[1]
    half = FOX_HEAD_DIM

    k_pos = lax.broadcasted_iota(jnp.int32, (t, t), 0)
    q_pos = lax.broadcasted_iota(jnp.int32, (t, t), 1)
    mask_ref[0] = jnp.zeros((t, t), F32)
    mask_ref[1] = jnp.where(k_pos <= q_pos, 0.0, -jnp.inf)
    acc_ref[...] = jnp.zeros_like(acc_ref)
    stat_ref[...] = jnp.zeros_like(stat_ref)

    unroll = ATTN_UNROLL
    row_m, row_alpha_new, row_alpha_old = 1, 1 + unroll, 1 + 2 * unroll
    steps = [(q, k) for q in range(nq) for k in range(q + 1)]
    n_groups = len(steps) // unroll
    assert len(steps) % unroll == 0 and n_groups >= 4 and 1 + 3 * unroll <= stat_ref.shape[1]

    def stat(hh, row):
        return stat_ref[hh, row:row + 1, :]

    def stage_a(hh, group):
        alpha_new = [stat(hh, row_alpha_new + u) for u in range(unroll)]
        m = stat(hh, 0)
        for u, (qa, ka) in enumerate(group):
            static = isinstance(ka, int)
            krows = pl.ds(ka * t, t) if static else pl.ds(pl.multiple_of(ka * t, t), t)
            diag = int(ka == qa) if static else (ka == qa).astype(jnp.int32)
            s = jnp.dot(kp_ref[0, krows, hh * LANES:(hh + 1) * LANES],
                        qpt_ref[0, qa, hh * LANES:(hh + 1) * LANES, :],
                        preferred_element_type=F32) + mask_ref[diag]
            s_ref[u * ATTN_HEADS + hh] = s
            m_prev = jnp.where(ka == 0, -jnp.inf, m)
            m = jnp.maximum(m_prev, jnp.max(s, axis=0, keepdims=True))
            stat_ref[hh, row_m + u:row_m + u + 1, :] = m
            stat_ref[hh, row_alpha_new + u:row_alpha_new + u + 1, :] = jnp.exp2(m_prev - m)
            stat_ref[hh, row_alpha_old + u:row_alpha_old + u + 1, :] = alpha_new[u]
        stat_ref[hh, 0:1, :] = m

    def stage_b(hh):
        for u in range(unroll):
            buf = u * ATTN_HEADS + hh
            p_ref[buf] = jnp.exp2(s_ref[buf] - stat(hh, row_m + u)).astype(BF16)

    def stage_c(hh, group, alpha_row):
        acc = acc_ref[hh]
        for u, (qc, kc) in enumerate(group):
            vt = fvt_ref[0, kc, hh * V_ROWS:(hh + 1) * V_ROWS, :]
            acc = stat(hh, alpha_row + u) * acc + jnp.dot(vt, p_ref[u * ATTN_HEADS + hh],
                                                          preferred_element_type=F32)
            if isinstance(kc, int):
                if kc != qc:
                    continue
                tile = qc
            else:
                tile = jnp.where(kc == qc, qc, nq)
            o_ref[0, tile, hh * half:(hh + 1) * half, :] = (acc[:half] / acc[half:half + 1]).astype(BF16)
        acc_ref[hh] = acc

    def group(g):
        return steps[g * unroll:(g + 1) * unroll]

    def next_step(q, k):
        wrap = k >= q
        return jnp.where(wrap, q + 1, q), jnp.where(wrap, 0, k + 1)

    def body(j, carry):
        a_first, b_group, c_group = carry
        a_group = []
        nxt = a_first
        for _ in range(unroll):
            a_group.append(nxt)
            nxt = next_step(*nxt)
        for hh in range(ATTN_HEADS):
            stage_c(hh, c_group, row_alpha_old)
            stage_b(hh)
            stage_a(hh, a_group)
        return nxt, tuple(a_group), b_group

    as_i32 = lambda grp: tuple((jnp.int32(q), jnp.int32(k)) for q, k in grp)
    for hh in range(ATTN_HEADS):
        stage_a(hh, group(0))
    for hh in range(ATTN_HEADS):
        stage_b(hh)
        stage_a(hh, group(1))
    lax.fori_loop(2, n_groups, body, (as_i32(group(2))[0], as_i32(group(1)), as_i32(group(0))))
    for hh in range(ATTN_HEADS):
        stage_c(hh, group(n_groups - 2), row_alpha_old)
        stage_b(hh)
    for hh in range(ATTN_HEADS):
        stage_c(hh, group(n_groups - 1), row_alpha_new)


def _fox(qpt, kp, fvt, f32_weights):
    b, s, _ = kp.shape
    t = ATTN_TILE
    nt = s // t
    g = ATTN_HEADS
    groups = FOX_HEADS // g
    steps = b * groups
    slab_specs = []
    for w in f32_weights:
        assert w.shape[0] % (steps * BF16_ROWS) == 0
        slab_specs.append(pl.BlockSpec((w.shape[0] // steps, w.shape[1]), lambda b, p: (b * groups + p, 0)))
    return _pallas_call(
        _fox_kernel,
        name="fox", operands=[qpt, kp, fvt, *f32_weights],
        grid=(b, groups),
        in_specs=[pl.BlockSpec((1, nt, g * LANES, t), lambda b, p: (b, 0, p, 0)),
                  pl.BlockSpec((1, s, g * LANES), lambda b, p: (b, 0, p)),
                  pl.BlockSpec((1, nt, g * V_ROWS, t), lambda b, p: (b, 0, p, 0))] + slab_specs,
        out_specs=[pl.BlockSpec((1, nt + 1, g * FOX_HEAD_DIM, t), lambda b, p: (b, 0, p, 0))] + slab_specs,
        out_shape=[jax.ShapeDtypeStruct((b, nt + 1, D_FOX, t), BF16)]
        + [jax.ShapeDtypeStruct(w.shape, BF16) for w in f32_weights],
        scratch_shapes=[pltpu.VMEM((ATTN_UNROLL * g, t, t), F32), pltpu.VMEM((ATTN_UNROLL * g, t, t), BF16),
                        pltpu.VMEM((g, V_ROWS, t), F32), pltpu.VMEM((2, t, t), F32),
                        pltpu.VMEM((g, 8 * pl.cdiv(1 + 3 * ATTN_UNROLL, 8), t), F32)])


def _aug_placement():
    pk = np.zeros((LANES, LANES), np.float32)
    pq = np.zeros((LANES, LANES), np.float32)
    for head in range(FOX_HEADS):
        _, base = _aug_lanes(head)
        for j in range(3):
            pk[24, base + j] = 1.0
            pk[8 * j + head, base + 3 + j] = -1.0
            pq[base + j, 8 * j + head] = 1.0
            pq[base + 3 + j, 24] = 1.0
    return jnp.asarray(pk, BF16), jnp.asarray(pq, BF16)


def _rotary_tables(seq):
    half = RET_HEAD_DIM // 2
    inv_freq = ROPE_BASE ** (-np.arange(half, dtype=np.float64) / half)
    ang = np.arange(seq, dtype=np.float64)[:, None] * inv_freq[None, :]
    cos, sin = np.cos(ang), np.sin(ang)
    return (jnp.asarray(np.concatenate([cos, cos], axis=-1), F32),
            jnp.asarray(np.concatenate([-sin, sin], axis=-1), F32))


def _retention_tables():
    c = RET_CHUNK
    log_gamma = np.log1p(-np.power(2.0, -5.0 - np.arange(RET_HEADS, dtype=np.float64)))
    idx = np.arange(c, dtype=np.float64)
    diff = idx[:, None] - idx[None, :]
    intra = np.where(diff >= 0, np.exp(log_gamma[:, None, None] * np.maximum(diff, 0.0)), 0.0)
    qdec = np.exp(log_gamma[:, None] * (idx + 1.0))[..., None]
    kdec = np.exp(log_gamma[:, None] * (c - 1.0 - idx))[..., None]
    cdec = np.exp(log_gamma * c)[:, None, None]
    bc = lambda a: jnp.asarray(np.broadcast_to(a, a.shape[:2] + (LANES,)), F32)
    return jnp.asarray(intra, F32), bc(qdec), bc(kdec), bc(cdec)


def kernel(x, c, w_ada, b_ada, ffn1_w_gate, ffn1_w_up, ffn1_w_down, ln1_g, ln1_b, w_in, fox_b_f, ret_gn_g,
           ret_gn_b, w_o, ln2_g, ln2_b, ffn2_w_gate, ffn2_w_up, ffn2_w_down, ln3_g, ln3_b):
    batch, seq, d = x.shape
    assert d == D_MODEL and seq % FFN_TILE == 0 and FFN_TILE % TOKEN_TILE == 0 and w_ada.shape[0] == DEPTH
    assert sum(FFN1_GROUPS) == FFN_TILE and sum(FFN2_GROUPS) == FFN_TILE and all(
        _ffn_group_rows(FFN2_GROUPS, i).start // ATTN_TILE == (_ffn_group_rows(FFN2_GROUPS, i).stop - 1) // ATTN_TILE
        for i in range(len(FFN2_GROUPS)))
    layer = 0

    c_pad = jnp.pad(c, ((0, 8 - batch), (0, 0)))
    mod3 = _ada(c_pad, w_ada[layer], b_ada[layer][None, :])[:batch].reshape(batch, N_MOD, d)

    w = w_in[layer]
    o_fl = 3 * D_FOX
    o_r = o_fl + FOX_HEADS
    q_scale = LOG2E * FOX_HEAD_DIM ** -0.5
    x1, wa, wr = _ffn(x, mod3, ffn1_w_gate[layer], ffn1_w_up[layer], ffn1_w_down[layer],
                      ln1_g[layer][None, :], ln1_b[layer][None, :], 0, w, q_scale)

    wfl = jnp.pad(w[:, o_fl:o_r], ((0, 0), (0, LANES - FOX_HEADS))).astype(BF16)
    bfl = jnp.pad(fox_b_f[layer], (0, LANES - FOX_HEADS))[None, :]
    triu = jnp.asarray(np.triu(np.ones((TOKEN_TILE, TOKEN_TILE), np.float32)), BF16)
    paugk, paugqt = _aug_placement()
    cos2, sin2 = _rotary_tables(seq)
    intra, qdec, kdec, cdec = _retention_tables()
    qpt, kp, fvt, ret = _inproj(x1, mod3, wa, wfl, bfl, wr, triu, paugk, paugqt, cos2, sin2,
                               intra, qdec, kdec, cdec, ret_gn_g[layer][None, :], ret_gn_b[layer][None, :])

    foxt, wo, wg2, wu2, wd2 = _fox(qpt, kp, fvt, [w_o[layer], ffn2_w_gate[layer], ffn2_w_up[layer],
                                                  ffn2_w_down[layer]])

    return _mix_ffn(foxt, ret, x1, mod3, wo, ln2_g[layer][None, :], ln2_b[layer][None, :],
                    wg2, wu2, wd2, ln3_g[layer][None, :], ln3_b[layer][None, :])
```

```python
import functools
import math

import numpy as np
import jax
import jax.numpy as jnp
from jax import lax
from jax.experimental import pallas as pl
from jax.experimental.pallas import tpu as pltpu

F32 = jnp.float32
BF16 = jnp.bfloat16

D_MODEL = 1024
D_FF = 2816
D_FOX = 512
D_RET = 512
FOX_HEADS = 8
FOX_HEAD_DIM = 64
RET_HEADS = 4
RET_HEAD_DIM = 128
N_MOD = 9
ROPE_BASE = 10000.0
LN_EPS = 1e-5
DEPTH = 1
DEEPNORM_ALPHA = (2.0 * DEPTH) ** 0.25
FFN_RES_WEIGHT = 0.5
LOG2E = math.log2(math.e)

LANES = 128
BF16_ROWS = 16
ADA_COLS = 2304
TOKEN_TILE = 512
FFN_TILE = 1024
FFN1_GROUPS = (512, 512)
FFN2_GROUPS = (512, 512)
FF_CHUNK = 256
RET_CHUNK = 256
ATTN_TILE = TOKEN_TILE
ATTN_HEADS = 2
ATTN_UNROLL = 2
AUG_COLS = 6
V_ROWS = FOX_HEAD_DIM + BF16_ROWS
V7X_VMEM_BYTES = 64 * 2 ** 20
VMEM_LIMIT_BYTES = V7X_VMEM_BYTES * 7 // 8

NT_DIMS = (((1,), (1,)), ((), ()))
TN_DIMS = (((0,), (0,)), ((), ()))


def _silu(v):
    return v / (1.0 + jnp.exp(-v))


def _layer_norm(y, g, b):
    mu = jnp.mean(y, axis=-1, keepdims=True)
    d = y - mu
    var = jnp.mean(d * d, axis=-1, keepdims=True)
    return d * lax.rsqrt(var + LN_EPS) * g + b


def _split3(v):
    hi = v.astype(BF16)
    r = v - hi.astype(F32)
    mid = r.astype(BF16)
    lo = (r - mid.astype(F32)).astype(BF16)
    return hi, mid, lo


def _stack_terms(v, tail, rows):
    hi, mid, lo = _split3(v)
    parts = [hi.astype(F32), mid.astype(F32), lo.astype(F32), tail]
    if rows > 32:
        parts.append(jnp.zeros((rows - 32, v.shape[1]), F32))
    return jnp.concatenate(parts, axis=0).astype(BF16)


def _const_spec(shape):
    zeros = (0,) * len(shape)
    return pl.BlockSpec(shape, lambda *_: zeros, pipeline_mode=pl.Buffered(1))


def _pallas_call(kernel, *, name, grid, in_specs, out_specs, out_shape, scratch_shapes, operands):
    out_specs_l = out_specs if isinstance(out_specs, (list, tuple)) else [out_specs]
    out_shape_l = out_shape if isinstance(out_shape, (list, tuple)) else [out_shape]

    def window(spec, dtype):
        single = spec.pipeline_mode is not None and spec.pipeline_mode.buffer_count == 1
        return (1 if single else 2) * math.prod(spec.block_shape) * jnp.dtype(dtype).itemsize

    declared = (sum(window(sp, op.dtype) for sp, op in zip(in_specs, operands))
                + sum(window(sp, sh.dtype) for sp, sh in zip(out_specs_l, out_shape_l))
                + sum(math.prod(sc.shape) * jnp.dtype(sc.dtype).itemsize for sc in scratch_shapes))
    assert declared <= VMEM_LIMIT_BYTES, (name, declared)
    return pl.pallas_call(
        kernel, grid=grid, in_specs=in_specs, out_specs=out_specs, out_shape=out_shape,
        scratch_shapes=scratch_shapes,
        compiler_params=pltpu.CompilerParams(dimension_semantics=("arbitrary",) * len(grid),
                                             vmem_limit_bytes=VMEM_LIMIT_BYTES),
        name=name,
    )(*operands)


def _aug_lanes(head):
    half = FOX_HEAD_DIM
    pair, odd = divmod(head, 2)
    data_lo = half if odd else 0
    aug_lo = (0 if odd else half) + AUG_COLS * pair
    return data_lo, aug_lo


def _ada_kernel(c_ref, w_ref, b_ref, o_ref):
    ca = _silu(c_ref[...]).astype(BF16)
    o_ref[...] = jnp.dot(ca, w_ref[...].astype(BF16), preferred_element_type=F32) + b_ref[...]


def _ada(c_pad, w, b):
    rows, d = c_pad.shape
    n = w.shape[1]
    tn = ADA_COLS
    assert n % tn == 0
    return _pallas_call(
        _ada_kernel,
        name="ada", operands=[c_pad, w, b],
        grid=(n // tn,),
        in_specs=[pl.BlockSpec((rows, d), lambda j: (0, 0)),
                  pl.BlockSpec((d, tn), lambda j: (0, j)),
                  pl.BlockSpec((1, tn), lambda j: (0, j))],
        out_specs=pl.BlockSpec((rows, tn), lambda j: (0, j)),
        out_shape=jax.ShapeDtypeStruct((rows, n), F32),
        scratch_shapes=[])


def _ffn_group_rows(groups, i):
    start = sum(groups[:i])
    return slice(start, start + groups[i])


def _swiglu_ln_tile(groups, load_x, store_out, mod_ref, mod_base, gate_w, up_w, down_w, lng_ref, lnb_ref,
                    act_ref):
    n_sub = len(groups)
    n_chunks = D_FF // FF_CHUNK
    lead_finish, lead_next = 2, 6
    sh = mod_ref[0, mod_base:mod_base + 1, :]
    sc = mod_ref[0, mod_base + 1:mod_base + 2, :]
    g = mod_ref[0, mod_base + 2:mod_base + 3, :]
    xs, hs = {}, {}
    rows_of = functools.partial(_ffn_group_rows, groups)

    def begin(i):
        xs[i] = load_x(i)
        hs[i] = (xs[i] * (1.0 + sc) + sh).astype(BF16)

    def chunks(i, first, last):
        for j in range(first, last):
            cols = slice(j * FF_CHUNK, (j + 1) * FF_CHUNK)
            gate = jnp.dot(hs[i], gate_w(j), preferred_element_type=F32)
            up = jnp.dot(hs[i], up_w(j), preferred_element_type=F32)
            act_ref[rows_of(i), cols] = (_silu(gate) * up).astype(BF16)

    def finish(i):
        f = jnp.dot(act_ref[rows_of(i), :], down_w(), preferred_element_type=F32)
        y = DEEPNORM_ALPHA * xs.pop(i) + (FFN_RES_WEIGHT * g) * f
        store_out(i, _layer_norm(y, lng_ref[...], lnb_ref[...]))

    begin(0)
    for i in range(n_sub):
        chunks(i, 0, lead_finish)
        if i > 0:
            finish(i - 1)
        chunks(i, lead_finish, lead_next)
        if i + 1 < n_sub:
            begin(i + 1)
        chunks(i, lead_next, n_chunks)
    finish(n_sub - 1)


def _resident_ffn_weights(wg_ref, wu_ref, wd_ref):
    def cols(j):
        return slice(j * FF_CHUNK, (j + 1) * FF_CHUNK)
    return (lambda j: wg_ref[:, cols(j)]), (lambda j: wu_ref[:, cols(j)]), (lambda: wd_ref[...])


def _ffn_kernel(x_ref, mod_ref, wgc_ref, wuc_ref, wdc_ref, lng_ref, lnb_ref, win_ref,
                o_ref, wa_ref, wr_ref, act_ref, wg_s, wu_s, wd_s, *, mod_base, q_scale):
    step = pl.program_id(0)
    n_chunks = D_FF // FF_CHUNK

    @pl.when(step < n_chunks)
    def _():
        wg_s[step] = wgc_ref[...].astype(BF16)
        wu_s[step] = wuc_ref[...].astype(BF16)
        wd_s[step] = wdc_ref[...].astype(BF16)

    @pl.when(step >= n_chunks)
    def _():
        o_fl = 3 * D_FOX
        o_r = o_fl + FOX_HEADS
        w = win_ref[...]
        wa_ref[:, 0:D_FOX] = (w[:, 0:D_FOX] * q_scale).astype(BF16)
        wa_ref[:, D_FOX:o_fl] = w[:, D_FOX:o_fl].astype(BF16)
        wr_ref[...] = w[:, o_r:o_r + 4 * D_RET].astype(BF16)
        rows_of = functools.partial(_ffn_group_rows, FFN1_GROUPS)

        def load_x(i):
            return x_ref[0, rows_of(i)]

        def store_out(i, value):
            o_ref[0, rows_of(i)] = value

        _swiglu_ln_tile(FFN1_GROUPS, load_x, store_out, mod_ref, mod_base, lambda j: wg_s[j], lambda j: wu_s[j],
                        lambda: wd_s[...].reshape(D_FF, D_MODEL), lng_ref, lnb_ref, act_ref)


def _mix_ffn_kernel(foxt_ref, ret_ref, x_ref, mod_ref, wo_ref, ln2g_ref, ln2b_ref,
                    wg_ref, wu_ref, wd_ref, lng_ref, lnb_ref, o_ref, act_ref):
    g2 = mod_ref[0, 5:6, :]
    rows_of = functools.partial(_ffn_group_rows, FFN2_GROUPS)

    def load_x(i):
        tile, part = divmod(rows_of(i).start, ATTN_TILE)
        fox_t = foxt_ref[0, tile, :, part:part + FFN2_GROUPS[i]]
        m = (lax.dot_general(fox_t, wo_ref[0:D_FOX, :], TN_DIMS, preferred_element_type=F32)
             + jnp.dot(ret_ref[0, rows_of(i)], wo_ref[D_FOX:D_FOX + D_RET, :], preferred_element_type=F32))
        return _layer_norm(DEEPNORM_ALPHA * x_ref[0, rows_of(i)] + g2 * m, ln2g_ref[...], ln2b_ref[...])

    def store_out(i, value):
        o_ref[0, rows_of(i)] = value

    _swiglu_ln_tile(FFN2_GROUPS, load_x, store_out, mod_ref, 6, *_resident_ffn_weights(wg_ref, wu_ref, wd_ref),
                    lng_ref, lnb_ref, act_ref)


def _tile_spec(tile, width):
    return pl.BlockSpec((1, tile, width), lambda b, t: (b, t, 0))


def _ffn_weight_specs():
    return [_const_spec((D_MODEL, D_FF)), _const_spec((D_MODEL, D_FF)), _const_spec((D_FF, D_MODEL)),
            _const_spec((1, D_MODEL)), _const_spec((1, D_MODEL))]


def _ffn(x, mod3, wg, wu, wd, lng, lnb, mod_base, w_in, q_scale):
    b, s, d = x.shape
    tiles = s // FFN_TILE
    steps = b * tiles
    rows = d // steps
    n_chunks = D_FF // FF_CHUNK
    assert d % (steps * BF16_ROWS) == 0

    def tile_of(i):
        return jnp.maximum(i - n_chunks, 0)

    def chunk_of(i):
        return jnp.minimum(i, n_chunks - 1)

    def slab(width):
        return pl.BlockSpec((rows, width), lambda i: (tile_of(i), 0))

    tile_spec = pl.BlockSpec((1, FFN_TILE, d), lambda i: (tile_of(i) // tiles, tile_of(i) % tiles, 0))
    return _pallas_call(
        functools.partial(_ffn_kernel, mod_base=mod_base, q_scale=q_scale),
        name="ffn1", operands=[x, mod3, wg, wu, wd, lng, lnb, w_in],
        grid=(n_chunks + steps,),
        in_specs=[tile_spec, pl.BlockSpec((1, N_MOD, d), lambda i: (tile_of(i) // tiles, 0, 0)),
                  pl.BlockSpec((d, FF_CHUNK), lambda i: (0, chunk_of(i))),
                  pl.BlockSpec((d, FF_CHUNK), lambda i: (0, chunk_of(i))),
                  pl.BlockSpec((FF_CHUNK, d), lambda i: (chunk_of(i), 0)),
                  _const_spec((1, d)), _const_spec((1, d)), slab(w_in.shape[1])],
        out_specs=[tile_spec, slab(3 * D_FOX), slab(4 * D_RET)],
        out_shape=[jax.ShapeDtypeStruct((b, s, d), F32), jax.ShapeDtypeStruct((d, 3 * D_FOX), BF16),
                   jax.ShapeDtypeStruct((d, 4 * D_RET), BF16)],
        scratch_shapes=[pltpu.VMEM((FFN_TILE, D_FF), BF16),
                        pltpu.VMEM((n_chunks, d, FF_CHUNK), BF16), pltpu.VMEM((n_chunks, d, FF_CHUNK), BF16),
                        pltpu.VMEM((n_chunks, FF_CHUNK, d), BF16)])


def _mix_ffn(foxt, ret, x1, mod3, wo, ln2g, ln2b, wg, wu, wd, lng, lnb):
    b, s, d = x1.shape
    per = FFN_TILE // ATTN_TILE
    return _pallas_call(
        _mix_ffn_kernel,
        name="ffn2", operands=[foxt, ret, x1, mod3, wo, ln2g, ln2b, wg, wu, wd, lng, lnb],
        grid=(b, s // FFN_TILE),
        in_specs=[pl.BlockSpec((1, per, D_FOX, ATTN_TILE), lambda b, t: (b, t, 0, 0)),
                  _tile_spec(FFN_TILE, D_RET), _tile_spec(FFN_TILE, d),
                  pl.BlockSpec((1, N_MOD, d), lambda b, t: (b, 0, 0)),
                  _const_spec((D_FOX + D_RET, d)), _const_spec((1, d)), _const_spec((1, d))] + _ffn_weight_specs(),
        out_specs=_tile_spec(FFN_TILE, d),
        out_shape=jax.ShapeDtypeStruct((b, s, d), F32),
        scratch_shapes=[pltpu.VMEM((FFN_TILE, D_FF), BF16)])


def _inproj_kernel(x_ref, mod_ref, wa_ref, wfl_ref, bfl_ref, wr_ref, triu_ref, paugk_ref, paugqt_ref,
                   cos_ref, sin_ref, intra_ref, qdec_ref, kdec_ref, cdec_ref, gng_ref, gnb_ref,
                   qpt_ref, kp_ref, fvt_ref, ret_ref, carry_ref, state_ref):
    tm = TOKEN_TILE
    half = FOX_HEAD_DIM

    @pl.when(pl.program_id(1) == 0)
    def _():
        carry_ref[...] = jnp.zeros_like(carry_ref)
        state_ref[...] = jnp.zeros_like(state_ref)

    sh = mod_ref[0, 3:4, :]
    sc = mod_ref[0, 4:5, :]
    h = (x_ref[0] * (1.0 + sc) + sh).astype(BF16)

    r_all = jnp.dot(h, wr_ref[...], preferred_element_type=F32)
    cos2 = cos_ref[...]
    sin2 = sin_ref[...]
    c = RET_CHUNK
    for hd in range(RET_HEADS):
        lo_, hi_ = hd * RET_HEAD_DIM, (hd + 1) * RET_HEAD_DIM
        tq = r_all[:, lo_:hi_]
        tk = r_all[:, D_RET + lo_:D_RET + hi_]
        q_r = tq * cos2 + pltpu.roll(tq, RET_HEAD_DIM // 2, 1) * sin2
        k_r = (tk * cos2 + pltpu.roll(tk, RET_HEAD_DIM // 2, 1) * sin2) * (RET_HEAD_DIM ** -0.5)
        v = r_all[:, 2 * D_RET + lo_:2 * D_RET + hi_]
        gate = r_all[:, 3 * D_RET + lo_:3 * D_RET + hi_]
        state = state_ref[hd]
        for ci in range(tm // c):
            rows = slice(ci * c, (ci + 1) * c)
            qc = q_r[rows].astype(BF16)
            kc = k_r[rows]
            vc = v[rows].astype(BF16)
            s = lax.dot_general(qc, kc.astype(BF16), NT_DIMS, preferred_element_type=F32) * intra_ref[hd]
            o = (jnp.dot(s.astype(BF16), vc, preferred_element_type=F32)
                 + qdec_ref[hd] * jnp.dot(qc, state.astype(BF16), preferred_element_type=F32))
            state = state * cdec_ref[hd] + lax.dot_general(
                (kc * kdec_ref[hd]).astype(BF16), vc, TN_DIMS, preferred_element_type=F32)
            mu = jnp.mean(o, axis=-1, keepdims=True)
            d = o - mu
            var = jnp.mean(d * d, axis=-1, keepdims=True)
            yn = d * lax.rsqrt(var + LN_EPS) * gng_ref[:, lo_:hi_] + gnb_ref[:, lo_:hi_]
            ret_ref[0, rows, lo_:hi_] = (_silu(gate[rows]) * yn).astype(BF16)
        state_ref[hd] = state

    z = jnp.dot(h, wfl_ref[...], preferred_element_type=F32) + bfl_ref[...]
    logf = jnp.minimum(z, 0.0) - jnp.log1p(jnp.exp(-jnp.abs(z)))
    logf_t = logf.T[0:FOX_HEADS, :]
    terms = _stack_terms(logf_t, jnp.zeros((FOX_HEADS, tm), F32), 4 * FOX_HEADS)
    sums = jnp.dot(terms, triu_ref[...], preferred_element_type=F32)
    cum_t = (sums[0:8] + sums[8:16] + sums[16:24]) + carry_ref[:, 0:1]
    carry_ref[...] = jnp.broadcast_to(cum_t[:, tm - 1:tm], carry_ref.shape)

    row8 = lax.broadcasted_iota(jnp.int32, (FOX_HEADS, tm), 0)
    packed_t = _stack_terms(cum_t * LOG2E, jnp.where(row8 == 0, 1.0, 0.0), LANES)
    aug_k = lax.dot_general(packed_t, paugk_ref[...], TN_DIMS, preferred_element_type=F32)
    aug_qt = jnp.dot(paugqt_ref[...], packed_t, preferred_element_type=F32)

    qk = jnp.dot(h, wa_ref[:, 0:2 * D_FOX], preferred_element_type=F32)
    lane = lax.broadcasted_iota(jnp.int32, (tm, LANES), 1)
    row = lax.broadcasted_iota(jnp.int32, (LANES, tm), 0)
    q_t = [qk[:, pr * LANES:(pr + 1) * LANES].T for pr in range(FOX_HEADS // 2)]
    for hd in range(FOX_HEADS):
        data_lo, aug_lo = _aug_lanes(hd)
        src = (hd // 2) * LANES
        out = slice(hd * LANES, (hd + 1) * LANES)
        is_data = (lane >= data_lo) & (lane < data_lo + half)
        is_aug = (lane >= aug_lo) & (lane < aug_lo + AUG_COLS)
        kp_ref[0, :, out] = jnp.where(is_data, qk[:, D_FOX + src:D_FOX + src + LANES],
                                      jnp.where(is_aug, aug_k, 0.0)).astype(BF16)
        is_data_t = (row >= data_lo) & (row < data_lo + half)
        is_aug_t = (row >= aug_lo) & (row < aug_lo + AUG_COLS)
        qpt_ref[0, 0, out, :] = jnp.where(is_data_t, q_t[hd // 2],
                                          jnp.where(is_aug_t, aug_qt, 0.0)).astype(BF16)

    vt = jnp.dot(h, wa_ref[:, 2 * D_FOX:3 * D_FOX], preferred_element_type=F32).T.astype(BF16)
    for hd in range(FOX_HEADS):
        fvt_ref[0, 0, hd * V_ROWS:hd * V_ROWS + half, :] = vt[hd * half:(hd + 1) * half]
        fvt_ref[0, 0, hd * V_ROWS + half:(hd + 1) * V_ROWS, :] = jnp.ones((BF16_ROWS, tm), BF16)


def _inproj(x1, mod3, wa, wfl, bfl, wr, triu, paugk, paugqt, cos2, sin2, intra, qdec, kdec, cdec, gng, gnb):
    b, s, d = x1.shape
    tm = TOKEN_TILE
    nt = s // tm
    c = RET_CHUNK
    in_specs = [
        _tile_spec(tm, d), pl.BlockSpec((1, N_MOD, d), lambda b, t: (b, 0, 0)),
        _const_spec((d, 3 * D_FOX)), _const_spec((d, LANES)), _const_spec((1, LANES)),
        _const_spec((d, 4 * D_RET)), _const_spec((tm, tm)), _const_spec((LANES, LANES)), _const_spec((LANES, LANES)),
        pl.BlockSpec((tm, LANES), lambda b, t: (t, 0)), pl.BlockSpec((tm, LANES), lambda b, t: (t, 0)),
        _const_spec((RET_HEADS, c, c)), _const_spec((RET_HEADS, c, LANES)), _const_spec((RET_HEADS, c, LANES)),
        _const_spec((RET_HEADS, 1, LANES)), _const_spec((1, D_RET)), _const_spec((1, D_RET)),
    ]
    out_specs = [
        pl.BlockSpec((1, 1, FOX_HEADS * LANES, tm), lambda b, t: (b, t, 0, 0)), _tile_spec(tm, FOX_HEADS * LANES),
        pl.BlockSpec((1, 1, FOX_HEADS * V_ROWS, tm), lambda b, t: (b, t, 0, 0)),
        _tile_spec(tm, D_RET),
    ]
    out_shape = [
        jax.ShapeDtypeStruct((b, nt, FOX_HEADS * LANES, tm), BF16), jax.ShapeDtypeStruct((b, s, FOX_HEADS * LANES), BF16),
        jax.ShapeDtypeStruct((b, nt, FOX_HEADS * V_ROWS, tm), BF16),
        jax.ShapeDtypeStruct((b, s, D_RET), BF16),
    ]
    return _pallas_call(
        _inproj_kernel,
        name="inproj",
        operands=[x1, mod3, wa, wfl, bfl, wr, triu, paugk, paugqt, cos2, sin2, intra, qdec, kdec, cdec, gng, gnb],
        grid=(b, nt),
        in_specs=in_specs, out_specs=out_specs, out_shape=out_shape,
        scratch_shapes=[pltpu.VMEM((FOX_HEADS, LANES), F32),
                        pltpu.VMEM((RET_HEADS, RET_HEAD_DIM, RET_HEAD_DIM), F32)])


def _fox_kernel(qpt_ref, kp_ref, fvt_ref, *rest):
    n_cast = (len(rest) - 6) // 2
    cast_in, o_ref, cast_out = rest[:n_cast], rest[n_cast], rest[n_cast + 1:2 * n_cast + 1]
    s_ref, p_ref, acc_ref, mask_ref, stat_ref = rest[2 * n_cast + 1:]
    for src, dst in zip(cast_in, cast_out):
        dst[...] = src[...].astype(BF16)
    _fox_attention(qpt_ref, kp_ref, fvt_ref, o_ref, s_ref, p_ref, acc_ref, mask_ref, stat_ref)


def _fox_attention(qpt_ref, kp_ref, fvt_ref, o_ref, s_ref, p_ref, acc_ref, mask_ref, stat_ref):
    t = ATTN_TILE
    nq = qpt_ref.shape[1]
    half = FOX_HEAD_DIM

    k_pos = lax.broadcasted_iota(jnp.int32, (t, t), 0)
    q_pos = lax.broadcasted_iota(jnp.int32, (t, t), 1)
    mask_ref[0] = jnp.zeros((t, t), F32)
    mask_ref[1] = jnp.where(k_pos <= q_pos, 0.0, -jnp.inf)
    acc_ref[...] = jnp.zeros_like(acc_ref)
    stat_ref[...] = jnp.zeros_like(stat_ref)

    unroll = ATTN_UNROLL
    row_m, row_alpha_new, row_alpha_old = 1, 1 + unroll, 1 + 2 * unroll
    steps = [(q, k) for q in range(nq) for k in range(q + 1)]
    n_groups = len(steps) // unroll
    assert len(steps) % unroll == 0 and n_groups >= 4 and 1 + 3 * unroll <= stat_ref.shape[1]

    def stat(hh, row):
        return stat_ref[hh, row:row + 1, :]

    def stage_a(hh, group):
        alpha_new = [stat(hh, row_alpha_new + u) for u in range(unroll)]
        m = stat(hh, 0)
        for u, (qa, ka) in enumerate(group):
            static = isinstance(ka, int)
            krows = pl.ds(ka * t, t) if static else pl.ds(pl.multiple_of(ka * t, t), t)
            diag = int(ka == qa) if static else (ka == qa).astype(jnp.int32)
            s = jnp.dot(kp_ref[0, krows, hh * LANES:(hh + 1) * LANES],
                        qpt_ref[0, qa, hh * LANES:(hh + 1) * LANES, :],
                        preferred_element_type=F32) + mask_ref[diag]
            s_ref[u * ATTN_HEADS + hh] = s
            m_prev = jnp.where(ka == 0, -jnp.inf, m)
            m = jnp.maximum(m_prev, jnp.max(s, axis=0, keepdims=True))
            stat_ref[hh, row_m + u:row_m + u + 1, :] = m
            stat_ref[hh, row_alpha_new + u:row_alpha_new + u + 1, :] = jnp.exp2(m_prev - m)
            stat_ref[hh, row_alpha_old + u:row_alpha_old + u + 1, :] = alpha_new[u]
        stat_ref[hh, 0:1, :] = m

    def stage_b(hh):
        for u in range(unroll):
            buf = u * ATTN_HEADS + hh
            p_ref[buf] = jnp.exp2(s_ref[buf] - stat(hh, row_m + u)).astype(BF16)

    def stage_c(hh, group, alpha_row):
        acc = acc_ref[hh]
        for u, (qc, kc) in enumerate(group):
            vt = fvt_ref[0, kc, hh * V_ROWS:(hh + 1) * V_ROWS, :]
            acc = stat(hh, alpha_row + u) * acc + jnp.dot(vt, p_ref[u * ATTN_HEADS + hh],
                                                          preferred_element_type=F32)
            if isinstance(kc, int):
                if kc != qc:
                    continue
                tile = qc
            else:
                tile = jnp.where(kc == qc, qc, nq)
            o_ref[0, tile, hh * half:(hh + 1) * half, :] = (acc[:half] / acc[half:half + 1]).astype(BF16)
        acc_ref[hh] = acc

    def group(g):
        return steps[g * unroll:(g + 1) * unroll]

    def next_step(q, k):
        wrap = k >= q
        return jnp.where(wrap, q + 1, q), jnp.where(wrap, 0, k + 1)

    def body(j, carry):
        a_first, b_group, c_group = carry
        a_group = []
        nxt = a_first
        for _ in range(unroll):
            a_group.append(nxt)
            nxt = next_step(*nxt)
        for hh in range(ATTN_HEADS):
            stage_c(hh, c_group, row_alpha_old)
            stage_b(hh)
            stage_a(hh, a_group)
        return nxt, tuple(a_group), b_group

    as_i32 = lambda grp: tuple((jnp.int32(q), jnp.int32(k)) for q, k in grp)
    for hh in range(ATTN_HEADS):
        stage_a(hh, group(0))
    for hh in range(ATTN_HEADS):
        stage_b(hh)
        stage_a(hh, group(1))
    lax.fori_loop(2, n_groups, body, (as_i32(group(2))[0], as_i32(group(1)), as_i32(group(0))))
    for hh in range(ATTN_HEADS):
        stage_c(hh, group(n_groups - 2), row_alpha_old)
        stage_b(hh)
    for hh in range(ATTN_HEADS):
        stage_c(hh, group(n_groups - 1), row_alpha_new)


def _fox(qpt, kp, fvt, f32_weights):
    b, s, _ = kp.shape
    t = ATTN_TILE
    nt = s // t
    g = ATTN_HEADS
    groups = FOX_HEADS // g
    steps = b * groups
    slab_specs = []
    for w in f32_weights:
        assert w.shape[0] % (steps * BF16_ROWS) == 0
        slab_specs.append(pl.BlockSpec((w.shape[0] // steps, w.shape[1]), lambda b, p: (b * groups + p, 0)))
    return _pallas_call(
        _fox_kernel,
        name="fox", operands=[qpt, kp, fvt, *f32_weights],
        grid=(b, groups),
        in_specs=[pl.BlockSpec((1, nt, g * LANES, t), lambda b, p: (b, 0, p, 0)),
                  pl.BlockSpec((1, s, g * LANES), lambda b, p: (b, 0, p)),
                  pl.BlockSpec((1, nt, g * V_ROWS, t), lambda b, p: (b, 0, p, 0))] + slab_specs,
        out_specs=[pl.BlockSpec((1, nt + 1, g * FOX_HEAD_DIM, t), lambda b, p: (b, 0, p, 0))] + slab_specs,
        out_shape=[jax.ShapeDtypeStruct((b, nt + 1, D_FOX, t), BF16)]
        + [jax.ShapeDtypeStruct(w.shape, BF16) for w in f32_weights],
        scratch_shapes=[pltpu.VMEM((ATTN_UNROLL * g, t, t), F32), pltpu.VMEM((ATTN_UNROLL * g, t, t), BF16),
                        pltpu.VMEM((g, V_ROWS, t), F32), pltpu.VMEM((2, t, t), F32),
                        pltpu.VMEM((g, 8 * pl.cdiv(1 + 3 * ATTN_UNROLL, 8), t), F32)])


def _aug_placement():
    pk = np.zeros((LANES, LANES), np.float32)
    pq = np.zeros((LANES, LANES), np.float32)
    for head in range(FOX_HEADS):
        _, base = _aug_lanes(head)
        for j in range(3):
            pk[24, base + j] = 1.0
            pk[8 * j + head, base + 3 + j] = -1.0
            pq[base + j, 8 * j + head] = 1.0
            pq[base + 3 + j, 24] = 1.0
    return jnp.asarray(pk, BF16), jnp.asarray(pq, BF16)


def _rotary_tables(seq):
    half = RET_HEAD_DIM // 2
    inv_freq = ROPE_BASE ** (-np.arange(half, dtype=np.float64) / half)
    ang = np.arange(seq, dtype=np.float64)[:, None] * inv_freq[None, :]
    cos, sin = np.cos(ang), np.sin(ang)
    return (jnp.asarray(np.concatenate([cos, cos], axis=-1), F32),
            jnp.asarray(np.concatenate([-sin, sin], axis=-1), F32))


def _retention_tables():
    c = RET_CHUNK
    log_gamma = np.log1p(-np.power(2.0, -5.0 - np.arange(RET_HEADS, dtype=np.float64)))
    idx = np.arange(c, dtype=np.float64)
    diff = idx[:, None] - idx[None, :]
    intra = np.where(diff >= 0, np.exp(log_gamma[:, None, None] * np.maximum(diff, 0.0)), 0.0)
    qdec = np.exp(log_gamma[:, None] * (idx + 1.0))[..., None]
    kdec = np.exp(log_gamma[:, None] * (c - 1.0 - idx))[..., None]
    cdec = np.exp(log_gamma * c)[:, None, None]
    bc = lambda a: jnp.asarray(np.broadcast_to(a, a.shape[:2] + (LANES,)), F32)
    return jnp.asarray(intra, F32), bc(qdec), bc(kdec), bc(cdec)


def kernel(x, c, w_ada, b_ada, ffn1_w_gate, ffn1_w_up, ffn1_w_down, ln1_g, ln1_b, w_in, fox_b_f, ret_gn_g,
           ret_gn_b, w_o, ln2_g, ln2_b, ffn2_w_gate, ffn2_w_up, ffn2_w_down, ln3_g, ln3_b):
    batch, seq, d = x.shape
    assert d == D_MODEL and seq % FFN_TILE == 0 and FFN_TILE % TOKEN_TILE == 0 and w_ada.shape[0] == DEPTH
    assert sum(FFN1_GROUPS) == FFN_TILE and sum(FFN2_GROUPS) == FFN_TILE and all(
        _ffn_group_rows(FFN2_GROUPS, i).start // ATTN_TILE == (_ffn_group_rows(FFN2_GROUPS, i).stop - 1) // ATTN_TILE
        for i in range(len(FFN2_GROUPS)))
    layer = 0

    c_pad = jnp.pad(c, ((0, 8 - batch), (0, 0)))
    mod3 = _ada(c_pad, w_ada[layer], b_ada[layer][None, :])[:batch].reshape(batch, N_MOD, d)

    w = w_in[layer]
    o_fl = 3 * D_FOX
    o_r = o_fl + FOX_HEADS
    q_scale = LOG2E * FOX_HEAD_DIM ** -0.5
    x1, wa, wr = _ffn(x, mod3, ffn1_w_gate[layer], ffn1_w_up[layer], ffn1_w_down[layer],
                      ln1_g[layer][None, :], ln1_b[layer][None, :], 0, w, q_scale)

    wfl = jnp.pad(w[:, o_fl:o_r], ((0, 0), (0, LANES - FOX_HEADS))).astype(BF16)
    bfl = jnp.pad(fox_b_f[layer], (0, LANES - FOX_HEADS))[None, :]
    triu = jnp.asarray(np.triu(np.ones((TOKEN_TILE, TOKEN_TILE), np.float32)), BF16)
    paugk, paugqt = _aug_placement()
    cos2, sin2 = _rotary_tables(seq)
    intra, qdec, kdec, cdec = _retention_tables()
    qpt, kp, fvt, ret = _inproj(x1, mod3, wa, wfl, bfl, wr, triu, paugk, paugqt, cos2, sin2,
                               intra, qdec, kdec, cdec, ret_gn_g[layer][None, :], ret_gn_b[layer][None, :])

    foxt, wo, wg2, wu2, wd2 = _fox(qpt, kp, fvt, [w_o[layer], ffn2_w_gate[layer], ffn2_w_up[layer],
                                                  ffn2_w_down[layer]])

    return _mix_ffn(foxt, ret, x1, mod3, wo, ln2_g[layer][None, :], ln2_b[layer][None, :],
                    wg2, wu2, wd2, ln3_g[layer][None, :], ln3_b[layer][None, :])
```
